```python
import math
import jax
import jax.numpy as jnp
from jax import lax
import numpy as np

D_MODEL = 2048
BATCH = 4
SEQ = 2048
DEPTH = 4
DEC_BATCH = 8
DEC_SEQ = 8
PAST_LEN = 16384
PAGE_SIZE = 128

N_MIXERS = 3
N_SB_LAYERS = (DEPTH + 2) // N_MIXERS
N_GDN_LAYERS = (DEPTH + 1) // N_MIXERS
N_MOBA_LAYERS = DEPTH // N_MIXERS

HEAD_DIM = 128
N_HEADS = D_MODEL // HEAD_DIM
N_KV_HEADS = N_HEADS // 4
GROUP = N_HEADS // N_KV_HEADS
ATTN_WIDTH = N_HEADS * HEAD_DIM
KV_WIDTH = N_KV_HEADS * HEAD_DIM
ATTN_IN = 2 * ATTN_WIDTH + 2 * KV_WIDTH
SB_QBLOCK = 128

GDN_HEAD_DIM = 128
GDN_K_HEADS = D_MODEL // GDN_HEAD_DIM
GDN_V_HEADS = 2 * GDN_K_HEADS
GDN_KEY_WIDTH = GDN_K_HEADS * GDN_HEAD_DIM
GDN_VAL_WIDTH = GDN_V_HEADS * GDN_HEAD_DIM
GDN_CONV_CH = 2 * GDN_KEY_WIDTH + GDN_VAL_WIDTH
GDN_CONV = 4
GDN_CHUNK = 64
GDN_IN = GDN_CONV_CH + GDN_VAL_WIDTH + 2 * GDN_V_HEADS

MOBA_BLOCK = 256
MOBA_TOPK = 3
MOBA_QCHUNK = 8
REL_BUCKETS = 32
REL_MAX_DIST = 4096

EPS = 1e-6
NEG = -1e30

kernel_name = 'hybrid_stickbreak_gdn_moba_decode_step'


def rms_norm(x, g):
    x32 = x.astype(jnp.float32)
    y = x32 * lax.rsqrt(jnp.mean(x32 * x32, axis=-1, keepdims=True) + EPS)
    return (y * g.astype(jnp.float32)).astype(x.dtype)


def l2_norm(x):
    x32 = x.astype(jnp.float32)
    return x32 * lax.rsqrt(jnp.sum(x32 * x32, axis=-1, keepdims=True) + EPS)


def gather_pages(pool, page_table):
    g = pool[page_table]
    return g.reshape(g.shape[0], g.shape[1] * g.shape[2], *g.shape[3:])


def stick_breaking_attention(q, k, v, q_start):
    b, tq = q.shape[:2]
    qg = q.reshape(b, tq, N_KV_HEADS, GROUP, HEAD_DIM)
    scale = HEAD_DIM ** -0.5
    outs = []
    for qs in range(0, tq, SB_QBLOCK):
        qe = min(qs + SB_QBLOCK, tq)
        n_keys = q_start + qe
        kb, vb = k[:, :n_keys], v[:, :n_keys]
        z = jnp.einsum('bqhgd,bshd->bhgqs', qg[:, qs:qe], kb).astype(jnp.float32) * scale
        t_pos = q_start + jnp.arange(qs, qe)
        past = jnp.arange(n_keys)[None, :] < t_pos[:, None]
        log_keep = jnp.where(past, jax.nn.log_sigmoid(-z), 0.0)
        later = lax.cumsum(log_keep, axis=4, reverse=True) - log_keep
        w = jnp.where(past, jnp.exp(jax.nn.log_sigmoid(z) + later), 0.0)
        o = jnp.einsum('bhgqs,bshd->bqhgd', w.astype(vb.dtype), vb)
        outs.append(o.reshape(b, qe - qs, ATTN_WIDTH))
    return jnp.concatenate(outs, axis=1)


def sb_mixer(h, past_k, past_v, w_in, w_out):
    b, t, _ = h.shape
    q, k, v, gate = jnp.split(h @ w_in, [ATTN_WIDTH, ATTN_WIDTH + KV_WIDTH, ATTN_WIDTH + 2 * KV_WIDTH], axis=-1)
    q = q.reshape(b, t, N_HEADS, HEAD_DIM)
    k = k.reshape(b, t, N_KV_HEADS, HEAD_DIM)
    v = v.reshape(b, t, N_KV_HEADS, HEAD_DIM)
    if past_k is None:
        k_all, v_all, start = k, v, 0
    else:
        k_all = jnp.concatenate([past_k, k], axis=1)
        v_all = jnp.concatenate([past_v, v], axis=1)
        start = past_k.shape[1]
    o = stick_breaking_attention(q, k_all, v_all, start)
    return (o * jax.nn.silu(gate)) @ w_out, k, v


def causal_conv(u, buf, w):
    full = jnp.concatenate([buf, u], axis=1)
    y = lax.conv_general_dilated(full, w[:, None, :], window_strides=(1,), padding='VALID',
                                 dimension_numbers=('NWC', 'WIO', 'NWC'),
                                 feature_group_count=u.shape[-1])
    return y, full[:, -(GDN_CONV - 1):]


def gated_delta_rule(q, k, v, g, beta, s0):
    b, t, h, dk = q.shape
    dv = v.shape[-1]
    c = min(GDN_CHUNK, t)
    pad = (-t) % c
    n = (t + pad) // c

    def prep(x):
        x = jnp.pad(x, [(0, 0), (0, pad)] + [(0, 0)] * (x.ndim - 2))
        x = jnp.swapaxes(x, 1, 2)
        return x.reshape(b, h, n, c, *x.shape[3:])

    q, k, v, g, beta = (prep(a) for a in (q, k, v, g, beta))
    g = jnp.cumsum(g, axis=-1)
    tri = jnp.tril(jnp.ones((c, c), bool))
    strict = jnp.tril(jnp.ones((c, c), bool), -1)
    decay = jnp.exp(jnp.where(tri, g[..., :, None] - g[..., None, :], -jnp.inf))
    kb = k * beta[..., None]
    lower = jnp.where(strict, jnp.einsum('bhncd,bhnsd->bhncs', kb, k) * decay, 0.0)
    a = lower + jnp.eye(c, dtype=lower.dtype)
    rhs = jnp.concatenate([v * beta[..., None], kb * jnp.exp(g)[..., None]], axis=-1)
    sol = lax.linalg.triangular_solve(a, rhs, left_side=True, lower=True, unit_diagonal=True)
    u, w = sol[..., :dv], sol[..., dv:]
    intra = jnp.einsum('bhncd,bhnsd->bhncs', q, k) * decay
    q_dec = q * jnp.exp(g)[..., None]
    k_dec = k * jnp.exp(g[..., -1:] - g)[..., None]
    g_last = jnp.exp(g[..., -1])

    def step(s, xs):
        u_i, w_i, intra_i, qd_i, kd_i, gl_i = xs
        v_new = u_i - jnp.einsum('bhcd,bhde->bhce', w_i, s)
        o_i = jnp.einsum('bhcd,bhde->bhce', qd_i, s) + jnp.einsum('bhcs,bhse->bhce', intra_i, v_new)
        s = s * gl_i[..., None, None] + jnp.einsum('bhcd,bhce->bhde', kd_i, v_new)
        return s, o_i

    xs = tuple(jnp.moveaxis(a_, 2, 0) for a_ in (u, w, intra, q_dec, k_dec, g_last))
    s_fin, o = lax.scan(step, s0, xs)
    o = jnp.moveaxis(o, 0, 2).reshape(b, h, n * c, dv)[:, :, :t]
    return jnp.swapaxes(o, 1, 2), s_fin


def gdn_mixer(h, conv_buf, s0, w_in, conv_w, a_log, dt_bias, o_norm, w_out):
    b, t, _ = h.shape
    qkv, z, bet, dec = jnp.split(h @ w_in, [GDN_CONV_CH, GDN_CONV_CH + GDN_VAL_WIDTH,
                                            GDN_CONV_CH + GDN_VAL_WIDTH + GDN_V_HEADS], axis=-1)
    qkv, new_buf = causal_conv(qkv, conv_buf, conv_w)
    qkv = jax.nn.silu(qkv)
    q, k, v = jnp.split(qkv, [GDN_KEY_WIDTH, 2 * GDN_KEY_WIDTH], axis=-1)
    rep = GDN_V_HEADS // GDN_K_HEADS
    q = jnp.repeat(l2_norm(q.reshape(b, t, GDN_K_HEADS, GDN_HEAD_DIM)) * GDN_HEAD_DIM ** -0.5, rep, axis=2)
    k = jnp.repeat(l2_norm(k.reshape(b, t, GDN_K_HEADS, GDN_HEAD_DIM)), rep, axis=2)
    v = v.reshape(b, t, GDN_V_HEADS, GDN_HEAD_DIM).astype(jnp.float32)
    beta = jax.nn.sigmoid(bet.astype(jnp.float32))
    g = -jnp.exp(a_log.astype(jnp.float32)) * jax.nn.softplus(dec.astype(jnp.float32) + dt_bias.astype(jnp.float32))
    o, s_new = gated_delta_rule(q, k, v, g, beta, s0.astype(jnp.float32))
    o = rms_norm(o, o_norm).reshape(b, t, GDN_VAL_WIDTH).astype(h.dtype)
    return (o * jax.nn.silu(z)) @ w_out, new_buf, s_new.astype(s0.dtype)


def rel_bucket(dist):
    max_exact = REL_BUCKETS // 2
    n = jnp.maximum(dist, 0)
    large = max_exact + (jnp.log(jnp.maximum(n, 1).astype(jnp.float32) / max_exact)
                         / math.log(REL_MAX_DIST / max_exact) * (REL_BUCKETS - max_exact)).astype(jnp.int32)
    large = jnp.minimum(large, REL_BUCKETS - 1)
    return jnp.where(n < max_exact, n, large)


def moba_attention(q, k, v, q_start, rel_bias):
    b, tq = q.shape[:2]
    sk = k.shape[1]
    n_blocks = -(-sk // MOBA_BLOCK)
    qc = min(MOBA_QCHUNK, tq)
    n_chunks = -(-tq // qc)
    win = MOBA_BLOCK + qc
    pad_k = n_blocks * MOBA_BLOCK + win - sk
    kp = jnp.pad(k, ((0, 0), (0, pad_k), (0, 0), (0, 0)))
    vp = jnp.pad(v, ((0, 0), (0, pad_k), (0, 0), (0, 0)))
    kb = kp[:, :n_blocks * MOBA_BLOCK].reshape(b, n_blocks, MOBA_BLOCK, N_KV_HEADS, HEAD_DIM)
    vb = vp[:, :n_blocks * MOBA_BLOCK].reshape(b, n_blocks, MOBA_BLOCK, N_KV_HEADS, HEAD_DIM)
    means = jnp.mean(kb.astype(jnp.float32), axis=2)
    kb = kb.transpose(0, 3, 1, 2, 4)
    vb = vb.transpose(0, 3, 1, 2, 4)
    top = min(MOBA_TOPK, n_blocks)
    qg = jnp.pad(q, ((0, 0), (0, n_chunks * qc - tq), (0, 0), (0, 0)))
    qg = jnp.moveaxis(qg.reshape(b, n_chunks, qc, N_KV_HEADS, GROUP, HEAD_DIM), 1, 0)
    pos = (q_start + jnp.arange(n_chunks * qc)).reshape(n_chunks, qc)
    bias_hg = rel_bias.reshape(REL_BUCKETS, N_KV_HEADS, GROUP)
    bi = jnp.arange(b)[:, None, None, None, None]
    hi = jnp.arange(N_KV_HEADS)[None, None, :, None, None]
    gi = jnp.arange(GROUP)[None, None, None, :, None, None]
    scale = HEAD_DIM ** -0.5
    jn = jnp.arange(n_blocks)

    def chunk(args):
        qx, px = args
        blk = px // MOBA_BLOCK
        blk5 = blk[None, :, None, None, None]
        gate = jnp.einsum('bqhgd,bnhd->bqhgn', qx.astype(jnp.float32), means)
        gate = jnp.where(jn[None, None, None, None, :] < blk5, gate, -jnp.inf)
        _, sel = lax.top_k(gate, top)
        sel_ok = sel < blk5
        kg = kb[bi, hi, sel]
        vg = vb[bi, hi, sel]
        s_pos = sel[..., None] * MOBA_BLOCK + jnp.arange(MOBA_BLOCK)
        lp = jnp.einsum('bqhgd,bqhgksd->bqhgks', qx, kg).astype(jnp.float32) * scale
        lp = lp + bias_hg[rel_bucket(px[None, :, None, None, None, None] - s_pos), hi[..., None], gi]
        lp = jnp.where(sel_ok[..., None], lp, NEG).reshape(b, qc, N_KV_HEADS, GROUP, top * MOBA_BLOCK)
        start = (px[0] // MOBA_BLOCK) * MOBA_BLOCK
        kw = lax.dynamic_slice_in_dim(kp, start, win, axis=1)
        vw = lax.dynamic_slice_in_dim(vp, start, win, axis=1)
        w_pos = start + jnp.arange(win)
        lo = jnp.einsum('bqhgd,bwhd->bqhgw', qx, kw).astype(jnp.float32) * scale
        bias_o = jnp.transpose(bias_hg[rel_bucket(px[:, None] - w_pos[None, :])], (0, 2, 3, 1))
        ok_o = (w_pos[None, :] // MOBA_BLOCK == blk[:, None]) & (w_pos[None, :] <= px[:, None])
        lo = jnp.where(ok_o[None, :, None, None, :], lo + bias_o[None], NEG)
        p = jax.nn.softmax(jnp.concatenate([lp, lo], axis=-1), axis=-1)
        pp = p[..., :top * MOBA_BLOCK].reshape(b, qc, N_KV_HEADS, GROUP, top, MOBA_BLOCK).astype(v.dtype)
        po = p[..., top * MOBA_BLOCK:].astype(v.dtype)
        return (jnp.einsum('bqhgks,bqhgksd->bqhgd', pp, vg)
                + jnp.einsum('bqhgw,bwhd->bqhgd', po, vw))

    o = lax.map(chunk, (qg, pos))
    return jnp.moveaxis(o, 0, 1).reshape(b, n_chunks * qc, ATTN_WIDTH)[:, :tq]


def moba_mixer(h, past_k, past_v, w_in, q_norm, k_norm, rel_bias, w_out):
    b, t, _ = h.shape
    q, k, v, gate = jnp.split(h @ w_in, [ATTN_WIDTH, ATTN_WIDTH + KV_WIDTH, ATTN_WIDTH + 2 * KV_WIDTH], axis=-1)
    q = rms_norm(q.reshape(b, t, N_HEADS, HEAD_DIM), q_norm)
    k = rms_norm(k.reshape(b, t, N_KV_HEADS, HEAD_DIM), k_norm)
    v = v.reshape(b, t, N_KV_HEADS, HEAD_DIM)
    if past_k is None:
        k_all, v_all, start = k, v, 0
    else:
        k_all = jnp.concatenate([past_k, k], axis=1)
        v_all = jnp.concatenate([past_v, v], axis=1)
        start = past_k.shape[1]
    o = moba_attention(q, k_all, v_all, start, rel_bias)
    return (o * jax.nn.silu(gate)) @ w_out, k, v


def setup_inputs(seed: int = 0) -> dict:
    key = jax.random.key(seed)
    ks = jax.random.split(key, 24)
    f32 = jnp.float32
    n_pages = PAST_LEN // PAGE_SIZE
    n_phys = (DEC_BATCH * n_pages * 5) // 4

    def nrm(k, shape, scale=1.0):
        return jax.random.normal(k, shape, f32) * scale

    def gain(k, shape):
        return 1.0 + 0.02 * jax.random.normal(k, shape, f32)

    perm = jax.random.permutation(ks[8], n_phys)
    page_table = perm[:DEC_BATCH * n_pages].reshape(DEC_BATCH, n_pages).astype(jnp.int32)
    dt = jnp.exp(jax.random.uniform(ks[15], (N_GDN_LAYERS, GDN_V_HEADS), f32, math.log(1e-3), math.log(1e-1)))
    pool = (n_phys, PAGE_SIZE, N_KV_HEADS, HEAD_DIM)
    return {
        'x_prompt': nrm(ks[0], (BATCH, SEQ, D_MODEL)),
        'x_sample': nrm(ks[1], (DEC_BATCH, DEC_SEQ, D_MODEL)),
        'cache_sb_k': nrm(ks[2], (N_SB_LAYERS,) + pool),
        'cache_sb_v': nrm(ks[3], (N_SB_LAYERS,) + pool),
        'state_gdn_conv': nrm(ks[4], (N_GDN_LAYERS, DEC_BATCH, GDN_CONV - 1, GDN_CONV_CH)),
        'state_gdn_rec': nrm(ks[5], (N_GDN_LAYERS, DEC_BATCH, GDN_V_HEADS, GDN_HEAD_DIM, GDN_HEAD_DIM), 0.5),
        'cache_moba_k': nrm(ks[6], (N_MOBA_LAYERS,) + pool),
        'cache_moba_v': nrm(ks[7], (N_MOBA_LAYERS,) + pool),
        'page_table': page_table,
        'norm_g': gain(ks[9], (DEPTH, D_MODEL)),
        'sb_w_in': nrm(ks[10], (N_SB_LAYERS, D_MODEL, ATTN_IN), D_MODEL ** -0.5),
        'sb_w_out': nrm(ks[11], (N_SB_LAYERS, ATTN_WIDTH, D_MODEL), ATTN_WIDTH ** -0.5),
        'gdn_w_in': nrm(ks[12], (N_GDN_LAYERS, D_MODEL, GDN_IN), D_MODEL ** -0.5),
        'gdn_conv_w': nrm(ks[13], (N_GDN_LAYERS, GDN_CONV, GDN_CONV_CH), GDN_CONV ** -0.5),
        'gdn_a_log': jnp.log(jax.random.uniform(ks[14], (N_GDN_LAYERS, GDN_V_HEADS), f32, 1.0, 16.0)),
        'gdn_dt_bias': dt + jnp.log(-jnp.expm1(-dt)),
        'gdn_o_norm': gain(ks[16], (N_GDN_LAYERS, GDN_HEAD_DIM)),
        'gdn_w_out': nrm(ks[17], (N_GDN_LAYERS, GDN_VAL_WIDTH, D_MODEL), GDN_VAL_WIDTH ** -0.5),
        'moba_w_in': nrm(ks[18], (N_MOBA_LAYERS, D_MODEL, ATTN_IN), D_MODEL ** -0.5),
        'moba_q_norm': gain(ks[19], (N_MOBA_LAYERS, HEAD_DIM)),
        'moba_k_norm': gain(ks[20], (N_MOBA_LAYERS, HEAD_DIM)),
        'moba_w_out': nrm(ks[21], (N_MOBA_LAYERS, ATTN_WIDTH, D_MODEL), ATTN_WIDTH ** -0.5),
        'rel_bias': nrm(ks[22], (REL_BUCKETS, N_HEADS), 0.5),
    }


def reference(x_prompt, x_sample, cache_sb_k, cache_sb_v, state_gdn_conv, state_gdn_rec,
              cache_moba_k, cache_moba_v, page_table, norm_g, sb_w_in, sb_w_out,
              gdn_w_in, gdn_conv_w, gdn_a_log, gdn_dt_bias, gdn_o_norm, gdn_w_out,
              moba_w_in, moba_q_norm, moba_k_norm, moba_w_out, rel_bias):
    yp, ys = x_prompt, x_sample
    bp = x_prompt.shape[0]
    sb_kp, sb_vp, sb_ks, sb_vs = [], [], [], []
    gdn_cp, gdn_sp, gdn_cs, gdn_ss = [], [], [], []
    mb_kp, mb_vp, mb_ks, mb_vs = [], [], [], []
    for layer in range(DEPTH):
        kind = layer % N_MIXERS
        j = layer // N_MIXERS
        hp = rms_norm(yp, norm_g[layer])
        hs = rms_norm(ys, norm_g[layer])
        if kind == 0:
            dp, kp_, vp_ = sb_mixer(hp, None, None, sb_w_in[j], sb_w_out[j])
            ds, ks_, vs_ = sb_mixer(hs, gather_pages(cache_sb_k[j], page_table),
                                    gather_pages(cache_sb_v[j], page_table), sb_w_in[j], sb_w_out[j])
            sb_kp.append(kp_); sb_vp.append(vp_); sb_ks.append(ks_); sb_vs.append(vs_)
        elif kind == 1:
            buf0 = jnp.zeros((bp, GDN_CONV - 1, GDN_CONV_CH), hp.dtype)
            s0 = jnp.zeros((bp,) + state_gdn_rec.shape[2:], state_gdn_rec.dtype)
            dp, cp_, sp_ = gdn_mixer(hp, buf0, s0, gdn_w_in[j], gdn_conv_w[j], gdn_a_log[j],
                                     gdn_dt_bias[j], gdn_o_norm[j], gdn_w_out[j])
            ds, cs_, ss_ = gdn_mixer(hs, state_gdn_conv[j], state_gdn_rec[j], gdn_w_in[j], gdn_conv_w[j],
                                     gdn_a_log[j], gdn_dt_bias[j], gdn_o_norm[j], gdn_w_out[j])
            gdn_cp.append(cp_); gdn_sp.append(sp_); gdn_cs.append(cs_); gdn_ss.append(ss_)
        else:
            dp, kp_, vp_ = moba_mixer(hp, None, None, moba_w_in[j], moba_q_norm[j], moba_k_norm[j],
                                      rel_bias, moba_w_out[j])
            ds, ks_, vs_ = moba_mixer(hs, gather_pages(cache_moba_k[j], page_table),
                                      gather_pages(cache_moba_v[j], page_table), moba_w_in[j],
                                      moba_q_norm[j], moba_k_norm[j], rel_bias, moba_w_out[j])
            mb_kp.append(kp_); mb_vp.append(vp_); mb_ks.append(ks_); mb_vs.append(vs_)
        yp = yp + dp
        ys = ys + ds
    return (yp, ys,
            jnp.stack(sb_kp), jnp.stack(sb_vp), jnp.stack(sb_ks), jnp.stack(sb_vs),
            jnp.stack(gdn_cp), jnp.stack(gdn_sp), jnp.stack(gdn_cs), jnp.stack(gdn_ss),
            jnp.stack(mb_kp), jnp.stack(mb_vp), jnp.stack(mb_ks), jnp.stack(mb_vs))
```

```python
import functools
import math

import jax
import jax.numpy as jnp
from jax import lax
from jax.experimental import pallas as pl
from jax.experimental.pallas import tpu as pltpu

F32 = jnp.float32
BF16 = jnp.bfloat16
HI = lax.Precision.HIGHEST

LANES = 128
SUBLANES = 8
VMEM_LIMIT = 56 * 1024 * 1024

HEAD_DIM = 128
N_HEADS = 16
N_KV_HEADS = 4
GROUP = N_HEADS // N_KV_HEADS
ATTN_WIDTH = N_HEADS * HEAD_DIM
KV_WIDTH = N_KV_HEADS * HEAD_DIM
ATTN_IN = 2 * ATTN_WIDTH + 2 * KV_WIDTH
Q_COL = 0
K_COL = ATTN_WIDTH // HEAD_DIM
V_COL = K_COL + N_KV_HEADS
GATE_COL = (ATTN_WIDTH + 2 * KV_WIDTH) // (GROUP * HEAD_DIM)
PAGE_SIZE = 128

GDN_HEAD_DIM = 128
GDN_K_HEADS = 16
GDN_V_HEADS = 32
GDN_KEY_WIDTH = GDN_K_HEADS * GDN_HEAD_DIM
GDN_VAL_WIDTH = GDN_V_HEADS * GDN_HEAD_DIM
GDN_CONV_CH = 2 * GDN_KEY_WIDTH + GDN_VAL_WIDTH
GDN_CONV = 4
GDN_CHUNK = 64
GDN_IN = GDN_CONV_CH + GDN_VAL_WIDTH + 2 * GDN_V_HEADS
GDN_Z_COL = GDN_CONV_CH // GDN_HEAD_DIM
GDN_BD_COL = (GDN_CONV_CH + GDN_VAL_WIDTH) // LANES

MOBA_BLOCK = 256
MOBA_TOPK = 3
REL_BUCKETS = 32
REL_MAX_DIST = 4096

EPS = 1e-6
NEG = -1e30
SCALE = HEAD_DIM ** -0.5

_NT = (((1,), (1,)), ((), ()))


def _cparams(*sem):
    return pltpu.CompilerParams(dimension_semantics=sem, vmem_limit_bytes=VMEM_LIMIT)


def _softplus(z):
    return jnp.maximum(z, 0.0) + jnp.log1p(jnp.exp(-jnp.abs(z)))


def _silu(x):
    return x * jax.nn.sigmoid(x)


def _dot(a, b, precision=None):
    return jnp.dot(a, b, preferred_element_type=F32, precision=precision)


def _dot_nt(a, b, precision=None):
    return lax.dot_general(a, b, _NT, preferred_element_type=F32, precision=precision)


def _norm_matmul_kernel(x_ref, g_ref, w_ref, hg_ref, o_ref, h_scr, *, n_norm_tiles, tn):
    j = pl.program_id(1)

    @pl.when(j == 0)
    def _():
        x = x_ref[...]
        ms = jnp.mean(x * x, axis=-1, keepdims=True)
        h_scr[...] = (x * lax.rsqrt(ms + EPS) * g_ref[...]).astype(BF16)

    acc = _dot(h_scr[...], w_ref[...])
    if n_norm_tiles == 0:
        o_ref[...] = acc
    else:
        @pl.when(j < n_norm_tiles)
        def _():
            for s in range(tn // HEAD_DIM):
                sl = slice(s * HEAD_DIM, (s + 1) * HEAD_DIM)
                a = acc[:, sl]
                ms = jnp.mean(a * a, axis=-1, keepdims=True)
                o_ref[:, sl] = a * lax.rsqrt(ms + EPS) * hg_ref[:, sl]

        @pl.when(j >= n_norm_tiles)
        def _():
            o_ref[...] = acc


def norm_matmul(x, g, w, head_gain=None, n_norm_cols=0, tn=512):
    m, d = x.shape
    n = w.shape[1]
    tm = min(m, 512)
    assert m % tm == 0 and n % tn == 0 and n_norm_cols % tn == 0
    if head_gain is None:
        head_gain = jnp.ones((1, n), F32)
    return pl.pallas_call(
        functools.partial(_norm_matmul_kernel, n_norm_tiles=n_norm_cols // tn, tn=tn),
        grid=(m // tm, n // tn),
        in_specs=[pl.BlockSpec((tm, d), lambda i, j: (i, 0)),
                  pl.BlockSpec((1, d), lambda i, j: (0, 0)),
                  pl.BlockSpec((d, tn), lambda i, j: (0, j)),
                  pl.BlockSpec((1, tn), lambda i, j: (0, j))],
        out_specs=pl.BlockSpec((tm, tn), lambda i, j: (i, j)),
        out_shape=jax.ShapeDtypeStruct((m, n), F32),
        scratch_shapes=[pltpu.VMEM((tm, d), BF16)],
        compiler_params=_cparams("parallel", "arbitrary"),
        name="norm_matmul",
    )(x, g.reshape(1, d), w, head_gain)


def _matmul_residual_kernel(a_ref, w_ref, r_ref, o_ref):
    o_ref[...] = r_ref[...] + _dot(a_ref[...].astype(BF16), w_ref[...])


def matmul_residual(a, w, res, tn=512):
    m, k = a.shape
    n = w.shape[1]
    tm = min(m, 512)
    assert m % tm == 0 and n % tn == 0
    return pl.pallas_call(
        _matmul_residual_kernel,
        grid=(m // tm, n // tn),
        in_specs=[pl.BlockSpec((tm, k), lambda i, j: (i, 0)),
                  pl.BlockSpec((k, tn), lambda i, j: (0, j)),
                  pl.BlockSpec((tm, tn), lambda i, j: (i, j))],
        out_specs=pl.BlockSpec((tm, tn), lambda i, j: (i, j)),
        out_shape=jax.ShapeDtypeStruct((m, n), F32),
        compiler_params=_cparams("parallel", "parallel"),
        name="matmul_residual",
    )(a, w, res)


def _stack_heads(x, n):
    return jnp.concatenate([x[:, g * HEAD_DIM:(g + 1) * HEAD_DIM] for g in range(n)], axis=0)


def _unstack_heads(x, n):
    t = x.shape[0] // n
    return jnp.concatenate([x[g * t:(g + 1) * t] for g in range(n)], axis=1)


def _later_matrix(tk):
    r = lax.broadcasted_iota(jnp.int32, (tk, tk), 0)
    c = lax.broadcasted_iota(jnp.int32, (tk, tk), 1)
    return jnp.where(r > c, 1.0, 0.0).astype(BF16)


def _sb_block(qs, k, v, carry, acc, later_mat, mask):
    z = _dot_nt(qs, k) * SCALE
    sp = _softplus(z)
    log_keep = -sp
    if mask is not None:
        log_keep = jnp.where(mask, log_keep, 0.0)
    hi = log_keep.astype(BF16)
    lo = (log_keep - hi.astype(F32)).astype(BF16)
    later = _dot(hi, later_mat) + _dot(lo, later_mat)
    w = jnp.exp(z - sp + later + carry)
    if mask is not None:
        w = jnp.where(mask, w, 0.0)
    acc = acc + _dot(w.astype(BF16), v)
    carry = carry + later[:, :1] + log_keep[:, :1]
    return carry, acc


def _sb_prompt_kernel(q_ref, k_ref, v_ref, gate_ref, o_ref, carry_scr, acc_scr, *, tq):
    qi = pl.program_id(2)
    rows = GROUP * tq
    qs = _stack_heads(q_ref[...], GROUP).astype(BF16)
    later_mat = _later_matrix(tq)
    t_loc = lax.broadcasted_iota(jnp.int32, (rows, tq), 0) & (tq - 1)
    s_loc = lax.broadcasted_iota(jnp.int32, (rows, tq), 1)

    def load(ref, blk):
        start = pl.multiple_of(blk * tq, tq)
        return ref[pl.ds(start, tq), :].astype(BF16)

    carry, acc = _sb_block(qs, load(k_ref, qi), load(v_ref, qi), jnp.zeros((rows, 1), F32),
                           jnp.zeros((rows, HEAD_DIM), F32), later_mat, s_loc < t_loc)
    carry_scr[...] = carry
    acc_scr[...] = acc

    def body(i, _):
        blk = qi - 1 - i
        c, a = _sb_block(qs, load(k_ref, blk), load(v_ref, blk), carry_scr[...], acc_scr[...],
                         later_mat, None)
        carry_scr[...] = c
        acc_scr[...] = a
        return 0

    lax.fori_loop(0, qi, body, 0)
    o = _unstack_heads(acc_scr[...], GROUP)
    o_ref[...] = (o * _silu(gate_ref[...])).astype(o_ref.dtype)


def sb_attention_prompt(proj, batch, seq, tq=128):
    nq = seq // tq
    gw = GROUP * HEAD_DIM
    return pl.pallas_call(
        functools.partial(_sb_prompt_kernel, tq=tq),
        grid=(batch, N_KV_HEADS, nq),
        in_specs=[pl.BlockSpec((tq, gw), lambda b, h, i: (b * nq + i, h)),
                  pl.BlockSpec((seq, HEAD_DIM), lambda b, h, i: (b, K_COL + h)),
                  pl.BlockSpec((seq, HEAD_DIM), lambda b, h, i: (b, V_COL + h)),
                  pl.BlockSpec((tq, gw), lambda b, h, i: (b * nq + i, GATE_COL + h))],
        out_specs=pl.BlockSpec((tq, gw), lambda b, h, i: (b * nq + i, h)),
        out_shape=jax.ShapeDtypeStruct((batch * seq, ATTN_WIDTH), BF16),
        scratch_shapes=[pltpu.VMEM((GROUP * tq, 1), F32), pltpu.VMEM((GROUP * tq, HEAD_DIM), F32)],
        compiler_params=_cparams("parallel", "parallel", "parallel"),
        name="sb_attention_prompt",
    )(proj, proj, proj, proj)


def _pad_rows(x, rows):
    return jnp.concatenate([x, jnp.zeros((rows - x.shape[0], x.shape[1]), x.dtype)], axis=0)


def _sb_sample_kernel(pt_ref, proj_ref, kp_ref, vp_ref, o_ref, carry_scr, acc_scr, *, tq, n_pages):
    del pt_ref
    p = pl.program_id(1)
    rows = GROUP * tq
    later_mat = _later_matrix(PAGE_SIZE)

    def q_rows(h):
        return _stack_heads(proj_ref[:, h * GROUP * HEAD_DIM:(h + 1) * GROUP * HEAD_DIM], GROUP).astype(BF16)

    @pl.when(p == 0)
    def _():
        t_loc = lax.broadcasted_iota(jnp.int32, (rows, PAGE_SIZE), 0) & (tq - 1)
        s_loc = lax.broadcasted_iota(jnp.int32, (rows, PAGE_SIZE), 1)
        for h in range(N_KV_HEADS):
            kc = ATTN_WIDTH + h * HEAD_DIM
            vc = ATTN_WIDTH + KV_WIDTH + h * HEAD_DIM
            k_new = _pad_rows(proj_ref[:, kc:kc + HEAD_DIM], PAGE_SIZE).astype(BF16)
            v_new = _pad_rows(proj_ref[:, vc:vc + HEAD_DIM], PAGE_SIZE).astype(BF16)
            c, a = _sb_block(q_rows(h), k_new, v_new, jnp.zeros((rows, 1), F32),
                             jnp.zeros((rows, HEAD_DIM), F32), later_mat, s_loc < t_loc)
            carry_scr[h] = c
            acc_scr[h] = a

    for h in range(N_KV_HEADS):
        sl = slice(h * HEAD_DIM, (h + 1) * HEAD_DIM)
        c, a = _sb_block(q_rows(h), kp_ref[0, :, sl].astype(BF16), vp_ref[0, :, sl].astype(BF16),
                         carry_scr[h], acc_scr[h], later_mat, None)
        carry_scr[h] = c
        acc_scr[h] = a

    @pl.when(p == n_pages - 1)
    def _():
        gc = ATTN_WIDTH + 2 * KV_WIDTH
        for h in range(N_KV_HEADS):
            sl = slice(h * GROUP * HEAD_DIM, (h + 1) * GROUP * HEAD_DIM)
            gate = proj_ref[:, gc + sl.start:gc + sl.stop]
            o_ref[:, sl] = _unstack_heads(acc_scr[h], GROUP) * _silu(gate)


def sb_attention_sample(proj, k_pool, v_pool, page_table, tq):
    batch, n_pages = page_table.shape
    rows = GROUP * tq
    page_spec = pl.BlockSpec((1, PAGE_SIZE, KV_WIDTH), lambda b, p, pt: (pt[b, n_pages - 1 - p], 0, 0))
    return pl.pallas_call(
        functools.partial(_sb_sample_kernel, tq=tq, n_pages=n_pages),
        grid_spec=pltpu.PrefetchScalarGridSpec(
            num_scalar_prefetch=1,
            grid=(batch, n_pages),
            in_specs=[pl.BlockSpec((tq, ATTN_IN), lambda b, p, pt: (b, 0)), page_spec, page_spec],
            out_specs=pl.BlockSpec((tq, ATTN_WIDTH), lambda b, p, pt: (b, 0)),
            scratch_shapes=[pltpu.VMEM((N_KV_HEADS, rows, 1), F32),
                            pltpu.VMEM((N_KV_HEADS, rows, HEAD_DIM), F32)]),
        out_shape=jax.ShapeDtypeStruct((batch * tq, ATTN_WIDTH), F32),
        compiler_params=_cparams("parallel", "arbitrary"),
        name="sb_attention_sample",
    )(page_table, proj, k_pool, v_pool)


def _gdn_prep_kernel(u_ref, c0_ref, w_ref, o_ref, carry_scr, *, tt, tc):
    c = pl.program_id(1)
    t = pl.program_id(2)

    @pl.when(t == 0)
    def _():
        carry_scr[...] = c0_ref[0]

    u = u_ref[...]
    prev = carry_scr[...]
    w = w_ref[...]
    row = lax.broadcasted_iota(jnp.int32, (SUBLANES, tc), 0)
    y = u * w[GDN_CONV - 1:GDN_CONV, :]
    for i in range(1, GDN_CONV):
        ru = pltpu.roll(u, i, 0)
        top = jnp.where(row < i, pltpu.roll(prev, i, 0), ru[:SUBLANES])
        shifted = top if tt == SUBLANES else jnp.concatenate([top, ru[SUBLANES:]], axis=0)
        y = y + shifted * w[GDN_CONV - 1 - i:GDN_CONV - i, :]
    carry_scr[...] = u[tt - SUBLANES:, :]
    a = _silu(y)

    n_q_tiles = GDN_KEY_WIDTH // tc

    @pl.when(c < 2 * n_q_tiles)
    def _():
        scale = jnp.where(c < n_q_tiles, GDN_HEAD_DIM ** -0.5, 1.0)
        for s in range(tc // GDN_HEAD_DIM):
            sl = slice(s * GDN_HEAD_DIM, (s + 1) * GDN_HEAD_DIM)
            x = a[:, sl]
            o_ref[:, sl] = x * lax.rsqrt(jnp.sum(x * x, axis=-1, keepdims=True) + EPS) * scale

    @pl.when(c >= 2 * n_q_tiles)
    def _():
        o_ref[...] = a


def gdn_prep(proj, conv_state, conv_w, batch, seq, tc=512):
    tt = min(seq, 256)
    nt = seq // tt
    c0 = jnp.pad(conv_state, ((0, 0), (SUBLANES - (GDN_CONV - 1), 0), (0, 0)))
    return pl.pallas_call(
        functools.partial(_gdn_prep_kernel, tt=tt, tc=tc),
        grid=(batch, GDN_CONV_CH // tc, nt),
        in_specs=[pl.BlockSpec((tt, tc), lambda b, c, t: (b * nt + t, c)),
                  pl.BlockSpec((1, SUBLANES, tc), lambda b, c, t: (b, 0, c)),
                  pl.BlockSpec((GDN_CONV, tc), lambda b, c, t: (0, c))],
        out_specs=pl.BlockSpec((tt, tc), lambda b, c, t: (b * nt + t, c)),
        out_shape=jax.ShapeDtypeStruct((batch * seq, GDN_CONV_CH), F32),
        scratch_shapes=[pltpu.VMEM((SUBLANES, tc), F32)],
        compiler_params=_cparams("parallel", "parallel", "arbitrary"),
        name="gdn_prep",
    )(proj, c0, conv_w)


def _gdn_chunk(q, k, v, beta, g, state):
    c = GDN_CHUNK
    ri = lax.broadcasted_iota(jnp.int32, (c, c), 0)
    ci = lax.broadcasted_iota(jnp.int32, (c, c), 1)
    lane = lax.broadcasted_iota(jnp.int32, (c, LANES), 1)
    tril = ri >= ci
    gcx = _dot(jnp.where(tril, 1.0, 0.0), jnp.broadcast_to(g, (c, LANES)), HI)
    gc = gcx[:, :1]
    g_row = _dot_nt(jnp.where(lane == 0, 1.0, 0.0), gcx, HI)
    decay = jnp.exp(jnp.where(tril, gc - g_row, NEG))
    kb = k * beta
    m = -jnp.where(ri > ci, _dot_nt(kb, k, HI) * decay, 0.0)
    inv = jnp.where(ri == ci, 1.0, 0.0) + m
    pw = m
    for _ in range(int(math.log2(c)) - 1):
        pw = _dot(pw, pw, HI)
        inv = inv + _dot(inv, pw, HI)
    egc = jnp.exp(gc)
    sol = _dot(inv, jnp.concatenate([v * beta, kb * egc], axis=1), HI)
    u, w = sol[:, :GDN_HEAD_DIM], sol[:, GDN_HEAD_DIM:]
    intra = _dot_nt(q, k, HI) * decay
    g_last = gcx[c - 1:c, :]
    k_dec = k * jnp.exp(g_last - gcx)
    v_new = u - _dot(w, state, HI)
    o = _dot(q * egc, state, HI) + _dot(intra, v_new, HI)
    state = state * jnp.exp(g_last) + _dot(k_dec.T, v_new, HI)
    return o, state


def _gdn_delta_kernel(q_ref, k_ref, v_ref, bd_ref, z_ref, alog_ref, dt_ref, onorm_ref, s0_ref,
                      o_ref, s_ref, *, seq):
    hv = pl.program_id(1)
    c = GDN_CHUNK
    lane = lax.broadcasted_iota(jnp.int32, (c, LANES), 1)
    row = lax.broadcasted_iota(jnp.int32, (c, 1), 0)
    neg_a = -jnp.exp(alog_ref[...])

    def chunk(q, k, v, raw, z, state):
        beta = jnp.sum(jnp.where(lane == hv, jax.nn.sigmoid(raw), 0.0), axis=1, keepdims=True)
        g_all = neg_a * _softplus(raw + dt_ref[...])
        g = jnp.sum(jnp.where(lane == hv + GDN_V_HEADS, g_all, 0.0), axis=1, keepdims=True)
        g = jnp.where(row < seq, g, 0.0)
        o, state = _gdn_chunk(q, k, v, beta, g, state)
        ms = jnp.mean(o * o, axis=-1, keepdims=True)
        return o * lax.rsqrt(ms + EPS) * onorm_ref[...] * _silu(z), state

    if seq < c:
        args = [_pad_rows(r[...], c) for r in (q_ref, k_ref, v_ref, bd_ref, z_ref)]
        o, state = chunk(*args, s0_ref[0, 0])
        o_ref[...] = o[:seq].astype(o_ref.dtype)
        s_ref[0, 0] = state
    else:
        def body(i, state):
            sl = pl.ds(pl.multiple_of(i * c, c), c)
            o, state = chunk(q_ref[sl, :], k_ref[sl, :], v_ref[sl, :], bd_ref[sl, :], z_ref[sl, :], state)
            o_ref[sl, :] = o.astype(o_ref.dtype)
            return state
        s_ref[0, 0] = lax.fori_loop(0, seq // c, body, s0_ref[0, 0])


def gdn_delta(act, proj, a_log, dt_bias, o_norm, s0, batch, seq, out_dtype):
    assert seq % GDN_CHUNK == 0 or seq < GDN_CHUNK
    rep = GDN_V_HEADS // GDN_K_HEADS
    pad = jnp.zeros((GDN_V_HEADS,), F32)
    tail = jnp.zeros((LANES - 2 * GDN_V_HEADS,), F32)
    alog = jnp.concatenate([pad, a_log, tail]).reshape(1, LANES)
    dt = jnp.concatenate([pad, dt_bias, tail]).reshape(1, LANES)
    blk = (seq, GDN_HEAD_DIM)
    return pl.pallas_call(
        functools.partial(_gdn_delta_kernel, seq=seq),
        grid=(batch, GDN_V_HEADS),
        in_specs=[pl.BlockSpec(blk, lambda b, h: (b, h // rep)),
                  pl.BlockSpec(blk, lambda b, h: (b, GDN_K_HEADS + h // rep)),
                  pl.BlockSpec(blk, lambda b, h: (b, 2 * GDN_K_HEADS + h)),
                  pl.BlockSpec((seq, LANES), lambda b, h: (b, GDN_BD_COL)),
                  pl.BlockSpec(blk, lambda b, h: (b, GDN_Z_COL + h)),
                  pl.BlockSpec((1, LANES), lambda b, h: (0, 0)),
                  pl.BlockSpec((1, LANES), lambda b, h: (0, 0)),
                  pl.BlockSpec((1, GDN_HEAD_DIM), lambda b, h: (0, 0)),
                  pl.BlockSpec((1, 1, GDN_HEAD_DIM, GDN_HEAD_DIM), lambda b, h: (b, h, 0, 0))],
        out_specs=[pl.BlockSpec(blk, lambda b, h: (b, h)),
                   pl.BlockSpec((1, 1, GDN_HEAD_DIM, GDN_HEAD_DIM), lambda b, h: (b, h, 0, 0))],
        out_shape=[jax.ShapeDtypeStruct((batch * seq, GDN_VAL_WIDTH), out_dtype),
                   jax.ShapeDtypeStruct((batch, GDN_V_HEADS, GDN_HEAD_DIM, GDN_HEAD_DIM), F32)],
        compiler_params=_cparams("parallel", "parallel"),
        name="gdn_delta",
    )(act, act, act, proj, proj, alog, dt, o_norm.reshape(1, GDN_HEAD_DIM), s0)


def _rel_bucket(dist):
    max_exact = REL_BUCKETS // 2
    n = jnp.maximum(dist, 0)
    large = max_exact + (jnp.log(jnp.maximum(n, 1).astype(F32) / max_exact)
                         / math.log(REL_MAX_DIST / max_exact) * (REL_BUCKETS - max_exact)).astype(jnp.int32)
    large = jnp.minimum(large, REL_BUCKETS - 1)
    return jnp.where(n < max_exact, n, large)


def _bias_table_kernel(rbt_ref, onehot_ref, o_ref):
    o_ref[...] = _dot(rbt_ref[...], onehot_ref[...], HI)


def bias_by_distance(rel_bias, dist):
    bucket = _rel_bucket(dist)
    onehot = (bucket[None, :] == jnp.arange(REL_BUCKETS)[:, None]) & (dist[None, :] >= 0)
    return pl.pallas_call(
        _bias_table_kernel,
        out_shape=jax.ShapeDtypeStruct((N_HEADS, dist.shape[0]), F32),
        compiler_params=pltpu.CompilerParams(vmem_limit_bytes=VMEM_LIMIT),
        name="bias_table",
    )(rel_bias.T, onehot.astype(F32))


def _toeplitz(window_row, rows, shift):
    x = jnp.broadcast_to(window_row, (rows, window_row.shape[1]))
    return pltpu.roll(x, shift, 1, stride=1, stride_axis=0)


def _select_topk(gate, n_valid, n_blocks):
    lane = lax.broadcasted_iota(jnp.int32, gate.shape, 1)
    valid = lane < n_valid
    gm = jnp.where(valid, gate, -jnp.inf)
    cnt = jnp.zeros(gate.shape, jnp.int32)
    for m in range(n_blocks):
        col = gm[:, m:m + 1]
        beats = (col > gm) | ((col == gm) & (lane > m))
        cnt = cnt + jnp.where(beats, 1, 0)
    return jnp.where(valid & (cnt < MOBA_TOPK), 1.0, 0.0)


def _moba_tile(qs, k, v, bias, mask, m, l, acc):
    s = _dot_nt(qs, k) * SCALE + bias
    s = jnp.where(mask, s, NEG)
    m_new = jnp.maximum(m, jnp.max(s, axis=1, keepdims=True))
    p = jnp.where(mask, jnp.exp(s - m_new), 0.0)
    alpha = jnp.exp(m - m_new)
    l = alpha * l + jnp.sum(p, axis=1, keepdims=True)
    acc = alpha * acc + _dot(p.astype(BF16), v)
    return m_new, l, acc


def _moba_prompt_kernel(q_ref, k_ref, v_ref, gate_ref, tb_ref, o_ref,
                        means_scr, bias_scr, sel_scr, m_scr, l_scr, acc_scr, *, tq, n_blocks):
    h = pl.program_id(0)
    b = pl.program_id(1)
    qi = pl.program_id(2)
    rows = GROUP * tq

    @pl.when(qi == 0)
    def _():
        means_scr[...] = jnp.zeros_like(means_scr)
        means_scr[0:n_blocks, :] = jnp.mean(k_ref[...].reshape(n_blocks, MOBA_BLOCK, HEAD_DIM), axis=1)

    @pl.when(b == 0)
    def _():
        for g in range(GROUP):
            wrow = tb_ref[pl.ds((h * GROUP + g) * n_blocks + qi, 1), :]
            bias_scr[qi, pl.ds(g * tq, tq), :] = _toeplitz(wrow, tq, tq + 1)[:, :tq]

    qs32 = _stack_heads(q_ref[...], GROUP)
    qs = qs32.astype(BF16)
    sel_scr[...] = _select_topk(_dot_nt(qs32, means_scr[...], HI), qi, n_blocks)
    lane = lax.broadcasted_iota(jnp.int32, (rows, LANES), 1)
    t_loc = lax.broadcasted_iota(jnp.int32, (rows, tq), 0) & (tq - 1)
    s_loc = lax.broadcasted_iota(jnp.int32, (rows, tq), 1)

    def load(ref, blk):
        return ref[pl.ds(pl.multiple_of(blk * tq, tq), tq), :].astype(BF16)

    m, l, acc = _moba_tile(qs, load(k_ref, qi), load(v_ref, qi), bias_scr[0], s_loc <= t_loc,
                           jnp.full((rows, 1), NEG, F32), jnp.zeros((rows, 1), F32),
                           jnp.zeros((rows, HEAD_DIM), F32))
    m_scr[...] = m
    l_scr[...] = l
    acc_scr[...] = acc

    def body(kb, _):
        selcol = jnp.sum(jnp.where(lane == kb, sel_scr[...], 0.0), axis=1, keepdims=True) > 0.5
        m, l, acc = _moba_tile(qs, load(k_ref, kb), load(v_ref, kb), bias_scr[qi - kb], selcol,
                               m_scr[...], l_scr[...], acc_scr[...])
        m_scr[...] = m
        l_scr[...] = l
        acc_scr[...] = acc
        return 0

    lax.fori_loop(0, qi, body, 0)
    o = _unstack_heads(acc_scr[...] / l_scr[...], GROUP)
    o_ref[...] = (o * _silu(gate_ref[...])).astype(o_ref.dtype)


def moba_attention_prompt(proj, rel_bias, batch, seq):
    tq = MOBA_BLOCK
    assert seq % tq == 0
    nb = seq // tq
    gw = GROUP * HEAD_DIM
    rows = GROUP * tq
    dist = (jnp.arange(nb)[:, None] * tq + (tq - 1) - jnp.arange(2 * tq)[None, :]).reshape(-1)
    table = bias_by_distance(rel_bias, dist).reshape(N_HEADS * nb, 2 * tq)
    return pl.pallas_call(
        functools.partial(_moba_prompt_kernel, tq=tq, n_blocks=nb),
        grid=(N_KV_HEADS, batch, nb),
        in_specs=[pl.BlockSpec((tq, gw), lambda h, b, i: (b * nb + i, h)),
                  pl.BlockSpec((seq, HEAD_DIM), lambda h, b, i: (b, K_COL + h)),
                  pl.BlockSpec((seq, HEAD_DIM), lambda h, b, i: (b, V_COL + h)),
                  pl.BlockSpec((tq, gw), lambda h, b, i: (b * nb + i, GATE_COL + h)),
                  pl.BlockSpec((N_HEADS * nb, 2 * tq), lambda h, b, i: (0, 0))],
        out_specs=pl.BlockSpec((tq, gw), lambda h, b, i: (b * nb + i, h)),
        out_shape=jax.ShapeDtypeStruct((batch * seq, ATTN_WIDTH), BF16),
        scratch_shapes=[pltpu.VMEM((LANES, HEAD_DIM), F32),
                        pltpu.VMEM((nb, rows, tq), F32),
                        pltpu.VMEM((rows, LANES), F32),
                        pltpu.VMEM((rows, 1), F32),
                        pltpu.VMEM((rows, 1), F32),
                        pltpu.VMEM((rows, HEAD_DIM), F32)],
        compiler_params=_cparams("arbitrary", "arbitrary", "arbitrary"),
        name="moba_attention_prompt",
    )(proj, proj, proj, proj, table)


def _moba_means_kernel(pt_ref, *refs):
    del pt_ref
    page_refs, o_ref = refs[:-1], refs[-1]
    total = sum(jnp.sum(r[0], axis=0, keepdims=True) for r in page_refs)
    o_ref[0, 0] = total * (1.0 / MOBA_BLOCK)


def moba_block_means(k_pool, page_table):
    batch, n_pages = page_table.shape
    ppb = MOBA_BLOCK // PAGE_SIZE
    nb = n_pages // ppb
    specs = [pl.BlockSpec((1, PAGE_SIZE, KV_WIDTH), functools.partial(lambda b, n, pt, j: (pt[b, n * ppb + j], 0, 0), j=j))
             for j in range(ppb)]
    return pl.pallas_call(
        _moba_means_kernel,
        grid_spec=pltpu.PrefetchScalarGridSpec(
            num_scalar_prefetch=1,
            grid=(batch, nb),
            in_specs=specs,
            out_specs=pl.BlockSpec((1, 1, 1, KV_WIDTH), lambda b, n, pt: (b, n, 0, 0))),
        out_shape=jax.ShapeDtypeStruct((batch, nb, 1, KV_WIDTH), F32),
        compiler_params=_cparams("parallel", "parallel"),
        name="moba_block_means",
    )(page_table, *([k_pool] * ppb))


def _moba_sample_kernel(pt_ref, proj_ref, means_ref, kp_ref, vp_ref, tb_ref, o_ref,
                        sel_scr, m_scr, l_scr, acc_scr, *, tq, n_pages):
    del pt_ref
    p = pl.program_id(1)
    rows = GROUP * tq
    n_past_blocks = n_pages * PAGE_SIZE // MOBA_BLOCK
    lane = lax.broadcasted_iota(jnp.int32, (rows, LANES), 1)

    def q_rows(h):
        return _stack_heads(proj_ref[:, h * GROUP * HEAD_DIM:(h + 1) * GROUP * HEAD_DIM], GROUP)

    def bias_rows(h, page):
        tiles = []
        for g in range(GROUP):
            wrow = tb_ref[pl.ds((h * GROUP + g) * (n_pages + 1) + page, 1), :]
            tiles.append(_toeplitz(wrow, tq, PAGE_SIZE + 1)[:, :PAGE_SIZE])
        return jnp.concatenate(tiles, axis=0)

    @pl.when(p == 0)
    def _():
        t_loc = lax.broadcasted_iota(jnp.int32, (rows, PAGE_SIZE), 0) & (tq - 1)
        s_loc = lax.broadcasted_iota(jnp.int32, (rows, PAGE_SIZE), 1)
        for h in range(N_KV_HEADS):
            sl = slice(h * HEAD_DIM, (h + 1) * HEAD_DIM)
            qs32 = q_rows(h)
            means = _pad_rows(means_ref[0, :, sl], LANES)
            sel_scr[h] = _select_topk(_dot_nt(qs32, means, HI), n_past_blocks, n_past_blocks)
            kc = ATTN_WIDTH + h * HEAD_DIM
            vc = ATTN_WIDTH + KV_WIDTH + h * HEAD_DIM
            k_new = _pad_rows(proj_ref[:, kc:kc + HEAD_DIM], PAGE_SIZE).astype(BF16)
            v_new = _pad_rows(proj_ref[:, vc:vc + HEAD_DIM], PAGE_SIZE).astype(BF16)
            m, l, acc = _moba_tile(qs32.astype(BF16), k_new, v_new, bias_rows(h, n_pages), s_loc <= t_loc,
                                   jnp.full((rows, 1), NEG, F32), jnp.zeros((rows, 1), F32),
                                   jnp.zeros((rows, HEAD_DIM), F32))
            m_scr[h] = m
            l_scr[h] = l
            acc_scr[h] = acc

    blk = p // (MOBA_BLOCK // PAGE_SIZE)
    for h in range(N_KV_HEADS):
        sl = slice(h * HEAD_DIM, (h + 1) * HEAD_DIM)
        selcol = jnp.sum(jnp.where(lane == blk, sel_scr[h], 0.0), axis=1, keepdims=True) > 0.5
        m, l, acc = _moba_tile(q_rows(h).astype(BF16), kp_ref[0, :, sl].astype(BF16),
                               vp_ref[0, :, sl].astype(BF16), bias_rows(h, p), selcol,
                               m_scr[h], l_scr[h], acc_scr[h])
        m_scr[h] = m
        l_scr[h] = l
        acc_scr[h] = acc

    @pl.when(p == n_pages - 1)
    def _():
        gc = ATTN_WIDTH + 2 * KV_WIDTH
        for h in range(N_KV_HEADS):
            sl = slice(h * GROUP * HEAD_DIM, (h + 1) * GROUP * HEAD_DIM)
            gate = proj_ref[:, gc + sl.start:gc + sl.stop]
            o_ref[:, sl] = _unstack_heads(acc_scr[h] / l_scr[h], GROUP) * _silu(gate)


def moba_attention_sample(proj, k_pool, v_pool, page_table, rel_bias, tq):
    batch, n_pages = page_table.shape
    assert tq <= MOBA_BLOCK - (n_pages * PAGE_SIZE) % MOBA_BLOCK and (n_pages * PAGE_SIZE) % MOBA_BLOCK == 0
    rows = GROUP * tq
    nb = n_pages * PAGE_SIZE // MOBA_BLOCK
    assert nb <= LANES
    means = moba_block_means(k_pool, page_table).reshape(batch, nb, KV_WIDTH)
    dist = ((n_pages - jnp.arange(n_pages + 1))[:, None] * PAGE_SIZE + (PAGE_SIZE - 1)
            - jnp.arange(2 * PAGE_SIZE)[None, :]).reshape(-1)
    table = bias_by_distance(rel_bias, dist).reshape(N_HEADS * (n_pages + 1), 2 * PAGE_SIZE)
    page_spec = pl.BlockSpec((1, PAGE_SIZE, KV_WIDTH), lambda b, p, pt: (pt[b, p], 0, 0))
    return pl.pallas_call(
        functools.partial(_moba_sample_kernel, tq=tq, n_pages=n_pages),
        grid_spec=pltpu.PrefetchScalarGridSpec(
            num_scalar_prefetch=1,
            grid=(batch, n_pages),
            in_specs=[pl.BlockSpec((tq, ATTN_IN), lambda b, p, pt: (b, 0)),
                      pl.BlockSpec((1, nb, KV_WIDTH), lambda b, p, pt: (b, 0, 0)),
                      page_spec, page_spec,
                      pl.BlockSpec((N_HEADS * (n_pages + 1), 2 * PAGE_SIZE), lambda b, p, pt: (0, 0))],
            out_specs=pl.BlockSpec((tq, ATTN_WIDTH), lambda b, p, pt: (b, 0)),
            scratch_shapes=[pltpu.VMEM((N_KV_HEADS, rows, LANES), F32),
                            pltpu.VMEM((N_KV_HEADS, rows, 1), F32),
                            pltpu.VMEM((N_KV_HEADS, rows, 1), F32),
                            pltpu.VMEM((N_KV_HEADS, rows, HEAD_DIM), F32)]),
        out_shape=jax.ShapeDtypeStruct((batch * tq, ATTN_WIDTH), F32),
        compiler_params=_cparams("parallel", "arbitrary"),
        name="moba_attention_sample",
    )(page_table, proj, means, k_pool, v_pool, table)


N_MIXERS = 3
GDN_IN_PADDED = -(-GDN_IN // 512) * 512


def _new_kv(proj, batch, seq):
    k = proj[:, ATTN_WIDTH:ATTN_WIDTH + KV_WIDTH].reshape(batch, seq, N_KV_HEADS, HEAD_DIM)
    v = proj[:, ATTN_WIDTH + KV_WIDTH:ATTN_WIDTH + 2 * KV_WIDTH].reshape(batch, seq, N_KV_HEADS, HEAD_DIM)
    return k, v


def _pool(cache):
    return cache.reshape(cache.shape[0], PAGE_SIZE, KV_WIDTH)


def kernel(x_prompt, x_sample, cache_sb_k, cache_sb_v, state_gdn_conv, state_gdn_rec, cache_moba_k, cache_moba_v, page_table, norm_g, sb_w_in, sb_w_out, gdn_w_in, gdn_conv_w, gdn_a_log, gdn_dt_bias, gdn_o_norm, gdn_w_out, moba_w_in, moba_q_norm, moba_k_norm, moba_w_out, rel_bias):
    bp, tp, d = x_prompt.shape
    bs, ts, _ = x_sample.shape
    yp = x_prompt.reshape(bp * tp, d)
    ys = x_sample.reshape(bs * ts, d)
    outs = {name: [] for name in ("sb_kp", "sb_vp", "sb_ks", "sb_vs", "gdn_cp", "gdn_sp", "gdn_cs", "gdn_ss",
                                  "mb_kp", "mb_vp", "mb_ks", "mb_vs")}
    for layer in range(norm_g.shape[0]):
        kind = layer % N_MIXERS
        j = layer // N_MIXERS
        g = norm_g[layer]
        if kind == 0:
            w_in = sb_w_in[j].astype(BF16)
            w_out = sb_w_out[j].astype(BF16)
            pp = norm_matmul(yp, g, w_in)
            ps = norm_matmul(ys, g, w_in)
            op = sb_attention_prompt(pp, bp, tp)
            os_ = sb_attention_sample(ps, _pool(cache_sb_k[j]), _pool(cache_sb_v[j]), page_table, ts)
            kp, vp = _new_kv(pp, bp, tp)
            ks, vs = _new_kv(ps, bs, ts)
            outs["sb_kp"].append(kp); outs["sb_vp"].append(vp); outs["sb_ks"].append(ks); outs["sb_vs"].append(vs)
        elif kind == 1:
            w_in = jnp.pad(gdn_w_in[j], ((0, 0), (0, GDN_IN_PADDED - GDN_IN))).astype(BF16)
            w_out = gdn_w_out[j].astype(BF16)
            pp = norm_matmul(yp, g, w_in)
            ps = norm_matmul(ys, g, w_in)
            act_p = gdn_prep(pp, jnp.zeros((bp, GDN_CONV - 1, GDN_CONV_CH), F32), gdn_conv_w[j], bp, tp)
            act_s = gdn_prep(ps, state_gdn_conv[j], gdn_conv_w[j], bs, ts)
            s0 = jnp.zeros((bp,) + state_gdn_rec.shape[2:], F32)
            op, sp = gdn_delta(act_p, pp, gdn_a_log[j], gdn_dt_bias[j], gdn_o_norm[j], s0, bp, tp, BF16)
            os_, ss = gdn_delta(act_s, ps, gdn_a_log[j], gdn_dt_bias[j], gdn_o_norm[j], state_gdn_rec[j],
                                bs, ts, F32)
            outs["gdn_cp"].append(pp.reshape(bp, tp, -1)[:, tp - (GDN_CONV - 1):, :GDN_CONV_CH])
            outs["gdn_cs"].append(ps.reshape(bs, ts, -1)[:, ts - (GDN_CONV - 1):, :GDN_CONV_CH])
            outs["gdn_sp"].append(sp); outs["gdn_ss"].append(ss)
        else:
            w_in = moba_w_in[j].astype(BF16)
            w_out = moba_w_out[j].astype(BF16)
            head_gain = jnp.concatenate([jnp.tile(moba_q_norm[j], N_HEADS), jnp.tile(moba_k_norm[j], N_KV_HEADS),
                                         jnp.ones((ATTN_IN - ATTN_WIDTH - KV_WIDTH,), F32)]).reshape(1, ATTN_IN)
            pp = norm_matmul(yp, g, w_in, head_gain, ATTN_WIDTH + KV_WIDTH)
            ps = norm_matmul(ys, g, w_in, head_gain, ATTN_WIDTH + KV_WIDTH)
            op = moba_attention_prompt(pp, rel_bias, bp, tp)
            os_ = moba_attention_sample(ps, _pool(cache_moba_k[j]), _pool(cache_moba_v[j]), page_table,
                                        rel_bias, ts)
            kp, vp = _new_kv(pp, bp, tp)
            ks, vs = _new_kv(ps, bs, ts)
            outs["mb_kp"].append(kp); outs["mb_vp"].append(vp); outs["mb_ks"].append(ks); outs["mb_vs"].append(vs)
        yp = matmul_residual(op, w_out, yp)
        ys = matmul_residual(os_, w_out, ys)
    stack = lambda name: jnp.stack(outs[name])
    return (yp.reshape(bp, tp, d), ys.reshape(bs, ts, d),
            stack("sb_kp"), stack("sb_vp"), stack("sb_ks"), stack("sb_vs"),
            stack("gdn_cp"), stack("gdn_sp"), stack("gdn_cs"), stack("gdn_ss"),
            stack("mb_kp"), stack("mb_vp"), stack("mb_ks"), stack("mb_vs"))
```

```python
import functools
import math

import jax
import jax.numpy as jnp
from jax import lax
from jax.experimental import pallas as pl
from jax.experimental.pallas import tpu as pltpu

F32 = jnp.float32
BF16 = jnp.bfloat16
HI = lax.Precision.HIGHEST

LANES = 128
SUBLANES = 8
VMEM_LIMIT = 56 * 1024 * 1024

HEAD_DIM = 128
N_HEADS = 16
N_KV_HEADS = 4
GROUP = N_HEADS // N_KV_HEADS
ATTN_WIDTH = N_HEADS * HEAD_DIM
KV_WIDTH = N_KV_HEADS * HEAD_DIM
ATTN_IN = 2 * ATTN_WIDTH + 2 * KV_WIDTH
Q_COL = 0
K_COL = ATTN_WIDTH // HEAD_DIM
V_COL = K_COL + N_KV_HEADS
GATE_COL = (ATTN_WIDTH + 2 * KV_WIDTH) // (GROUP * HEAD_DIM)
PAGE_SIZE = 128

GDN_HEAD_DIM = 128
GDN_K_HEADS = 16
GDN_V_HEADS = 32
GDN_KEY_WIDTH = GDN_K_HEADS * GDN_HEAD_DIM
GDN_VAL_WIDTH = GDN_V_HEADS * GDN_HEAD_DIM
GDN_CONV_CH = 2 * GDN_KEY_WIDTH + GDN_VAL_WIDTH
GDN_CONV = 4
GDN_CHUNK = 64
GDN_IN = GDN_CONV_CH + GDN_VAL_WIDTH + 2 * GDN_V_HEADS
GDN_Z_COL = GDN_CONV_CH // GDN_HEAD_DIM
GDN_BD_COL = (GDN_CONV_CH + GDN_VAL_WIDTH) // LANES

MOBA_BLOCK = 256
MOBA_TOPK = 3
REL_BUCKETS = 32
REL_MAX_DIST = 4096

EPS = 1e-6
NEG = -1e30
SCALE = HEAD_DIM ** -0.5

_NT = (((1,), (1,)), ((), ()))


def _cparams(*sem):
    return pltpu.CompilerParams(dimension_semantics=sem, vmem_limit_bytes=VMEM_LIMIT)


def _softplus(z):
    return jnp.maximum(z, 0.0) + jnp.log1p(jnp.exp(-jnp.abs(z)))


def _silu(x):
    return x * jax.nn.sigmoid(x)


def _dot(a, b, precision=None):
    return jnp.dot(a, b, preferred_element_type=F32, precision=precision)


def _dot_nt(a, b, precision=None):
    return lax.dot_general(a, b, _NT, preferred_element_type=F32, precision=precision)


def _split_bf16(x):
    hi = x.astype(BF16)
    return hi, (x - hi.astype(F32)).astype(BF16)


def _bdot(a, b):
    return jnp.einsum("bij,bjk->bik", a, b, preferred_element_type=F32)


def _bdot_split(a, b):
    return _bdot(a[0], b[0]) + (_bdot(a[0], b[1]) + _bdot(a[1], b[0]))


def _norm_matmul_kernel(x_ref, g_ref, w_ref, hg_ref, o_ref, h_scr, *, n_norm_tiles, tn):
    j = pl.program_id(1)

    @pl.when(j == 0)
    def _():
        x = x_ref[...]
        ms = jnp.mean(x * x, axis=-1, keepdims=True)
        h_scr[...] = (x * lax.rsqrt(ms + EPS) * g_ref[...]).astype(BF16)

    acc = _dot(h_scr[...], w_ref[...])
    if n_norm_tiles == 0:
        o_ref[...] = acc
    else:
        @pl.when(j < n_norm_tiles)
        def _():
            for s in range(tn // HEAD_DIM):
                sl = slice(s * HEAD_DIM, (s + 1) * HEAD_DIM)
                a = acc[:, sl]
                ms = jnp.mean(a * a, axis=-1, keepdims=True)
                o_ref[:, sl] = a * lax.rsqrt(ms + EPS) * hg_ref[:, sl]

        @pl.when(j >= n_norm_tiles)
        def _():
            o_ref[...] = acc


def norm_matmul(x, g, w, head_gain=None, n_norm_cols=0, tn=512):
    m, d = x.shape
    n = w.shape[1]
    tm = min(m, 512)
    assert m % tm == 0 and n % tn == 0 and n_norm_cols % tn == 0
    if head_gain is None:
        head_gain = jnp.ones((1, n), F32)
    return pl.pallas_call(
        functools.partial(_norm_matmul_kernel, n_norm_tiles=n_norm_cols // tn, tn=tn),
        grid=(m // tm, n // tn),
        in_specs=[pl.BlockSpec((tm, d), lambda i, j: (i, 0)),
                  pl.BlockSpec((1, d), lambda i, j: (0, 0)),
                  pl.BlockSpec((d, tn), lambda i, j: (0, j)),
                  pl.BlockSpec((1, tn), lambda i, j: (0, j))],
        out_specs=pl.BlockSpec((tm, tn), lambda i, j: (i, j)),
        out_shape=jax.ShapeDtypeStruct((m, n), F32),
        scratch_shapes=[pltpu.VMEM((tm, d), BF16)],
        compiler_params=_cparams("parallel", "arbitrary"),
        name="norm_matmul",
    )(x, g.reshape(1, d), w, head_gain)


def _matmul_residual_kernel(a_ref, w_ref, r_ref, o_ref):
    o_ref[...] = r_ref[...] + _dot(a_ref[...].astype(BF16), w_ref[...])


def matmul_residual(a, w, res, tn=512):
    m, k = a.shape
    n = w.shape[1]
    tm = min(m, 512)
    assert m % tm == 0 and n % tn == 0
    return pl.pallas_call(
        _matmul_residual_kernel,
        grid=(m // tm, n // tn),
        in_specs=[pl.BlockSpec((tm, k), lambda i, j: (i, 0)),
                  pl.BlockSpec((k, tn), lambda i, j: (0, j)),
                  pl.BlockSpec((tm, tn), lambda i, j: (i, j))],
        out_specs=pl.BlockSpec((tm, tn), lambda i, j: (i, j)),
        out_shape=jax.ShapeDtypeStruct((m, n), F32),
        compiler_params=_cparams("parallel", "parallel"),
        name="matmul_residual",
    )(a, w, res)


def _stack_heads(x, n):
    return jnp.concatenate([x[:, g * HEAD_DIM:(g + 1) * HEAD_DIM] for g in range(n)], axis=0)


def _unstack_heads(x, n):
    t = x.shape[0] // n
    return jnp.concatenate([x[g * t:(g + 1) * t] for g in range(n)], axis=1)


def _later_matrix(tk):
    r = lax.broadcasted_iota(jnp.int32, (tk, tk), 0)
    c = lax.broadcasted_iota(jnp.int32, (tk, tk), 1)
    return jnp.where(r > c, 1.0, 0.0).astype(BF16)


def _sb_block(qs, k, v, carry, acc, later_mat, mask=None):
    tk = later_mat.shape[0]
    n = k.shape[0] // tk
    z = _dot_nt(qs, k)
    sp = _softplus(z)
    log_keep = -sp
    if mask is not None:
        log_keep = jnp.where(mask, log_keep, 0.0)
    hi = log_keep.astype(BF16)
    lo = (log_keep - hi.astype(F32)).astype(BF16)
    later, total = [], []
    for j in range(n):
        sl = slice(j * tk, (j + 1) * tk)
        lt = _dot(hi[:, sl], later_mat) + _dot(lo[:, sl], later_mat)
        later.append(lt)
        total.append(lt[:, :1] + log_keep[:, j * tk:j * tk + 1])
    for j in reversed(range(n)):
        later[j] = later[j] + carry
        carry = carry + total[j]
    later = later[0] if n == 1 else jnp.concatenate(later, axis=1)
    w = jnp.exp(z - sp + later)
    if mask is not None:
        w = jnp.where(mask, w, 0.0)
    return carry, acc + _dot(w.astype(BF16), v)


def _sb_prompt_kernel(q_ref, k_ref, v_ref, gate_ref, o_ref, carry_scr, acc_scr, *, tq):
    qi = pl.program_id(2)
    rows = GROUP * tq
    qs = (_stack_heads(q_ref[...], GROUP) * SCALE).astype(BF16)
    later_mat = _later_matrix(tq)
    t_loc = lax.broadcasted_iota(jnp.int32, (rows, tq), 0) & (tq - 1)
    s_loc = lax.broadcasted_iota(jnp.int32, (rows, tq), 1)

    def load(ref, blk, n):
        start = pl.multiple_of(blk * tq, tq)
        return ref[pl.ds(start, n * tq), :].astype(BF16)

    def update(blk, n, carry, acc, mask=None):
        c, a = _sb_block(qs, load(k_ref, blk, n), load(v_ref, blk, n), carry, acc, later_mat, mask)
        carry_scr[...] = c
        acc_scr[...] = a

    update(qi, 1, jnp.zeros((rows, 1), F32), jnp.zeros((rows, HEAD_DIM), F32), s_loc < t_loc)

    def body(i, _):
        update(qi - 2 - 2 * i, 2, carry_scr[...], acc_scr[...])
        return 0

    lax.fori_loop(0, qi // 2, body, 0)

    @pl.when(qi % 2 == 1)
    def _():
        update(0, 1, carry_scr[...], acc_scr[...])

    o = _unstack_heads(acc_scr[...], GROUP)
    o_ref[...] = (o * _silu(gate_ref[...])).astype(o_ref.dtype)


def sb_attention_prompt(proj, batch, seq, tq=128):
    nq = seq // tq
    gw = GROUP * HEAD_DIM
    return pl.pallas_call(
        functools.partial(_sb_prompt_kernel, tq=tq),
        grid=(batch, N_KV_HEADS, nq),
        in_specs=[pl.BlockSpec((tq, gw), lambda b, h, i: (b * nq + i, h)),
                  pl.BlockSpec((seq, HEAD_DIM), lambda b, h, i: (b, K_COL + h)),
                  pl.BlockSpec((seq, HEAD_DIM), lambda b, h, i: (b, V_COL + h)),
                  pl.BlockSpec((tq, gw), lambda b, h, i: (b * nq + i, GATE_COL + h))],
        out_specs=pl.BlockSpec((tq, gw), lambda b, h, i: (b * nq + i, h)),
        out_shape=jax.ShapeDtypeStruct((batch * seq, ATTN_WIDTH), BF16),
        scratch_shapes=[pltpu.VMEM((GROUP * tq, 1), F32), pltpu.VMEM((GROUP * tq, HEAD_DIM), F32)],
        compiler_params=_cparams("parallel", "parallel", "parallel"),
        name="sb_attention_prompt",
    )(proj, proj, proj, proj)


def _pad_rows(x, rows):
    return jnp.concatenate([x, jnp.zeros((rows - x.shape[0], x.shape[1]), x.dtype)], axis=0)


PAGES_PER_STEP = 8


def _page_specs(n_pages, page_offset, latest_first):
    n_steps = n_pages // PAGES_PER_STEP

    def index_map(b, p, pt, *, j):
        step = n_steps - 1 - p if latest_first else p
        return (page_offset + pt[b, step * PAGES_PER_STEP + j], 0, 0)

    return [pl.BlockSpec((1, PAGE_SIZE * N_KV_HEADS, HEAD_DIM), functools.partial(index_map, j=j))
            for j in range(PAGES_PER_STEP)]


def _head_pages(page_refs, h):
    return jnp.concatenate([r[0, pl.ds(h, PAGE_SIZE, stride=N_KV_HEADS), :] for r in page_refs], axis=0)


def _flat_pool(cache):
    return cache.reshape(cache.shape[0] * cache.shape[1], PAGE_SIZE * N_KV_HEADS, HEAD_DIM)


def _sb_sample_kernel(pt_ref, proj_ref, *refs, tq):
    del pt_ref
    k_refs, v_refs = refs[:PAGES_PER_STEP], refs[PAGES_PER_STEP:2 * PAGES_PER_STEP]
    o_ref, carry_scr, acc_scr = refs[2 * PAGES_PER_STEP:]
    p = pl.program_id(1)
    rows = GROUP * tq
    later_mat = _later_matrix(PAGE_SIZE)

    def q_rows(h):
        q = proj_ref[:, h * GROUP * HEAD_DIM:(h + 1) * GROUP * HEAD_DIM]
        return (_stack_heads(q, GROUP) * SCALE).astype(BF16)

    @pl.when(p == 0)
    def _():
        t_loc = lax.broadcasted_iota(jnp.int32, (rows, PAGE_SIZE), 0) & (tq - 1)
        s_loc = lax.broadcasted_iota(jnp.int32, (rows, PAGE_SIZE), 1)
        for h in range(N_KV_HEADS):
            kc = ATTN_WIDTH + h * HEAD_DIM
            vc = ATTN_WIDTH + KV_WIDTH + h * HEAD_DIM
            k_new = _pad_rows(proj_ref[:, kc:kc + HEAD_DIM], PAGE_SIZE).astype(BF16)
            v_new = _pad_rows(proj_ref[:, vc:vc + HEAD_DIM], PAGE_SIZE).astype(BF16)
            c, a = _sb_block(q_rows(h), k_new, v_new, jnp.zeros((rows, 1), F32),
                             jnp.zeros((rows, HEAD_DIM), F32), later_mat, s_loc < t_loc)
            carry_scr[h] = c
            acc_scr[h] = a

    for h in range(N_KV_HEADS):
        c, a = _sb_block(q_rows(h), _head_pages(k_refs, h).astype(BF16), _head_pages(v_refs, h).astype(BF16),
                         carry_scr[h], acc_scr[h], later_mat)
        carry_scr[h] = c
        acc_scr[h] = a

    @pl.when(p == pl.num_programs(1) - 1)
    def _():
        gc = ATTN_WIDTH + 2 * KV_WIDTH
        for h in range(N_KV_HEADS):
            sl = slice(h * GROUP * HEAD_DIM, (h + 1) * GROUP * HEAD_DIM)
            gate = proj_ref[:, gc + sl.start:gc + sl.stop]
            o_ref[:, sl] = _unstack_heads(acc_scr[h], GROUP) * _silu(gate)


def sb_attention_sample(proj, k_pool, v_pool, page_offset, page_table, tq):
    batch, n_pages = page_table.shape
    assert n_pages % PAGES_PER_STEP == 0
    rows = GROUP * tq
    specs = _page_specs(n_pages, page_offset, latest_first=True)
    return pl.pallas_call(
        functools.partial(_sb_sample_kernel, tq=tq),
        grid_spec=pltpu.PrefetchScalarGridSpec(
            num_scalar_prefetch=1,
            grid=(batch, n_pages // PAGES_PER_STEP),
            in_specs=[pl.BlockSpec((tq, ATTN_IN), lambda b, p, pt: (b, 0))] + specs + specs,
            out_specs=pl.BlockSpec((tq, ATTN_WIDTH), lambda b, p, pt: (b, 0)),
            scratch_shapes=[pltpu.VMEM((N_KV_HEADS, rows, 1), F32),
                            pltpu.VMEM((N_KV_HEADS, rows, HEAD_DIM), F32)]),
        out_shape=jax.ShapeDtypeStruct((batch * tq, ATTN_WIDTH), F32),
        compiler_params=_cparams("parallel", "arbitrary"),
        name="sb_attention_sample",
    )(page_table, proj, *([k_pool] * PAGES_PER_STEP), *([v_pool] * PAGES_PER_STEP))


def _gdn_prep_kernel(u_ref, c0_ref, w_ref, o_ref, carry_scr, *, tt, tc):
    c = pl.program_id(1)
    t = pl.program_id(2)

    @pl.when(t == 0)
    def _():
        carry_scr[...] = c0_ref[0]

    u = u_ref[...]
    prev = carry_scr[...]
    w = w_ref[...]
    row = lax.broadcasted_iota(jnp.int32, (SUBLANES, tc), 0)
    y = u * w[GDN_CONV - 1:GDN_CONV, :]
    for i in range(1, GDN_CONV):
        ru = pltpu.roll(u, i, 0)
        top = jnp.where(row < i, pltpu.roll(prev, i, 0), ru[:SUBLANES])
        shifted = top if tt == SUBLANES else jnp.concatenate([top, ru[SUBLANES:]], axis=0)
        y = y + shifted * w[GDN_CONV - 1 - i:GDN_CONV - i, :]
    carry_scr[...] = u[tt - SUBLANES:, :]
    a = _silu(y)

    n_q_tiles = GDN_KEY_WIDTH // tc

    @pl.when(c < 2 * n_q_tiles)
    def _():
        scale = jnp.where(c < n_q_tiles, GDN_HEAD_DIM ** -0.5, 1.0)
        for s in range(tc // GDN_HEAD_DIM):
            sl = slice(s * GDN_HEAD_DIM, (s + 1) * GDN_HEAD_DIM)
            x = a[:, sl]
            o_ref[:, sl] = x * lax.rsqrt(jnp.sum(x * x, axis=-1, keepdims=True) + EPS) * scale

    @pl.when(c >= 2 * n_q_tiles)
    def _():
        o_ref[...] = a


def gdn_prep(proj, conv_state, conv_w, batch, seq, tc=512):
    tt = min(seq, 256)
    nt = seq // tt
    c0 = jnp.pad(conv_state, ((0, 0), (SUBLANES - (GDN_CONV - 1), 0), (0, 0)))
    return pl.pallas_call(
        functools.partial(_gdn_prep_kernel, tt=tt, tc=tc),
        grid=(batch, GDN_CONV_CH // tc, nt),
        in_specs=[pl.BlockSpec((tt, tc), lambda b, c, t: (b * nt + t, c)),
                  pl.BlockSpec((1, SUBLANES, tc), lambda b, c, t: (b, 0, c)),
                  pl.BlockSpec((GDN_CONV, tc), lambda b, c, t: (0, c))],
        out_specs=pl.BlockSpec((tt, tc), lambda b, c, t: (b * nt + t, c)),
        out_shape=jax.ShapeDtypeStruct((batch * seq, GDN_CONV_CH), F32),
        scratch_shapes=[pltpu.VMEM((SUBLANES, tc), F32)],
        compiler_params=_cparams("parallel", "parallel", "arbitrary"),
        name="gdn_prep",
    )(proj, c0, conv_w)


GDN_ROWS = 128
GDN_REP = GDN_V_HEADS // GDN_K_HEADS


def _gdn_delta_kernel(q_ref, k_ref, v_ref, bd_ref, z_ref, alog_ref, dt_ref, onorm_ref, s0_ref,
                      o_ref, s_ref, u_scr, w_scr, in_scr, qd_scr, kdt_scr, gl_scr, gt_scr, *, seq):
    hk = pl.program_id(1)
    c = GDN_ROWS
    n_chunks = max(seq // c, 1)
    unroll = gt_scr.shape[0]
    ri = lax.broadcasted_iota(jnp.int32, (c, c), 0)
    ci = lax.broadcasted_iota(jnp.int32, (c, c), 1)
    lane = lax.broadcasted_iota(jnp.int32, (c, LANES), 1)
    row = lax.broadcasted_iota(jnp.int32, (c, 1), 0)
    tril = ri >= ci
    tril_f = jnp.where(tril, 1.0, 0.0)
    neg_a = -jnp.exp(alog_ref[...])

    def chunk_rows(i):
        return pl.ds(0, c) if n_chunks == 1 else pl.ds(pl.multiple_of(i * c, c), c)

    def rows(ref, i):
        if seq < c:
            return _pad_rows(ref[...], c)
        return ref[chunk_rows(i), :]

    def prepare(i, _):
        ms, rhss = [], []
        for j in range(unroll):
            ch = i * unroll + j
            sl = chunk_rows(ch)
            q, k, v2, raw = rows(q_ref, ch), rows(k_ref, ch), rows(v_ref, ch), rows(bd_ref, ch)
            k16 = k.astype(BF16)
            kk = _dot_nt(k16, k16)
            qk = _dot_nt(q.astype(BF16), k16)
            sig = jax.nn.sigmoid(raw)
            g_all = jnp.where(row < seq, neg_a * _softplus(raw + dt_ref[...]), 0.0)
            gcx = _dot(tril_f, g_all, HI)
            gt_scr[j] = gcx.T
            for e in range(GDN_REP):
                hv = hk * GDN_REP + e
                beta = jnp.sum(jnp.where(lane == hv, sig, 0.0), axis=1, keepdims=True)
                gc = jnp.sum(jnp.where(lane == hv + GDN_V_HEADS, gcx, 0.0), axis=1, keepdims=True)
                g_row = gt_scr[j, pl.ds(hv + GDN_V_HEADS, 1), :]
                decay = jnp.exp(jnp.where(tril, gc - g_row, NEG))
                ms.append(-jnp.where(ri > ci, kk * beta * decay, 0.0))
                egc = jnp.exp(gc)
                v = v2[:, e * GDN_HEAD_DIM:(e + 1) * GDN_HEAD_DIM]
                rhss.append(jnp.concatenate([v * beta, k * (beta * egc)], axis=1))
                g_last = gc[c - 1:c, :]
                in_scr[e, sl, :] = (qk * decay).astype(BF16)
                qd_scr[e, sl, :] = (q * egc).astype(BF16)
                kdt_scr[e, ch] = (k * jnp.exp(g_last - gc)).T.astype(BF16)
                gl_scr[e, ch] = jnp.broadcast_to(jnp.exp(g_last), (SUBLANES, LANES))
        nmat = jnp.stack(ms)
        pw = _split_bf16(nmat)
        for _ in range(int(math.log2(c)) - 1):
            sq = _bdot_split(pw, pw)
            pw = _split_bf16(sq)
            nmat = nmat + sq + _bdot_split(_split_bf16(nmat), pw)
        rhs = jnp.stack(rhss)
        sol = rhs + _bdot(nmat.astype(BF16), rhs.astype(BF16))
        for j in range(unroll):
            sl = chunk_rows(i * unroll + j)
            for e in range(GDN_REP):
                x = sol[j * GDN_REP + e]
                u_scr[e, sl, :] = x[:, :GDN_HEAD_DIM]
                w_scr[e, sl, :] = x[:, GDN_HEAD_DIM:].astype(BF16)
        return 0

    def advance(i, states):
        sl = chunk_rows(i)
        z2 = rows(z_ref, i)
        out = []
        for e in range(GDN_REP):
            s = states[e]
            s16 = s.astype(BF16)
            v_new = u_scr[e, sl, :] - _dot(w_scr[e, sl, :], s16)
            v16 = v_new.astype(BF16)
            o = _dot(qd_scr[e, sl, :], s16) + _dot(in_scr[e, sl, :], v16)
            out.append(s * gl_scr[e, i][0:1, :] + _dot(kdt_scr[e, i], v16))
            ms = jnp.mean(o * o, axis=-1, keepdims=True)
            z = z2[:, e * GDN_HEAD_DIM:(e + 1) * GDN_HEAD_DIM]
            o = (o * lax.rsqrt(ms + EPS) * onorm_ref[...] * _silu(z)).astype(o_ref.dtype)
            cols = slice(e * GDN_HEAD_DIM, (e + 1) * GDN_HEAD_DIM)
            if seq < c:
                o_ref[:, cols] = o[:seq]
            else:
                o_ref[sl, cols] = o
        return tuple(out)

    states = tuple(s0_ref[0, e] for e in range(GDN_REP))
    if n_chunks == 1:
        prepare(0, 0)
        states = advance(0, states)
    else:
        lax.fori_loop(0, n_chunks // unroll, prepare, 0)
        states = lax.fori_loop(0, n_chunks, advance, states)
    for e in range(GDN_REP):
        s_ref[0, e] = states[e]


def gdn_delta(act, proj, a_log, dt_bias, o_norm, s0, batch, seq, out_dtype):
    assert seq % GDN_ROWS == 0 or seq < GDN_ROWS
    pad = jnp.zeros((GDN_V_HEADS,), F32)
    tail = jnp.zeros((LANES - 2 * GDN_V_HEADS,), F32)
    alog = jnp.concatenate([pad, a_log, tail]).reshape(1, LANES)
    dt = jnp.concatenate([pad, dt_bias, tail]).reshape(1, LANES)
    blk = (seq, GDN_HEAD_DIM)
    wide = (seq, GDN_REP * GDN_HEAD_DIM)
    state_blk = (1, GDN_REP, GDN_HEAD_DIM, GDN_HEAD_DIM)
    rows = max(seq, GDN_ROWS)
    n_chunks = rows // GDN_ROWS
    return pl.pallas_call(
        functools.partial(_gdn_delta_kernel, seq=seq),
        grid=(batch, GDN_K_HEADS),
        in_specs=[pl.BlockSpec(blk, lambda b, h: (b, h)),
                  pl.BlockSpec(blk, lambda b, h: (b, GDN_K_HEADS + h)),
                  pl.BlockSpec(wide, lambda b, h: (b, 2 * GDN_K_HEADS // GDN_REP + h)),
                  pl.BlockSpec((seq, LANES), lambda b, h: (b, GDN_BD_COL)),
                  pl.BlockSpec(wide, lambda b, h: (b, GDN_Z_COL // GDN_REP + h)),
                  pl.BlockSpec((1, LANES), lambda b, h: (0, 0)),
                  pl.BlockSpec((1, LANES), lambda b, h: (0, 0)),
                  pl.BlockSpec((1, GDN_HEAD_DIM), lambda b, h: (0, 0)),
                  pl.BlockSpec(state_blk, lambda b, h: (b, h, 0, 0))],
        out_specs=[pl.BlockSpec(wide, lambda b, h: (b, h)),
                   pl.BlockSpec(state_blk, lambda b, h: (b, h, 0, 0))],
        scratch_shapes=[pltpu.VMEM((GDN_REP, rows, GDN_HEAD_DIM), F32),
                        pltpu.VMEM((GDN_REP, rows, GDN_HEAD_DIM), BF16),
                        pltpu.VMEM((GDN_REP, rows, GDN_ROWS), BF16),
                        pltpu.VMEM((GDN_REP, rows, GDN_HEAD_DIM), BF16),
                        pltpu.VMEM((GDN_REP, n_chunks, GDN_HEAD_DIM, GDN_ROWS), BF16),
                        pltpu.VMEM((GDN_REP, n_chunks, SUBLANES, LANES), F32),
                        pltpu.VMEM((2 if n_chunks % 2 == 0 else 1, GDN_ROWS, LANES), F32)],
        out_shape=[jax.ShapeDtypeStruct((batch * seq, GDN_VAL_WIDTH), out_dtype),
                   jax.ShapeDtypeStruct((batch, GDN_V_HEADS, GDN_HEAD_DIM, GDN_HEAD_DIM), F32)],
        compiler_params=_cparams("parallel", "parallel"),
        name="gdn_delta",
    )(act, act, act, proj, proj, alog, dt, o_norm.reshape(1, GDN_HEAD_DIM), s0)


def _rel_bucket(dist):
    max_exact = REL_BUCKETS // 2
    n = jnp.maximum(dist, 0)
    large = max_exact + (jnp.log(jnp.maximum(n, 1).astype(F32) / max_exact)
                         / math.log(REL_MAX_DIST / max_exact) * (REL_BUCKETS - max_exact)).astype(jnp.int32)
    large = jnp.minimum(large, REL_BUCKETS - 1)
    return jnp.where(n < max_exact, n, large)


def _bias_table_kernel(rbt_ref, onehot_ref, o_ref):
    o_ref[...] = _dot(rbt_ref[...], onehot_ref[...], HI)


def bias_by_distance(rel_bias, dist):
    bucket = _rel_bucket(dist)
    onehot = (bucket[None, :] == jnp.arange(REL_BUCKETS)[:, None]) & (dist[None, :] >= 0)
    return pl.pallas_call(
        _bias_table_kernel,
        out_shape=jax.ShapeDtypeStruct((N_HEADS, dist.shape[0]), F32),
        compiler_params=pltpu.CompilerParams(vmem_limit_bytes=VMEM_LIMIT),
        name="bias_table",
    )(rel_bias.T, onehot.astype(F32))


def _toeplitz(window_row, rows, shift):
    x = jnp.broadcast_to(window_row, (rows, window_row.shape[1]))
    return pltpu.roll(x, shift, 1, stride=1, stride_axis=0)


def _select_topk(gate, n_valid, n_blocks):
    lane = lax.broadcasted_iota(jnp.int32, gate.shape, 1)
    valid = lane < n_valid
    gm = jnp.where(valid, gate, -jnp.inf)
    cnt = jnp.zeros(gate.shape, jnp.int32)
    for m in range(n_blocks):
        col = gm[:, m:m + 1]
        beats = (col > gm) | ((col == gm) & (lane > m))
        cnt = cnt + jnp.where(beats, 1, 0)
    return jnp.where(valid & (cnt < MOBA_TOPK), 1.0, 0.0)


def _moba_tile(qs, k, v, bias, mask, m, l, acc):
    s = _dot_nt(qs, k) * SCALE + bias
    s = jnp.where(mask, s, NEG)
    m_new = jnp.maximum(m, jnp.max(s, axis=1, keepdims=True))
    p = jnp.where(mask, jnp.exp(s - m_new), 0.0)
    alpha = jnp.exp(m - m_new)
    l = alpha * l + jnp.sum(p, axis=1, keepdims=True)
    acc = alpha * acc + _dot(p.astype(BF16), v)
    return m_new, l, acc


def _moba_prompt_kernel(q_ref, k_ref, v_ref, gate_ref, tb_ref, o_ref,
                        means_scr, bias_scr, sel_scr, m_scr, l_scr, acc_scr, *, tq, n_blocks):
    h = pl.program_id(0)
    b = pl.program_id(1)
    qi = pl.program_id(2)
    rows = GROUP * tq

    @pl.when(qi == 0)
    def _():
        means_scr[...] = jnp.zeros_like(means_scr)
        means_scr[0:n_blocks, :] = jnp.mean(k_ref[...].reshape(n_blocks, MOBA_BLOCK, HEAD_DIM), axis=1)

    @pl.when(b == 0)
    def _():
        for g in range(GROUP):
            wrow = tb_ref[pl.ds((h * GROUP + g) * n_blocks + qi, 1), :]
            bias_scr[qi, pl.ds(g * tq, tq), :] = _toeplitz(wrow, tq, tq + 1)[:, :tq]

    qs32 = _stack_heads(q_ref[...], GROUP)
    qs = qs32.astype(BF16)
    sel_scr[...] = _select_topk(_dot_nt(qs32, means_scr[...], HI), qi, n_blocks)
    lane = lax.broadcasted_iota(jnp.int32, (rows, LANES), 1)
    t_loc = lax.broadcasted_iota(jnp.int32, (rows, tq), 0) & (tq - 1)
    s_loc = lax.broadcasted_iota(jnp.int32, (rows, tq), 1)

    def load(ref, blk):
        return ref[pl.ds(pl.multiple_of(blk * tq, tq), tq), :].astype(BF16)

    m, l, acc = _moba_tile(qs, load(k_ref, qi), load(v_ref, qi), bias_scr[0], s_loc <= t_loc,
                           jnp.full((rows, 1), NEG, F32), jnp.zeros((rows, 1), F32),
                           jnp.zeros((rows, HEAD_DIM), F32))
    m_scr[...] = m
    l_scr[...] = l
    acc_scr[...] = acc

    def body(kb, _):
        selcol = jnp.sum(jnp.where(lane == kb, sel_scr[...], 0.0), axis=1, keepdims=True) > 0.5
        m, l, acc = _moba_tile(qs, load(k_ref, kb), load(v_ref, kb), bias_scr[qi - kb], selcol,
                               m_scr[...], l_scr[...], acc_scr[...])
        m_scr[...] = m
        l_scr[...] = l
        acc_scr[...] = acc
        return 0

    lax.fori_loop(0, qi, body, 0)
    o = _unstack_heads(acc_scr[...] / l_scr[...], GROUP)
    o_ref[...] = (o * _silu(gate_ref[...])).astype(o_ref.dtype)


def moba_attention_prompt(proj, rel_bias, batch, seq):
    tq = MOBA_BLOCK
    assert seq % tq == 0
    nb = seq // tq
    gw = GROUP * HEAD_DIM
    rows = GROUP * tq
    dist = (jnp.arange(nb)[:, None] * tq + (tq - 1) - jnp.arange(2 * tq)[None, :]).reshape(-1)
    table = bias_by_distance(rel_bias, dist).reshape(N_HEADS * nb, 2 * tq)
    return pl.pallas_call(
        functools.partial(_moba_prompt_kernel, tq=tq, n_blocks=nb),
        grid=(N_KV_HEADS, batch, nb),
        in_specs=[pl.BlockSpec((tq, gw), lambda h, b, i: (b * nb + i, h)),
                  pl.BlockSpec((seq, HEAD_DIM), lambda h, b, i: (b, K_COL + h)),
                  pl.BlockSpec((seq, HEAD_DIM), lambda h, b, i: (b, V_COL + h)),
                  pl.BlockSpec((tq, gw), lambda h, b, i: (b * nb + i, GATE_COL + h)),
                  pl.BlockSpec((N_HEADS * nb, 2 * tq), lambda h, b, i: (0, 0))],
        out_specs=pl.BlockSpec((tq, gw), lambda h, b, i: (b * nb + i, h)),
        out_shape=jax.ShapeDtypeStruct((batch * seq, ATTN_WIDTH), BF16),
        scratch_shapes=[pltpu.VMEM((LANES, HEAD_DIM), F32),
                        pltpu.VMEM((nb, rows, tq), F32),
                        pltpu.VMEM((rows, LANES), F32),
                        pltpu.VMEM((rows, 1), F32),
                        pltpu.VMEM((rows, 1), F32),
                        pltpu.VMEM((rows, HEAD_DIM), F32)],
        compiler_params=_cparams("arbitrary", "arbitrary", "arbitrary"),
        name="moba_attention_prompt",
    )(proj, proj, proj, proj, table)


PAGES_PER_BLOCK = MOBA_BLOCK // PAGE_SIZE
BLOCKS_PER_STEP = PAGES_PER_STEP // PAGES_PER_BLOCK


def _moba_means_kernel(pt_ref, *refs):
    del pt_ref
    page_refs, o_ref = refs[:-1], refs[-1]
    for blk in range(BLOCKS_PER_STEP):
        pages = page_refs[blk * PAGES_PER_BLOCK:(blk + 1) * PAGES_PER_BLOCK]
        for h in range(N_KV_HEADS):
            total = jnp.sum(_head_pages(pages, h), axis=0, keepdims=True)
            o_ref[0, blk, :, h * HEAD_DIM:(h + 1) * HEAD_DIM] = total * (1.0 / MOBA_BLOCK)


def moba_block_means(k_pool, page_offset, page_table):
    batch, n_pages = page_table.shape
    assert n_pages % PAGES_PER_STEP == 0
    nb = n_pages // PAGES_PER_BLOCK
    return pl.pallas_call(
        _moba_means_kernel,
        grid_spec=pltpu.PrefetchScalarGridSpec(
            num_scalar_prefetch=1,
            grid=(batch, n_pages // PAGES_PER_STEP),
            in_specs=_page_specs(n_pages, page_offset, latest_first=False),
            out_specs=pl.BlockSpec((1, BLOCKS_PER_STEP, 1, KV_WIDTH), lambda b, n, pt: (b, n, 0, 0))),
        out_shape=jax.ShapeDtypeStruct((batch, nb, 1, KV_WIDTH), F32),
        compiler_params=_cparams("parallel", "parallel"),
        name="moba_block_means",
    )(page_table, *([k_pool] * PAGES_PER_STEP))


def _moba_sample_kernel(pt_ref, proj_ref, means_ref, tb_ref, *refs, tq, n_pages):
    del pt_ref
    k_refs, v_refs = refs[:PAGES_PER_STEP], refs[PAGES_PER_STEP:2 * PAGES_PER_STEP]
    o_ref, sel_scr, m_scr, l_scr, acc_scr = refs[2 * PAGES_PER_STEP:]
    p = pl.program_id(1)
    rows = GROUP * tq
    n_past_blocks = n_pages // PAGES_PER_BLOCK
    lane = lax.broadcasted_iota(jnp.int32, (rows, LANES), 1)

    def q_rows(h):
        return _stack_heads(proj_ref[:, h * GROUP * HEAD_DIM:(h + 1) * GROUP * HEAD_DIM], GROUP)

    def bias_rows(h, page):
        tiles = []
        for g in range(GROUP):
            wrow = tb_ref[pl.ds((h * GROUP + g) * (n_pages + 1) + page, 1), :]
            tiles.append(_toeplitz(wrow, tq, PAGE_SIZE + 1)[:, :PAGE_SIZE])
        return jnp.concatenate(tiles, axis=0)

    @pl.when(p == 0)
    def _():
        t_loc = lax.broadcasted_iota(jnp.int32, (rows, PAGE_SIZE), 0) & (tq - 1)
        s_loc = lax.broadcasted_iota(jnp.int32, (rows, PAGE_SIZE), 1)
        for h in range(N_KV_HEADS):
            sl = slice(h * HEAD_DIM, (h + 1) * HEAD_DIM)
            qs32 = q_rows(h)
            means = _pad_rows(means_ref[0, :, sl], LANES)
            sel_scr[h] = _select_topk(_dot_nt(qs32, means, HI), n_past_blocks, n_past_blocks)
            kc = ATTN_WIDTH + h * HEAD_DIM
            vc = ATTN_WIDTH + KV_WIDTH + h * HEAD_DIM
            k_new = _pad_rows(proj_ref[:, kc:kc + HEAD_DIM], PAGE_SIZE).astype(BF16)
            v_new = _pad_rows(proj_ref[:, vc:vc + HEAD_DIM], PAGE_SIZE).astype(BF16)
            m, l, acc = _moba_tile(qs32.astype(BF16), k_new, v_new, bias_rows(h, n_pages), s_loc <= t_loc,
                                   jnp.full((rows, 1), NEG, F32), jnp.zeros((rows, 1), F32),
                                   jnp.zeros((rows, HEAD_DIM), F32))
            m_scr[h] = m
            l_scr[h] = l
            acc_scr[h] = acc

    for h in range(N_KV_HEADS):
        sel = sel_scr[h]
        mask = []
        for j in range(BLOCKS_PER_STEP):
            col = jnp.sum(jnp.where(lane == p * BLOCKS_PER_STEP + j, sel, 0.0), axis=1, keepdims=True) > 0.5
            mask.append(jnp.broadcast_to(col, (rows, MOBA_BLOCK)))
        bias = jnp.concatenate([bias_rows(h, p * PAGES_PER_STEP + j) for j in range(PAGES_PER_STEP)], axis=1)
        m, l, acc = _moba_tile(q_rows(h).astype(BF16), _head_pages(k_refs, h).astype(BF16),
                               _head_pages(v_refs, h).astype(BF16), bias, jnp.concatenate(mask, axis=1),
                               m_scr[h], l_scr[h], acc_scr[h])
        m_scr[h] = m
        l_scr[h] = l
        acc_scr[h] = acc

    @pl.when(p == pl.num_programs(1) - 1)
    def _():
        gc = ATTN_WIDTH + 2 * KV_WIDTH
        for h in range(N_KV_HEADS):
            sl = slice(h * GROUP * HEAD_DIM, (h + 1) * GROUP * HEAD_DIM)
            gate = proj_ref[:, gc + sl.start:gc + sl.stop]
            o_ref[:, sl] = _unstack_heads(acc_scr[h] / l_scr[h], GROUP) * _silu(gate)


def moba_attention_sample(proj, k_pool, v_pool, page_offset, page_table, rel_bias, tq):
    batch, n_pages = page_table.shape
    assert n_pages % PAGES_PER_STEP == 0 and tq <= MOBA_BLOCK
    rows = GROUP * tq
    nb = n_pages // PAGES_PER_BLOCK
    assert nb <= LANES
    means = moba_block_means(k_pool, page_offset, page_table).reshape(batch, nb, KV_WIDTH)
    dist = ((n_pages - jnp.arange(n_pages + 1))[:, None] * PAGE_SIZE + (PAGE_SIZE - 1)
            - jnp.arange(2 * PAGE_SIZE)[None, :]).reshape(-1)
    table = bias_by_distance(rel_bias, dist).reshape(N_HEADS * (n_pages + 1), 2 * PAGE_SIZE)
    specs = _page_specs(n_pages, page_offset, latest_first=False)
    return pl.pallas_call(
        functools.partial(_moba_sample_kernel, tq=tq, n_pages=n_pages),
        grid_spec=pltpu.PrefetchScalarGridSpec(
            num_scalar_prefetch=1,
            grid=(batch, n_pages // PAGES_PER_STEP),
            in_specs=[pl.BlockSpec((tq, ATTN_IN), lambda b, p, pt: (b, 0)),
                      pl.BlockSpec((1, nb, KV_WIDTH), lambda b, p, pt: (b, 0, 0)),
                      pl.BlockSpec((N_HEADS * (n_pages + 1), 2 * PAGE_SIZE), lambda b, p, pt: (0, 0))]
            + specs + specs,
            out_specs=pl.BlockSpec((tq, ATTN_WIDTH), lambda b, p, pt: (b, 0)),
            scratch_shapes=[pltpu.VMEM((N_KV_HEADS, rows, LANES), F32),
                            pltpu.VMEM((N_KV_HEADS, rows, 1), F32),
                            pltpu.VMEM((N_KV_HEADS, rows, 1), F32),
                            pltpu.VMEM((N_KV_HEADS, rows, HEAD_DIM), F32)]),
        out_shape=jax.ShapeDtypeStruct((batch * tq, ATTN_WIDTH), F32),
        compiler_params=_cparams("parallel", "arbitrary"),
        name="moba_attention_sample",
    )(page_table, proj, means, table, *([k_pool] * PAGES_PER_STEP), *([v_pool] * PAGES_PER_STEP))


N_MIXERS = 3
GDN_IN_PADDED = -(-GDN_IN // 512) * 512


def _new_kv(proj, batch, seq):
    k = proj[:, ATTN_WIDTH:ATTN_WIDTH + KV_WIDTH].reshape(batch, seq, N_KV_HEADS, HEAD_DIM)
    v = proj[:, ATTN_WIDTH + KV_WIDTH:ATTN_WIDTH + 2 * KV_WIDTH].reshape(batch, seq, N_KV_HEADS, HEAD_DIM)
    return k, v


def kernel(x_prompt, x_sample, cache_sb_k, cache_sb_v, state_gdn_conv, state_gdn_rec, cache_moba_k, cache_moba_v, page_table, norm_g, sb_w_in, sb_w_out, gdn_w_in, gdn_conv_w, gdn_a_log, gdn_dt_bias, gdn_o_norm, gdn_w_out, moba_w_in, moba_q_norm, moba_k_norm, moba_w_out, rel_bias):
    bp, tp, d = x_prompt.shape
    bs, ts, _ = x_sample.shape
    n_phys = cache_sb_k.shape[1]
    yp = x_prompt.reshape(bp * tp, d)
    ys = x_sample.reshape(bs * ts, d)
    outs = {name: [] for name in ("sb_kp", "sb_vp", "sb_ks", "sb_vs", "gdn_cp", "gdn_sp", "gdn_cs", "gdn_ss",
                                  "mb_kp", "mb_vp", "mb_ks", "mb_vs")}
    for layer in range(norm_g.shape[0]):
        kind = layer % N_MIXERS
        j = layer // N_MIXERS
        g = norm_g[layer]
        if kind == 0:
            w_in = sb_w_in[j].astype(BF16)
            w_out = sb_w_out[j].astype(BF16)
            pp = norm_matmul(yp, g, w_in)
            ps = norm_matmul(ys, g, w_in)
            op = sb_attention_prompt(pp, bp, tp)
            os_ = sb_attention_sample(ps, _flat_pool(cache_sb_k), _flat_pool(cache_sb_v), j * n_phys, page_table, ts)
            kp, vp = _new_kv(pp, bp, tp)
            ks, vs = _new_kv(ps, bs, ts)
            outs["sb_kp"].append(kp); outs["sb_vp"].append(vp); outs["sb_ks"].append(ks); outs["sb_vs"].append(vs)
        elif kind == 1:
            w_in = jnp.pad(gdn_w_in[j], ((0, 0), (0, GDN_IN_PADDED - GDN_IN))).astype(BF16)
            w_out = gdn_w_out[j].astype(BF16)
            pp = norm_matmul(yp, g, w_in)
            ps = norm_matmul(ys, g, w_in)
            act_p = gdn_prep(pp, jnp.zeros((bp, GDN_CONV - 1, GDN_CONV_CH), F32), gdn_conv_w[j], bp, tp)
            act_s = gdn_prep(ps, state_gdn_conv[j], gdn_conv_w[j], bs, ts)
            s0 = jnp.zeros((bp,) + state_gdn_rec.shape[2:], F32)
            op, sp = gdn_delta(act_p, pp, gdn_a_log[j], gdn_dt_bias[j], gdn_o_norm[j], s0, bp, tp, BF16)
            os_, ss = gdn_delta(act_s, ps, gdn_a_log[j], gdn_dt_bias[j], gdn_o_norm[j], state_gdn_rec[j],
                                bs, ts, F32)
            outs["gdn_cp"].append(pp.reshape(bp, tp, -1)[:, tp - (GDN_CONV - 1):, :GDN_CONV_CH])
            outs["gdn_cs"].append(ps.reshape(bs, ts, -1)[:, ts - (GDN_CONV - 1):, :GDN_CONV_CH])
            outs["gdn_sp"].append(sp); outs["gdn_ss"].append(ss)
        else:
            w_in = moba_w_in[j].astype(BF16)
            w_out = moba_w_out[j].astype(BF16)
            head_gain = jnp.concatenate([jnp.tile(moba_q_norm[j], N_HEADS), jnp.tile(moba_k_norm[j], N_KV_HEADS),
                                         jnp.ones((ATTN_IN - ATTN_WIDTH - KV_WIDTH,), F32)]).reshape(1, ATTN_IN)
            pp = norm_matmul(yp, g, w_in, head_gain, ATTN_WIDTH + KV_WIDTH)
            ps = norm_matmul(ys, g, w_in, head_gain, ATTN_WIDTH + KV_WIDTH)
            op = moba_attention_prompt(pp, rel_bias, bp, tp)
            os_ = moba_attention_sample(ps, _flat_pool(cache_moba_k), _flat_pool(cache_moba_v), j * n_phys,
                                        page_table, rel_bias, ts)
            kp, vp = _new_kv(pp, bp, tp)
            ks, vs = _new_kv(ps, bs, ts)
            outs["mb_kp"].append(kp); outs["mb_vp"].append(vp); outs["mb_ks"].append(ks); outs["mb_vs"].append(vs)
        yp = matmul_residual(op, w_out, yp)
        ys = matmul_residual(os_, w_out, ys)
    stack = lambda name: jnp.stack(outs[name])
    return (yp.reshape(bp, tp, d), ys.reshape(bs, ts, d),
            stack("sb_kp"), stack("sb_vp"), stack("sb_ks"), stack("sb_vs"),
            stack("gdn_cp"), stack("gdn_sp"), stack("gdn_cs"), stack("gdn_ss"),
            stack("mb_kp"), stack("mb_vp"), stack("mb_ks"), stack("mb_vs"))
```

```python
import functools
import math

import jax
import jax.numpy as jnp
from jax import lax
from jax.experimental import pallas as pl
from jax.experimental.pallas import tpu as pltpu

F32 = jnp.float32
BF16 = jnp.bfloat16
HI = lax.Precision.HIGHEST

LANES = 128
SUBLANES = 8
VMEM_LIMIT = 56 * 1024 * 1024

HEAD_DIM = 128
N_HEADS = 16
N_KV_HEADS = 4
GROUP = N_HEADS // N_KV_HEADS
ATTN_WIDTH = N_HEADS * HEAD_DIM
KV_WIDTH = N_KV_HEADS * HEAD_DIM
ATTN_IN = 2 * ATTN_WIDTH + 2 * KV_WIDTH
Q_COL = 0
K_COL = ATTN_WIDTH // HEAD_DIM
V_COL = K_COL + N_KV_HEADS
GATE_COL = (ATTN_WIDTH + 2 * KV_WIDTH) // (GROUP * HEAD_DIM)
PAGE_SIZE = 128

GDN_HEAD_DIM = 128
GDN_K_HEADS = 16
GDN_V_HEADS = 32
GDN_KEY_WIDTH = GDN_K_HEADS * GDN_HEAD_DIM
GDN_VAL_WIDTH = GDN_V_HEADS * GDN_HEAD_DIM
GDN_CONV_CH = 2 * GDN_KEY_WIDTH + GDN_VAL_WIDTH
GDN_CONV = 4
GDN_CHUNK = 64
GDN_IN = GDN_CONV_CH + GDN_VAL_WIDTH + 2 * GDN_V_HEADS
GDN_Z_COL = GDN_CONV_CH // GDN_HEAD_DIM
GDN_BD_COL = (GDN_CONV_CH + GDN_VAL_WIDTH) // LANES

MOBA_BLOCK = 256
MOBA_TOPK = 3
REL_BUCKETS = 32
REL_MAX_DIST = 4096

EPS = 1e-6
NEG = -1e30
SCALE = HEAD_DIM ** -0.5

_NT = (((1,), (1,)), ((), ()))


def _cparams(*sem):
    return pltpu.CompilerParams(dimension_semantics=sem, vmem_limit_bytes=VMEM_LIMIT)


def _softplus(z):
    return jnp.maximum(z, 0.0) + jnp.log1p(jnp.exp(-jnp.abs(z)))


def _silu(x):
    return x * jax.nn.sigmoid(x)


def _dot(a, b, precision=None):
    return jnp.dot(a, b, preferred_element_type=F32, precision=precision)


def _dot_nt(a, b, precision=None):
    return lax.dot_general(a, b, _NT, preferred_element_type=F32, precision=precision)


def _split_bf16(x):
    hi = x.astype(BF16)
    return hi, (x - hi.astype(F32)).astype(BF16)


def _bdot(a, b):
    return jnp.einsum("bij,bjk->bik", a, b, preferred_element_type=F32)


def _bdot_split(a, b):
    return _bdot(a[0], b[0]) + (_bdot(a[0], b[1]) + _bdot(a[1], b[0]))


def _norm_matmul_kernel(x_ref, g_ref, w_ref, hg_ref, o_ref, h_scr, *, n_norm_tiles, tn):
    j = pl.program_id(1)

    @pl.when(j == 0)
    def _():
        x = x_ref[...]
        ms = jnp.mean(x * x, axis=-1, keepdims=True)
        h_scr[...] = (x * lax.rsqrt(ms + EPS) * g_ref[...]).astype(BF16)

    acc = _dot(h_scr[...], w_ref[...])
    if n_norm_tiles == 0:
        o_ref[...] = acc
    else:
        @pl.when(j < n_norm_tiles)
        def _():
            for s in range(tn // HEAD_DIM):
                sl = slice(s * HEAD_DIM, (s + 1) * HEAD_DIM)
                a = acc[:, sl]
                ms = jnp.mean(a * a, axis=-1, keepdims=True)
                o_ref[:, sl] = a * lax.rsqrt(ms + EPS) * hg_ref[:, sl]

        @pl.when(j >= n_norm_tiles)
        def _():
            o_ref[...] = acc


def norm_matmul(x, g, w, head_gain=None, n_norm_cols=0, tn=512):
    m, d = x.shape
    n = w.shape[1]
    tm = min(m, 1024)
    assert m % tm == 0 and n % tn == 0 and n_norm_cols % tn == 0
    if head_gain is None:
        head_gain = jnp.ones((1, n), F32)
    return pl.pallas_call(
        functools.partial(_norm_matmul_kernel, n_norm_tiles=n_norm_cols // tn, tn=tn),
        grid=(m // tm, n // tn),
        in_specs=[pl.BlockSpec((tm, d), lambda i, j: (i, 0)),
                  pl.BlockSpec((1, d), lambda i, j: (0, 0)),
                  pl.BlockSpec((d, tn), lambda i, j: (0, j)),
                  pl.BlockSpec((1, tn), lambda i, j: (0, j))],
        out_specs=pl.BlockSpec((tm, tn), lambda i, j: (i, j)),
        out_shape=jax.ShapeDtypeStruct((m, n), F32),
        scratch_shapes=[pltpu.VMEM((tm, d), BF16)],
        compiler_params=_cparams("parallel", "arbitrary"),
        name="norm_matmul",
    )(x, g.reshape(1, d), w, head_gain)


def _matmul_residual_kernel(a_ref, w_ref, r_ref, o_ref):
    o_ref[...] = r_ref[...] + _dot(a_ref[...].astype(BF16), w_ref[...])


def matmul_residual(a, w, res, tn=512):
    m, k = a.shape
    n = w.shape[1]
    tm = min(m, 512)
    assert m % tm == 0 and n % tn == 0
    return pl.pallas_call(
        _matmul_residual_kernel,
        grid=(m // tm, n // tn),
        in_specs=[pl.BlockSpec((tm, k), lambda i, j: (i, 0)),
                  pl.BlockSpec((k, tn), lambda i, j: (0, j)),
                  pl.BlockSpec((tm, tn), lambda i, j: (i, j))],
        out_specs=pl.BlockSpec((tm, tn), lambda i, j: (i, j)),
        out_shape=jax.ShapeDtypeStruct((m, n), F32),
        compiler_params=_cparams("parallel", "parallel"),
        name="matmul_residual",
    )(a, w, res)


def _stack_heads(x, n):
    return jnp.concatenate([x[:, g * HEAD_DIM:(g + 1) * HEAD_DIM] for g in range(n)], axis=0)


def _unstack_heads(x, n):
    t = x.shape[0] // n
    return jnp.concatenate([x[g * t:(g + 1) * t] for g in range(n)], axis=1)


def _later_matrix(tk):
    r = lax.broadcasted_iota(jnp.int32, (2 * tk, tk), 0) & (tk - 1)
    c = lax.broadcasted_iota(jnp.int32, (2 * tk, tk), 1)
    return jnp.where(r > c, -1.0, 0.0).astype(BF16)


def _sb_block(qs, k, v, carry, acc, later_mat, mask=None):
    tk = later_mat.shape[1]
    n = k.shape[0] // tk
    z = _dot_nt(qs, k)
    if mask is not None:
        z = jnp.where(mask, z, NEG)
    sp = jnp.maximum(z, 0.0) + jnp.log(1.0 + jnp.exp(-jnp.abs(z)))
    hi = sp.astype(BF16)
    lo = (sp - hi.astype(F32)).astype(BF16)
    later, total = [], []
    for j in range(n):
        sl = slice(j * tk, (j + 1) * tk)
        lt = _dot(jnp.concatenate([hi[:, sl], lo[:, sl]], axis=1), later_mat)
        later.append(lt)
        total.append(lt[:, :1] - sp[:, j * tk:j * tk + 1])
    for j in reversed(range(n)):
        later[j] = later[j] + carry
        carry = carry + total[j]
    later = later[0] if n == 1 else jnp.concatenate(later, axis=1)
    w = jnp.exp(z - sp + later)
    return carry, acc + _dot(w.astype(BF16), v)


SB_BLOCKS_PER_STEP = 4


def _sb_prompt_kernel(q_ref, k_ref, v_ref, gate_ref, o_ref, carry_scr, acc_scr, *, tq):
    qi = pl.program_id(2)
    rows = GROUP * tq
    nk = SB_BLOCKS_PER_STEP
    qs = (_stack_heads(q_ref[...], GROUP) * SCALE).astype(BF16)
    later_mat = _later_matrix(tq)
    q_pos = qi * tq + (lax.broadcasted_iota(jnp.int32, (rows, 1), 0) & (tq - 1))
    s_off = lax.broadcasted_iota(jnp.int32, (rows, nk * tq), 1)
    carry_scr[...] = jnp.zeros_like(carry_scr)
    acc_scr[...] = jnp.zeros_like(acc_scr)

    def body(i, _):
        top = qi + 1 - nk * i
        start = pl.multiple_of(jnp.maximum(top - nk, 0) * tq, tq)
        k = k_ref[pl.ds(start, nk * tq), :].astype(BF16)
        v = v_ref[pl.ds(start, nk * tq), :].astype(BF16)
        mask = s_off + start < jnp.minimum(top * tq, q_pos)
        c, a = _sb_block(qs, k, v, carry_scr[...], acc_scr[...], later_mat, mask)
        carry_scr[...] = c
        acc_scr[...] = a
        return 0

    lax.fori_loop(0, (qi + nk) // nk, body, 0)
    o = _unstack_heads(acc_scr[...], GROUP)
    o_ref[...] = (o * _silu(gate_ref[...])).astype(o_ref.dtype)


def sb_attention_prompt(proj, batch, seq, tq=128):
    assert seq % tq == 0 and seq >= SB_BLOCKS_PER_STEP * tq
    nq = seq // tq
    gw = GROUP * HEAD_DIM
    return pl.pallas_call(
        functools.partial(_sb_prompt_kernel, tq=tq),
        grid=(batch, N_KV_HEADS, nq),
        in_specs=[pl.BlockSpec((tq, gw), lambda b, h, i: (b * nq + i, h)),
                  pl.BlockSpec((seq, HEAD_DIM), lambda b, h, i: (b, K_COL + h)),
                  pl.BlockSpec((seq, HEAD_DIM), lambda b, h, i: (b, V_COL + h)),
                  pl.BlockSpec((tq, gw), lambda b, h, i: (b * nq + i, GATE_COL + h))],
        out_specs=pl.BlockSpec((tq, gw), lambda b, h, i: (b * nq + i, h)),
        out_shape=jax.ShapeDtypeStruct((batch * seq, ATTN_WIDTH), BF16),
        scratch_shapes=[pltpu.VMEM((GROUP * tq, 1), F32), pltpu.VMEM((GROUP * tq, HEAD_DIM), F32)],
        compiler_params=_cparams("parallel", "parallel", "parallel"),
        name="sb_attention_prompt",
    )(proj, proj, proj, proj)


def _pad_rows(x, rows):
    return jnp.concatenate([x, jnp.zeros((rows - x.shape[0], x.shape[1]), x.dtype)], axis=0)


PAGES_PER_STEP = 8


def _page_specs(n_pages, page_offset, latest_first):
    n_steps = n_pages // PAGES_PER_STEP

    def index_map(b, p, pt, *, j):
        step = n_steps - 1 - p if latest_first else p
        return (page_offset + pt[b, step * PAGES_PER_STEP + j], 0, 0)

    return [pl.BlockSpec((1, PAGE_SIZE * N_KV_HEADS, HEAD_DIM), functools.partial(index_map, j=j))
            for j in range(PAGES_PER_STEP)]


def _head_pages(page_refs, h):
    return jnp.concatenate([r[0, pl.ds(h, PAGE_SIZE, stride=N_KV_HEADS), :] for r in page_refs], axis=0)


def _flat_pool(cache):
    return cache.reshape(cache.shape[0] * cache.shape[1], PAGE_SIZE * N_KV_HEADS, HEAD_DIM)


def _sb_sample_kernel(pt_ref, proj_ref, *refs, tq):
    del pt_ref
    k_refs, v_refs = refs[:PAGES_PER_STEP], refs[PAGES_PER_STEP:2 * PAGES_PER_STEP]
    o_ref, carry_scr, acc_scr = refs[2 * PAGES_PER_STEP:]
    p = pl.program_id(1)
    rows = GROUP * tq
    later_mat = _later_matrix(PAGE_SIZE)

    def q_rows(h):
        q = proj_ref[:, h * GROUP * HEAD_DIM:(h + 1) * GROUP * HEAD_DIM]
        return (_stack_heads(q, GROUP) * SCALE).astype(BF16)

    @pl.when(p == 0)
    def _():
        t_loc = lax.broadcasted_iota(jnp.int32, (rows, PAGE_SIZE), 0) & (tq - 1)
        s_loc = lax.broadcasted_iota(jnp.int32, (rows, PAGE_SIZE), 1)
        for h in range(N_KV_HEADS):
            kc = ATTN_WIDTH + h * HEAD_DIM
            vc = ATTN_WIDTH + KV_WIDTH + h * HEAD_DIM
            k_new = _pad_rows(proj_ref[:, kc:kc + HEAD_DIM], PAGE_SIZE).astype(BF16)
            v_new = _pad_rows(proj_ref[:, vc:vc + HEAD_DIM], PAGE_SIZE).astype(BF16)
            c, a = _sb_block(q_rows(h), k_new, v_new, jnp.zeros((rows, 1), F32),
                             jnp.zeros((rows, HEAD_DIM), F32), later_mat, s_loc < t_loc)
            carry_scr[h] = c
            acc_scr[h] = a

    for h in range(N_KV_HEADS):
        c, a = _sb_block(q_rows(h), _head_pages(k_refs, h).astype(BF16), _head_pages(v_refs, h).astype(BF16),
                         carry_scr[h], acc_scr[h], later_mat)
        carry_scr[h] = c
        acc_scr[h] = a

    @pl.when(p == pl.num_programs(1) - 1)
    def _():
        gc = ATTN_WIDTH + 2 * KV_WIDTH
        for h in range(N_KV_HEADS):
            sl = slice(h * GROUP * HEAD_DIM, (h + 1) * GROUP * HEAD_DIM)
            gate = proj_ref[:, gc + sl.start:gc + sl.stop]
            o_ref[:, sl] = _unstack_heads(acc_scr[h], GROUP) * _silu(gate)


def sb_attention_sample(proj, k_pool, v_pool, page_offset, page_table, tq):
    batch, n_pages = page_table.shape
    assert n_pages % PAGES_PER_STEP == 0
    rows = GROUP * tq
    specs = _page_specs(n_pages, page_offset, latest_first=True)
    return pl.pallas_call(
        functools.partial(_sb_sample_kernel, tq=tq),
        grid_spec=pltpu.PrefetchScalarGridSpec(
            num_scalar_prefetch=1,
            grid=(batch, n_pages // PAGES_PER_STEP),
            in_specs=[pl.BlockSpec((tq, ATTN_IN), lambda b, p, pt: (b, 0))] + specs + specs,
            out_specs=pl.BlockSpec((tq, ATTN_WIDTH), lambda b, p, pt: (b, 0)),
            scratch_shapes=[pltpu.VMEM((N_KV_HEADS, rows, 1), F32),
                            pltpu.VMEM((N_KV_HEADS, rows, HEAD_DIM), F32)]),
        out_shape=jax.ShapeDtypeStruct((batch * tq, ATTN_WIDTH), F32),
        compiler_params=_cparams("parallel", "arbitrary"),
        name="sb_attention_sample",
    )(page_table, proj, *([k_pool] * PAGES_PER_STEP), *([v_pool] * PAGES_PER_STEP))


def _gdn_prep_kernel(u_ref, c0_ref, w_ref, o_ref, carry_scr, *, tt, tc):
    c = pl.program_id(1)
    t = pl.program_id(2)

    @pl.when(t == 0)
    def _():
        carry_scr[...] = c0_ref[0]

    u = u_ref[...]
    prev = carry_scr[...]
    w = w_ref[...]
    row = lax.broadcasted_iota(jnp.int32, (SUBLANES, tc), 0)
    y = u * w[GDN_CONV - 1:GDN_CONV, :]
    for i in range(1, GDN_CONV):
        ru = pltpu.roll(u, i, 0)
        top = jnp.where(row < i, pltpu.roll(prev, i, 0), ru[:SUBLANES])
        shifted = top if tt == SUBLANES else jnp.concatenate([top, ru[SUBLANES:]], axis=0)
        y = y + shifted * w[GDN_CONV - 1 - i:GDN_CONV - i, :]
    carry_scr[...] = u[tt - SUBLANES:, :]
    a = _silu(y)

    n_q_tiles = GDN_KEY_WIDTH // tc

    @pl.when(c < 2 * n_q_tiles)
    def _():
        scale = jnp.where(c < n_q_tiles, GDN_HEAD_DIM ** -0.5, 1.0)
        for s in range(tc // GDN_HEAD_DIM):
            sl = slice(s * GDN_HEAD_DIM, (s + 1) * GDN_HEAD_DIM)
            x = a[:, sl]
            o_ref[:, sl] = x * lax.rsqrt(jnp.sum(x * x, axis=-1, keepdims=True) + EPS) * scale

    @pl.when(c >= 2 * n_q_tiles)
    def _():
        o_ref[...] = a


def gdn_prep(proj, conv_state, conv_w, batch, seq, tc=512):
    tt = min(seq, 256)
    nt = seq // tt
    c0 = jnp.pad(conv_state, ((0, 0), (SUBLANES - (GDN_CONV - 1), 0), (0, 0)))
    return pl.pallas_call(
        functools.partial(_gdn_prep_kernel, tt=tt, tc=tc),
        grid=(batch, GDN_CONV_CH // tc, nt),
        in_specs=[pl.BlockSpec((tt, tc), lambda b, c, t: (b * nt + t, c)),
                  pl.BlockSpec((1, SUBLANES, tc), lambda b, c, t: (b, 0, c)),
                  pl.BlockSpec((GDN_CONV, tc), lambda b, c, t: (0, c))],
        out_specs=pl.BlockSpec((tt, tc), lambda b, c, t: (b * nt + t, c)),
        out_shape=jax.ShapeDtypeStruct((batch * seq, GDN_CONV_CH), F32),
        scratch_shapes=[pltpu.VMEM((SUBLANES, tc), F32)],
        compiler_params=_cparams("parallel", "parallel", "arbitrary"),
        name="gdn_prep",
    )(proj, c0, conv_w)


GDN_ROWS = 128
GDN_REP = GDN_V_HEADS // GDN_K_HEADS


def _gdn_delta_kernel(q_ref, k_ref, v_ref, bd_ref, z_ref, alog_ref, dt_ref, onorm_ref, s0_ref,
                      o_ref, s_ref, u_scr, w_scr, in_scr, qd_scr, kdt_scr, gl_scr, gt_scr, *, seq):
    hk = pl.program_id(1)
    c = GDN_ROWS
    n_chunks = max(seq // c, 1)
    unroll = gt_scr.shape[0]
    n_doublings = max(math.ceil(math.log2(min(seq, c))) - 1, 0)
    ri = lax.broadcasted_iota(jnp.int32, (c, c), 0)
    ci = lax.broadcasted_iota(jnp.int32, (c, c), 1)
    lane = lax.broadcasted_iota(jnp.int32, (c, LANES), 1)
    row = lax.broadcasted_iota(jnp.int32, (c, 1), 0)
    tril = ri >= ci
    tril_f = jnp.where(tril, 1.0, 0.0)
    neg_a = -jnp.exp(alog_ref[...])

    def chunk_rows(i):
        return pl.ds(0, c) if n_chunks == 1 else pl.ds(pl.multiple_of(i * c, c), c)

    def rows(ref, i):
        if seq < c:
            return _pad_rows(ref[...], c)
        return ref[chunk_rows(i), :]

    def prepare(i, _):
        ms, rhss = [], []
        for j in range(unroll):
            ch = i * unroll + j
            sl = chunk_rows(ch)
            q, k, v2, raw = rows(q_ref, ch), rows(k_ref, ch), rows(v_ref, ch), rows(bd_ref, ch)
            k16 = k.astype(BF16)
            kk = _dot_nt(k16, k16)
            qk = _dot_nt(q.astype(BF16), k16)
            sig = jax.nn.sigmoid(raw)
            g_all = jnp.where(row < seq, neg_a * _softplus(raw + dt_ref[...]), 0.0)
            gcx = _dot(tril_f, g_all, HI)
            gt_scr[j] = gcx.T
            for e in range(GDN_REP):
                hv = hk * GDN_REP + e
                beta = jnp.sum(jnp.where(lane == hv, sig, 0.0), axis=1, keepdims=True)
                gc = jnp.sum(jnp.where(lane == hv + GDN_V_HEADS, gcx, 0.0), axis=1, keepdims=True)
                g_row = gt_scr[j, pl.ds(hv + GDN_V_HEADS, 1), :]
                decay = jnp.exp(jnp.where(tril, gc - g_row, NEG))
                ms.append(-jnp.where(ri > ci, kk * beta * decay, 0.0))
                egc = jnp.exp(gc)
                v = v2[:, e * GDN_HEAD_DIM:(e + 1) * GDN_HEAD_DIM]
                rhss.append(jnp.concatenate([v * beta, k * (beta * egc)], axis=1))
                g_last = gc[c - 1:c, :]
                in_scr[e, sl, :] = (qk * decay).astype(BF16)
                qd_scr[e, sl, :] = (q * egc).astype(BF16)
                kdt_scr[e, ch] = (k * jnp.exp(g_last - gc)).T.astype(BF16)
                gl_scr[e, ch] = jnp.broadcast_to(jnp.exp(g_last), (SUBLANES, LANES))
        nmat = jnp.stack(ms)
        pw = _split_bf16(nmat)
        for _ in range(n_doublings):
            sq = _bdot_split(pw, pw)
            pw = _split_bf16(sq)
            nmat = nmat + sq + _bdot_split(_split_bf16(nmat), pw)
        rhs = jnp.stack(rhss)
        sol = rhs + _bdot(nmat.astype(BF16), rhs.astype(BF16))
        for j in range(unroll):
            sl = chunk_rows(i * unroll + j)
            for e in range(GDN_REP):
                x = sol[j * GDN_REP + e]
                u_scr[e, sl, :] = x[:, :GDN_HEAD_DIM]
                w_scr[e, sl, :] = x[:, GDN_HEAD_DIM:].astype(BF16)
        return 0

    def advance(i, states):
        sl = chunk_rows(i)
        z2 = rows(z_ref, i)
        out = []
        for e in range(GDN_REP):
            s = states[e]
            s16 = s.astype(BF16)
            v_new = u_scr[e, sl, :] - _dot(w_scr[e, sl, :], s16)
            v16 = v_new.astype(BF16)
            o = _dot(qd_scr[e, sl, :], s16) + _dot(in_scr[e, sl, :], v16)
            out.append(s * gl_scr[e, i][0:1, :] + _dot(kdt_scr[e, i], v16))
            ms = jnp.mean(o * o, axis=-1, keepdims=True)
            z = z2[:, e * GDN_HEAD_DIM:(e + 1) * GDN_HEAD_DIM]
            o = (o * lax.rsqrt(ms + EPS) * onorm_ref[...] * _silu(z)).astype(o_ref.dtype)
            cols = slice(e * GDN_HEAD_DIM, (e + 1) * GDN_HEAD_DIM)
            if seq < c:
                o_ref[:, cols] = o[:seq]
            else:
                o_ref[sl, cols] = o
        return tuple(out)

    states = tuple(s0_ref[0, e] for e in range(GDN_REP))
    if n_chunks == 1:
        prepare(0, 0)
        states = advance(0, states)
    else:
        lax.fori_loop(0, n_chunks // unroll, prepare, 0)
        states = lax.fori_loop(0, n_chunks, advance, states)
    for e in range(GDN_REP):
        s_ref[0, e] = states[e]


def gdn_delta(act, proj, a_log, dt_bias, o_norm, s0, batch, seq, out_dtype):
    assert seq % GDN_ROWS == 0 or seq < GDN_ROWS
    pad = jnp.zeros((GDN_V_HEADS,), F32)
    tail = jnp.zeros((LANES - 2 * GDN_V_HEADS,), F32)
    alog = jnp.concatenate([pad, a_log, tail]).reshape(1, LANES)
    dt = jnp.concatenate([pad, dt_bias, tail]).reshape(1, LANES)
    blk = (seq, GDN_HEAD_DIM)
    wide = (seq, GDN_REP * GDN_HEAD_DIM)
    state_blk = (1, GDN_REP, GDN_HEAD_DIM, GDN_HEAD_DIM)
    rows = max(seq, GDN_ROWS)
    n_chunks = rows // GDN_ROWS
    return pl.pallas_call(
        functools.partial(_gdn_delta_kernel, seq=seq),
        grid=(batch, GDN_K_HEADS),
        in_specs=[pl.BlockSpec(blk, lambda b, h: (b, h)),
                  pl.BlockSpec(blk, lambda b, h: (b, GDN_K_HEADS + h)),
                  pl.BlockSpec(wide, lambda b, h: (b, 2 * GDN_K_HEADS // GDN_REP + h)),
                  pl.BlockSpec((seq, LANES), lambda b, h: (b, GDN_BD_COL)),
                  pl.BlockSpec(wide, lambda b, h: (b, GDN_Z_COL // GDN_REP + h)),
                  pl.BlockSpec((1, LANES), lambda b, h: (0, 0)),
                  pl.BlockSpec((1, LANES), lambda b, h: (0, 0)),
                  pl.BlockSpec((1, GDN_HEAD_DIM), lambda b, h: (0, 0)),
                  pl.BlockSpec(state_blk, lambda b, h: (b, h, 0, 0))],
        out_specs=[pl.BlockSpec(wide, lambda b, h: (b, h)),
                   pl.BlockSpec(state_blk, lambda b, h: (b, h, 0, 0))],
        scratch_shapes=[pltpu.VMEM((GDN_REP, rows, GDN_HEAD_DIM), F32),
                        pltpu.VMEM((GDN_REP, rows, GDN_HEAD_DIM), BF16),
                        pltpu.VMEM((GDN_REP, rows, GDN_ROWS), BF16),
                        pltpu.VMEM((GDN_REP, rows, GDN_HEAD_DIM), BF16),
                        pltpu.VMEM((GDN_REP, n_chunks, GDN_HEAD_DIM, GDN_ROWS), BF16),
                        pltpu.VMEM((GDN_REP, n_chunks, SUBLANES, LANES), F32),
                        pltpu.VMEM((2 if n_chunks % 2 == 0 else 1, GDN_ROWS, LANES), F32)],
        out_shape=[jax.ShapeDtypeStruct((batch * seq, GDN_VAL_WIDTH), out_dtype),
                   jax.ShapeDtypeStruct((batch, GDN_V_HEADS, GDN_HEAD_DIM, GDN_HEAD_DIM), F32)],
        compiler_params=_cparams("parallel", "parallel"),
        name="gdn_delta",
    )(act, act, act, proj, proj, alog, dt, o_norm.reshape(1, GDN_HEAD_DIM), s0)


def _rel_bucket(dist):
    max_exact = REL_BUCKETS // 2
    n = jnp.maximum(dist, 0)
    large = max_exact + (jnp.log(jnp.maximum(n, 1).astype(F32) / max_exact)
                         / math.log(REL_MAX_DIST / max_exact) * (REL_BUCKETS - max_exact)).astype(jnp.int32)
    large = jnp.minimum(large, REL_BUCKETS - 1)
    return jnp.where(n < max_exact, n, large)


def _bias_table_kernel(rbt_ref, onehot_ref, o_ref):
    o_ref[...] = _dot(rbt_ref[...], onehot_ref[...], HI)


def bias_by_distance(rel_bias, dist):
    bucket = _rel_bucket(dist)
    onehot = (bucket[None, :] == jnp.arange(REL_BUCKETS)[:, None]) & (dist[None, :] >= 0)
    return pl.pallas_call(
        _bias_table_kernel,
        out_shape=jax.ShapeDtypeStruct((N_HEADS, dist.shape[0]), F32),
        compiler_params=pltpu.CompilerParams(vmem_limit_bytes=VMEM_LIMIT),
        name="bias_table",
    )(rel_bias.T, onehot.astype(F32))


def _toeplitz(window_row, rows, shift):
    x = jnp.broadcast_to(window_row, (rows, window_row.shape[1]))
    return pltpu.roll(x, shift, 1, stride=1, stride_axis=0)


def _select_topk(gate, n_valid, n_blocks):
    lane = lax.broadcasted_iota(jnp.int32, gate.shape, 1)
    valid = lane < n_valid
    gm = jnp.where(valid, gate, -jnp.inf)
    cnt = jnp.zeros(gate.shape, jnp.int32)
    for m in range(n_blocks):
        col = gm[:, m:m + 1]
        beats = (col > gm) | ((col == gm) & (lane > m))
        cnt = cnt + jnp.where(beats, 1, 0)
    return jnp.where(valid & (cnt < MOBA_TOPK), 1.0, 0.0)


def _select_topk_t(gate_t, n_valid):
    blk = lax.broadcasted_iota(jnp.int32, gate_t.shape, 0)
    valid = blk < n_valid
    gm = jnp.where(valid, gate_t, -jnp.inf)
    cnt = jnp.zeros(gate_t.shape, jnp.int32)
    for m in range(gate_t.shape[0]):
        row = gm[m:m + 1, :]
        beats = (row > gm) | ((row == gm) & (blk > m))
        cnt = cnt + jnp.where(beats, 1, 0)
    return jnp.where(valid & (cnt < MOBA_TOPK), 1.0, 0.0)


def _moba_tile(qs, k, v, bias, mask, m, l, acc):
    s = _dot_nt(qs, k) * SCALE + bias
    s = jnp.where(mask, s, NEG)
    m_new = jnp.maximum(m, jnp.max(s, axis=1, keepdims=True))
    p = jnp.where(mask, jnp.exp(s - m_new), 0.0)
    alpha = jnp.exp(m - m_new)
    l = alpha * l + jnp.sum(p, axis=1, keepdims=True)
    acc = alpha * acc + _dot(p.astype(BF16), v)
    return m_new, l, acc


def _moba_prompt_kernel(q_ref, k_ref, v_ref, gate_ref, tb_ref, o_ref,
                        means_scr, bias_scr, pen_scr, m_scr, l_scr, acc_scr, *, tq, n_blocks):
    h = pl.program_id(0)
    b = pl.program_id(1)
    qi = pl.program_id(2)
    rows = GROUP * tq

    @pl.when(qi == 0)
    def _():
        means_scr[...] = jnp.zeros_like(means_scr)
        means_scr[0:n_blocks, :] = jnp.mean(k_ref[...].reshape(n_blocks, MOBA_BLOCK, HEAD_DIM), axis=1)

    @pl.when(b == 0)
    def _():
        for g in range(GROUP):
            wrow = tb_ref[pl.ds((h * GROUP + g) * n_blocks + qi, 1), :]
            bias_scr[qi, pl.ds(g * tq, tq), :] = _toeplitz(wrow, tq, tq + 1)[:, :tq]

    qs32 = _stack_heads(q_ref[...], GROUP)
    qs = (qs32 * SCALE).astype(BF16)
    sel_t = _select_topk_t(_dot_nt(means_scr[...], qs32, HI), qi)
    pen_t = (sel_t - 1.0) * -NEG
    pen_scr[...] = jnp.concatenate([pen_t, jnp.zeros((LANES - pen_t.shape[0], rows), F32)], axis=0).T
    lane = lax.broadcasted_iota(jnp.int32, (rows, LANES), 1)
    t_loc = lax.broadcasted_iota(jnp.int32, (rows, tq), 0) & (tq - 1)
    s_loc = lax.broadcasted_iota(jnp.int32, (rows, tq), 1)

    def load(ref, blk):
        return ref[pl.ds(pl.multiple_of(blk * tq, tq), tq), :].astype(BF16)

    def update(s, v, m, l, acc):
        m_new = jnp.maximum(m, jnp.max(s, axis=1, keepdims=True))
        p = jnp.exp(s - m_new)
        alpha = jnp.exp(m - m_new)
        m_scr[...] = m_new
        l_scr[...] = alpha * l + jnp.sum(p, axis=1, keepdims=True)
        acc_scr[...] = alpha * acc + _dot(p.astype(BF16), v)

    s = jnp.where(s_loc <= t_loc, _dot_nt(qs, load(k_ref, qi)) + bias_scr[0], NEG)
    update(s, load(v_ref, qi), jnp.full((rows, 1), NEG, F32), jnp.zeros((rows, 1), F32),
           jnp.zeros((rows, HEAD_DIM), F32))

    def body(kb, _):
        pen = jnp.sum(jnp.where(lane == kb, pen_scr[...], 0.0), axis=1, keepdims=True)
        update(_dot_nt(qs, load(k_ref, kb)) + bias_scr[qi - kb] + pen, load(v_ref, kb),
               m_scr[...], l_scr[...], acc_scr[...])
        return 0

    lax.fori_loop(0, qi, body, 0)
    o = _unstack_heads(acc_scr[...] / l_scr[...], GROUP)
    o_ref[...] = (o * _silu(gate_ref[...])).astype(o_ref.dtype)


def moba_attention_prompt(proj, rel_bias, batch, seq):
    tq = MOBA_BLOCK
    assert seq % tq == 0
    nb = seq // tq
    gw = GROUP * HEAD_DIM
    rows = GROUP * tq
    dist = (jnp.arange(nb)[:, None] * tq + (tq - 1) - jnp.arange(2 * tq)[None, :]).reshape(-1)
    table = bias_by_distance(rel_bias, dist).reshape(N_HEADS * nb, 2 * tq)
    return pl.pallas_call(
        functools.partial(_moba_prompt_kernel, tq=tq, n_blocks=nb),
        grid=(N_KV_HEADS, batch, nb),
        in_specs=[pl.BlockSpec((tq, gw), lambda h, b, i: (b * nb + i, h)),
                  pl.BlockSpec((seq, HEAD_DIM), lambda h, b, i: (b, K_COL + h)),
                  pl.BlockSpec((seq, HEAD_DIM), lambda h, b, i: (b, V_COL + h)),
                  pl.BlockSpec((tq, gw), lambda h, b, i: (b * nb + i, GATE_COL + h)),
                  pl.BlockSpec((N_HEADS * nb, 2 * tq), lambda h, b, i: (0, 0))],
        out_specs=pl.BlockSpec((tq, gw), lambda h, b, i: (b * nb + i, h)),
        out_shape=jax.ShapeDtypeStruct((batch * seq, ATTN_WIDTH), BF16),
        scratch_shapes=[pltpu.VMEM((-(-nb // SUBLANES) * SUBLANES, HEAD_DIM), F32),
                        pltpu.VMEM((nb, rows, tq), F32),
                        pltpu.VMEM((rows, LANES), F32),
                        pltpu.VMEM((rows, 1), F32),
                        pltpu.VMEM((rows, 1), F32),
                        pltpu.VMEM((rows, HEAD_DIM), F32)],
        compiler_params=_cparams("arbitrary", "arbitrary", "arbitrary"),
        name="moba_attention_prompt",
    )(proj, proj, proj, proj, table)


PAGES_PER_BLOCK = MOBA_BLOCK // PAGE_SIZE
BLOCKS_PER_STEP = PAGES_PER_STEP // PAGES_PER_BLOCK


def _moba_means_kernel(pt_ref, *refs):
    del pt_ref
    page_refs, o_ref = refs[:-1], refs[-1]
    for blk in range(BLOCKS_PER_STEP):
        pages = page_refs[blk * PAGES_PER_BLOCK:(blk + 1) * PAGES_PER_BLOCK]
        for h in range(N_KV_HEADS):
            total = jnp.sum(_head_pages(pages, h), axis=0, keepdims=True)
            o_ref[0, blk, :, h * HEAD_DIM:(h + 1) * HEAD_DIM] = total * (1.0 / MOBA_BLOCK)


def moba_block_means(k_pool, page_offset, page_table):
    batch, n_pages = page_table.shape
    assert n_pages % PAGES_PER_STEP == 0
    nb = n_pages // PAGES_PER_BLOCK
    return pl.pallas_call(
        _moba_means_kernel,
        grid_spec=pltpu.PrefetchScalarGridSpec(
            num_scalar_prefetch=1,
            grid=(batch, n_pages // PAGES_PER_STEP),
            in_specs=_page_specs(n_pages, page_offset, latest_first=False),
            out_specs=pl.BlockSpec((1, BLOCKS_PER_STEP, 1, KV_WIDTH), lambda b, n, pt: (b, n, 0, 0))),
        out_shape=jax.ShapeDtypeStruct((batch, nb, 1, KV_WIDTH), F32),
        compiler_params=_cparams("parallel", "parallel"),
        name="moba_block_means",
    )(page_table, *([k_pool] * PAGES_PER_STEP))


def _moba_sample_kernel(pt_ref, proj_ref, means_ref, tb_ref, *refs, tq, n_pages):
    del pt_ref
    k_refs, v_refs = refs[:PAGES_PER_STEP], refs[PAGES_PER_STEP:2 * PAGES_PER_STEP]
    o_ref, sel_scr, m_scr, l_scr, acc_scr = refs[2 * PAGES_PER_STEP:]
    p = pl.program_id(1)
    rows = GROUP * tq
    n_past_blocks = n_pages // PAGES_PER_BLOCK
    lane = lax.broadcasted_iota(jnp.int32, (rows, LANES), 1)

    def q_rows(h):
        return _stack_heads(proj_ref[:, h * GROUP * HEAD_DIM:(h + 1) * GROUP * HEAD_DIM], GROUP)

    def bias_rows(h, page):
        tiles = []
        for g in range(GROUP):
            wrow = tb_ref[pl.ds((h * GROUP + g) * (n_pages + 1) + page, 1), :]
            tiles.append(_toeplitz(wrow, tq, PAGE_SIZE + 1)[:, :PAGE_SIZE])
        return jnp.concatenate(tiles, axis=0)

    @pl.when(p == 0)
    def _():
        t_loc = lax.broadcasted_iota(jnp.int32, (rows, PAGE_SIZE), 0) & (tq - 1)
        s_loc = lax.broadcasted_iota(jnp.int32, (rows, PAGE_SIZE), 1)
        for h in range(N_KV_HEADS):
            sl = slice(h * HEAD_DIM, (h + 1) * HEAD_DIM)
            qs32 = q_rows(h)
            means = _pad_rows(means_ref[0, :, sl], LANES)
            sel_scr[h] = _select_topk(_dot_nt(qs32, means, HI), n_past_blocks, n_past_blocks)
            kc = ATTN_WIDTH + h * HEAD_DIM
            vc = ATTN_WIDTH + KV_WIDTH + h * HEAD_DIM
            k_new = _pad_rows(proj_ref[:, kc:kc + HEAD_DIM], PAGE_SIZE).astype(BF16)
            v_new = _pad_rows(proj_ref[:, vc:vc + HEAD_DIM], PAGE_SIZE).astype(BF16)
            m, l, acc = _moba_tile(qs32.astype(BF16), k_new, v_new, bias_rows(h, n_pages), s_loc <= t_loc,
                                   jnp.full((rows, 1), NEG, F32), jnp.zeros((rows, 1), F32),
                                   jnp.zeros((rows, HEAD_DIM), F32))
            m_scr[h] = m
            l_scr[h] = l
            acc_scr[h] = acc

    for h in range(N_KV_HEADS):
        sel = sel_scr[h]
        mask = []
        for j in range(BLOCKS_PER_STEP):
            col = jnp.sum(jnp.where(lane == p * BLOCKS_PER_STEP + j, sel, 0.0), axis=1, keepdims=True) > 0.5
            mask.append(jnp.broadcast_to(col, (rows, MOBA_BLOCK)))
        bias = jnp.concatenate([bias_rows(h, p * PAGES_PER_STEP + j) for j in range(PAGES_PER_STEP)], axis=1)
        m, l, acc = _moba_tile(q_rows(h).astype(BF16), _head_pages(k_refs, h).astype(BF16),
                               _head_pages(v_refs, h).astype(BF16), bias, jnp.concatenate(mask, axis=1),
                               m_scr[h], l_scr[h], acc_scr[h])
        m_scr[h] = m
        l_scr[h] = l
        acc_scr[h] = acc

    @pl.when(p == pl.num_programs(1) - 1)
    def _():
        gc = ATTN_WIDTH + 2 * KV_WIDTH
        for h in range(N_KV_HEADS):
            sl = slice(h * GROUP * HEAD_DIM, (h + 1) * GROUP * HEAD_DIM)
            gate = proj_ref[:, gc + sl.start:gc + sl.stop]
            o_ref[:, sl] = _unstack_heads(acc_scr[h] / l_scr[h], GROUP) * _silu(gate)


def moba_attention_sample(proj, k_pool, v_pool, page_offset, page_table, rel_bias, tq):
    batch, n_pages = page_table.shape
    assert n_pages % PAGES_PER_STEP == 0 and tq <= MOBA_BLOCK
    rows = GROUP * tq
    nb = n_pages // PAGES_PER_BLOCK
    assert nb <= LANES
    means = moba_block_means(k_pool, page_offset, page_table).reshape(batch, nb, KV_WIDTH)
    dist = ((n_pages - jnp.arange(n_pages + 1))[:, None] * PAGE_SIZE + (PAGE_SIZE - 1)
            - jnp.arange(2 * PAGE_SIZE)[None, :]).reshape(-1)
    table = bias_by_distance(rel_bias, dist).reshape(N_HEADS * (n_pages + 1), 2 * PAGE_SIZE)
    specs = _page_specs(n_pages, page_offset, latest_first=False)
    return pl.pallas_call(
        functools.partial(_moba_sample_kernel, tq=tq, n_pages=n_pages),
        grid_spec=pltpu.PrefetchScalarGridSpec(
            num_scalar_prefetch=1,
            grid=(batch, n_pages // PAGES_PER_STEP),
            in_specs=[pl.BlockSpec((tq, ATTN_IN), lambda b, p, pt: (b, 0)),
                      pl.BlockSpec((1, nb, KV_WIDTH), lambda b, p, pt: (b, 0, 0)),
                      pl.BlockSpec((N_HEADS * (n_pages + 1), 2 * PAGE_SIZE), lambda b, p, pt: (0, 0))]
            + specs + specs,
            out_specs=pl.BlockSpec((tq, ATTN_WIDTH), lambda b, p, pt: (b, 0)),
            scratch_shapes=[pltpu.VMEM((N_KV_HEADS, rows, LANES), F32),
                            pltpu.VMEM((N_KV_HEADS, rows, 1), F32),
                            pltpu.VMEM((N_KV_HEADS, rows, 1), F32),
                            pltpu.VMEM((N_KV_HEADS, rows, HEAD_DIM), F32)]),
        out_shape=jax.ShapeDtypeStruct((batch * tq, ATTN_WIDTH), F32),
        compiler_params=_cparams("parallel", "arbitrary"),
        name="moba_attention_sample",
    )(page_table, proj, means, table, *([k_pool] * PAGES_PER_STEP), *([v_pool] * PAGES_PER_STEP))


N_MIXERS = 3
GDN_IN_PADDED = -(-GDN_IN // 512) * 512


def _new_kv(proj, batch, seq):
    k = proj[:, ATTN_WIDTH:ATTN_WIDTH + KV_WIDTH].reshape(batch, seq, N_KV_HEADS, HEAD_DIM)
    v = proj[:, ATTN_WIDTH + KV_WIDTH:ATTN_WIDTH + 2 * KV_WIDTH].reshape(batch, seq, N_KV_HEADS, HEAD_DIM)
    return k, v


def kernel(x_prompt, x_sample, cache_sb_k, cache_sb_v, state_gdn_conv, state_gdn_rec, cache_moba_k, cache_moba_v, page_table, norm_g, sb_w_in, sb_w_out, gdn_w_in, gdn_conv_w, gdn_a_log, gdn_dt_bias, gdn_o_norm, gdn_w_out, moba_w_in, moba_q_norm, moba_k_norm, moba_w_out, rel_bias):
    bp, tp, d = x_prompt.shape
    bs, ts, _ = x_sample.shape
    n_phys = cache_sb_k.shape[1]
    yp = x_prompt.reshape(bp * tp, d)
    ys = x_sample.reshape(bs * ts, d)
    outs = {name: [] for name in ("sb_kp", "sb_vp", "sb_ks", "sb_vs", "gdn_cp", "gdn_sp", "gdn_cs", "gdn_ss",
                                  "mb_kp", "mb_vp", "mb_ks", "mb_vs")}
    for layer in range(norm_g.shape[0]):
        kind = layer % N_MIXERS
        j = layer // N_MIXERS
        g = norm_g[layer]
        if kind == 0:
            w_in = sb_w_in[j].astype(BF16)
            w_out = sb_w_out[j].astype(BF16)
            pp = norm_matmul(yp, g, w_in)
            ps = norm_matmul(ys, g, w_in)
            op = sb_attention_prompt(pp, bp, tp)
            os_ = sb_attention_sample(ps, _flat_pool(cache_sb_k), _flat_pool(cache_sb_v), j * n_phys, page_table, ts)
            kp, vp = _new_kv(pp, bp, tp)
            ks, vs = _new_kv(ps, bs, ts)
            outs["sb_kp"].append(kp); outs["sb_vp"].append(vp); outs["sb_ks"].append(ks); outs["sb_vs"].append(vs)
        elif kind == 1:
            w_in = jnp.pad(gdn_w_in[j], ((0, 0), (0, GDN_IN_PADDED - GDN_IN))).astype(BF16)
            w_out = gdn_w_out[j].astype(BF16)
            pp = norm_matmul(yp, g, w_in)
            ps = norm_matmul(ys, g, w_in)
            act_p = gdn_prep(pp, jnp.zeros((bp, GDN_CONV - 1, GDN_CONV_CH), F32), gdn_conv_w[j], bp, tp)
            act_s = gdn_prep(ps, state_gdn_conv[j], gdn_conv_w[j], bs, ts)
            s0 = jnp.zeros((bp,) + state_gdn_rec.shape[2:], F32)
            op, sp = gdn_delta(act_p, pp, gdn_a_log[j], gdn_dt_bias[j], gdn_o_norm[j], s0, bp, tp, BF16)
            os_, ss = gdn_delta(act_s, ps, gdn_a_log[j], gdn_dt_bias[j], gdn_o_norm[j], state_gdn_rec[j],
                                bs, ts, F32)
            outs["gdn_cp"].append(pp.reshape(bp, tp, -1)[:, tp - (GDN_CONV - 1):, :GDN_CONV_CH])
            outs["gdn_cs"].append(ps.reshape(bs, ts, -1)[:, ts - (GDN_CONV - 1):, :GDN_CONV_CH])
            outs["gdn_sp"].append(sp); outs["gdn_ss"].append(ss)
        else:
            w_in = moba_w_in[j].astype(BF16)
            w_out = moba_w_out[j].astype(BF16)
            head_gain = jnp.concatenate([jnp.tile(moba_q_norm[j], N_HEADS), jnp.tile(moba_k_norm[j], N_KV_HEADS),
                                         jnp.ones((ATTN_IN - ATTN_WIDTH - KV_WIDTH,), F32)]).reshape(1, ATTN_IN)
            pp = norm_matmul(yp, g, w_in, head_gain, ATTN_WIDTH + KV_WIDTH)
            ps = norm_matmul(ys, g, w_in, head_gain, ATTN_WIDTH + KV_WIDTH)
            op = moba_attention_prompt(pp, rel_bias, bp, tp)
            os_ = moba_attention_sample(ps, _flat_pool(cache_moba_k), _flat_pool(cache_moba_v), j * n_phys,
                                        page_table, rel_bias, ts)
            kp, vp = _new_kv(pp, bp, tp)
            ks, vs = _new_kv(ps, bs, ts)
            outs["mb_kp"].append(kp); outs["mb_vp"].append(vp); outs["mb_ks"].append(ks); outs["mb_vs"].append(vs)
        yp = matmul_residual(op, w_out, yp)
        ys = matmul_residual(os_, w_out, ys)
    stack = lambda name: jnp.stack(outs[name])
    return (yp.reshape(bp, tp, d), ys.reshape(bs, ts, d),
            stack("sb_kp"), stack("sb_vp"), stack("sb_ks"), stack("sb_vs"),
            stack("gdn_cp"), stack("gdn_sp"), stack("gdn_cs"), stack("gdn_ss"),
            stack("mb_kp"), stack("mb_vp"), stack("mb_ks"), stack("mb_vs"))
```

```python
import functools
import math

import jax
import jax.numpy as jnp
from jax import lax
from jax.experimental import pallas as pl
from jax.experimental.pallas import tpu as pltpu

F32 = jnp.float32
BF16 = jnp.bfloat16
HI = lax.Precision.HIGHEST

LANES = 128
SUBLANES = 8
VMEM_LIMIT = 56 * 1024 * 1024

HEAD_DIM = 128
N_HEADS = 16
N_KV_HEADS = 4
GROUP = N_HEADS // N_KV_HEADS
ATTN_WIDTH = N_HEADS * HEAD_DIM
KV_WIDTH = N_KV_HEADS * HEAD_DIM
ATTN_IN = 2 * ATTN_WIDTH + 2 * KV_WIDTH
Q_COL = 0
K_COL = ATTN_WIDTH // HEAD_DIM
V_COL = K_COL + N_KV_HEADS
GATE_COL = (ATTN_WIDTH + 2 * KV_WIDTH) // (GROUP * HEAD_DIM)
PAGE_SIZE = 128

GDN_HEAD_DIM = 128
GDN_K_HEADS = 16
GDN_V_HEADS = 32
GDN_KEY_WIDTH = GDN_K_HEADS * GDN_HEAD_DIM
GDN_VAL_WIDTH = GDN_V_HEADS * GDN_HEAD_DIM
GDN_CONV_CH = 2 * GDN_KEY_WIDTH + GDN_VAL_WIDTH
GDN_CONV = 4
GDN_CHUNK = 64
GDN_IN = GDN_CONV_CH + GDN_VAL_WIDTH + 2 * GDN_V_HEADS
GDN_Z_COL = GDN_CONV_CH // GDN_HEAD_DIM
GDN_BD_COL = (GDN_CONV_CH + GDN_VAL_WIDTH) // LANES

MOBA_BLOCK = 256
MOBA_TOPK = 3
REL_BUCKETS = 32
REL_MAX_DIST = 4096

EPS = 1e-6
NEG = -1e30
SCALE = HEAD_DIM ** -0.5

_NT = (((1,), (1,)), ((), ()))


def _cparams(*sem):
    return pltpu.CompilerParams(dimension_semantics=sem, vmem_limit_bytes=VMEM_LIMIT)


def _softplus(z):
    return jnp.maximum(z, 0.0) + jnp.log1p(jnp.exp(-jnp.abs(z)))


def _silu(x):
    return x * jax.nn.sigmoid(x)


def _dot(a, b, precision=None):
    return jnp.dot(a, b, preferred_element_type=F32, precision=precision)


def _dot_nt(a, b, precision=None):
    return lax.dot_general(a, b, _NT, preferred_element_type=F32, precision=precision)


def _split_bf16(x):
    hi = x.astype(BF16)
    return hi, (x - hi.astype(F32)).astype(BF16)


def _bdot(a, b):
    return jnp.einsum("bij,bjk->bik", a, b, preferred_element_type=F32)


def _bdot_split(a, b):
    return _bdot(a[0], b[0]) + (_bdot(a[0], b[1]) + _bdot(a[1], b[0]))


def _norm_matmul_kernel(x_ref, g_ref, w_ref, hg_ref, o_ref, h_scr, *, n_norm_tiles, tn):
    j = pl.program_id(1)

    @pl.when(j == 0)
    def _():
        x = x_ref[...]
        ms = jnp.mean(x * x, axis=-1, keepdims=True)
        h_scr[...] = (x * lax.rsqrt(ms + EPS) * g_ref[...]).astype(BF16)

    acc = _dot(h_scr[...], w_ref[...])
    if n_norm_tiles == 0:
        o_ref[...] = acc
    else:
        @pl.when(j < n_norm_tiles)
        def _():
            for s in range(tn // HEAD_DIM):
                sl = slice(s * HEAD_DIM, (s + 1) * HEAD_DIM)
                a = acc[:, sl]
                ms = jnp.mean(a * a, axis=-1, keepdims=True)
                o_ref[:, sl] = a * lax.rsqrt(ms + EPS) * hg_ref[:, sl]

        @pl.when(j >= n_norm_tiles)
        def _():
            o_ref[...] = acc


def norm_matmul(x, g, w, head_gain=None, n_norm_cols=0, tn=512):
    m, d = x.shape
    n = w.shape[1]
    tm = min(m, 1024)
    assert m % tm == 0 and n % tn == 0 and n_norm_cols % tn == 0
    if head_gain is None:
        head_gain = jnp.ones((1, n), F32)
    return pl.pallas_call(
        functools.partial(_norm_matmul_kernel, n_norm_tiles=n_norm_cols // tn, tn=tn),
        grid=(m // tm, n // tn),
        in_specs=[pl.BlockSpec((tm, d), lambda i, j: (i, 0)),
                  pl.BlockSpec((1, d), lambda i, j: (0, 0)),
                  pl.BlockSpec((d, tn), lambda i, j: (0, j)),
                  pl.BlockSpec((1, tn), lambda i, j: (0, j))],
        out_specs=pl.BlockSpec((tm, tn), lambda i, j: (i, j)),
        out_shape=jax.ShapeDtypeStruct((m, n), F32),
        scratch_shapes=[pltpu.VMEM((tm, d), BF16)],
        compiler_params=_cparams("parallel", "arbitrary"),
        name="norm_matmul",
    )(x, g.reshape(1, d), w, head_gain)


def _matmul_residual_kernel(a_ref, w_ref, r_ref, o_ref):
    o_ref[...] = r_ref[...] + _dot(a_ref[...].astype(BF16), w_ref[...])


def matmul_residual(a, w, res, tn=512):
    m, k = a.shape
    n = w.shape[1]
    tm = min(m, 1024)
    assert m % tm == 0 and n % tn == 0
    return pl.pallas_call(
        _matmul_residual_kernel,
        grid=(m // tm, n // tn),
        in_specs=[pl.BlockSpec((tm, k), lambda i, j: (i, 0)),
                  pl.BlockSpec((k, tn), lambda i, j: (0, j)),
                  pl.BlockSpec((tm, tn), lambda i, j: (i, j))],
        out_specs=pl.BlockSpec((tm, tn), lambda i, j: (i, j)),
        out_shape=jax.ShapeDtypeStruct((m, n), F32),
        compiler_params=_cparams("parallel", "parallel"),
        name="matmul_residual",
    )(a, w, res)


def _stack_heads(x, n):
    return jnp.concatenate([x[:, g * HEAD_DIM:(g + 1) * HEAD_DIM] for g in range(n)], axis=0)


def _unstack_heads(x, n):
    t = x.shape[0] // n
    return jnp.concatenate([x[g * t:(g + 1) * t] for g in range(n)], axis=1)


def _later_matrix(tk):
    r = lax.broadcasted_iota(jnp.int32, (2 * tk, tk), 0) & (tk - 1)
    c = lax.broadcasted_iota(jnp.int32, (2 * tk, tk), 1)
    return jnp.where(r > c, -1.0, 0.0).astype(BF16)


def _sb_block(qs, k, v, carry, acc, later_mat, mask=None):
    tk = later_mat.shape[1]
    n = k.shape[-2] // tk
    batched = qs.ndim == 3
    z = jnp.einsum("hrd,hsd->hrs", qs, k, preferred_element_type=F32) if batched else _dot_nt(qs, k)
    if mask is not None:
        z = jnp.where(mask, z, NEG)
    sp = jnp.maximum(z, 0.0) + jnp.log(1.0 + jnp.exp(-jnp.abs(z)))
    hi = sp.astype(BF16)
    lo = (sp - hi.astype(F32)).astype(BF16)
    later, total = [], []
    for j in range(n):
        sl = slice(j * tk, (j + 1) * tk)
        split = jnp.concatenate([hi[..., sl], lo[..., sl]], axis=-1)
        lt = _dot(split.reshape(-1, 2 * tk), later_mat).reshape(z.shape[:-1] + (tk,))
        later.append(lt)
        total.append(lt[..., :1] - sp[..., j * tk:j * tk + 1])
    for j in reversed(range(n)):
        later[j] = later[j] + carry
        carry = carry + total[j]
    later = later[0] if n == 1 else jnp.concatenate(later, axis=-1)
    w = jnp.exp(z - sp + later).astype(BF16)
    pv = jnp.einsum("hrs,hsd->hrd", w, v, preferred_element_type=F32) if batched else _dot(w, v)
    return carry, acc + pv


SB_BLOCKS_PER_STEP = 4


def _sb_prompt_kernel(q_ref, k_ref, v_ref, gate_ref, o_ref, carry_scr, acc_scr, *, tq):
    qi = pl.program_id(2)
    rows = GROUP * tq
    nk = SB_BLOCKS_PER_STEP
    qs = (_stack_heads(q_ref[...], GROUP) * SCALE).astype(BF16)
    later_mat = _later_matrix(tq)
    q_pos = qi * tq + (lax.broadcasted_iota(jnp.int32, (rows, 1), 0) & (tq - 1))
    s_off = lax.broadcasted_iota(jnp.int32, (rows, nk * tq), 1)
    carry_scr[...] = jnp.zeros_like(carry_scr)
    acc_scr[...] = jnp.zeros_like(acc_scr)

    def body(i, _):
        top = qi + 1 - nk * i
        start = pl.multiple_of(jnp.maximum(top - nk, 0) * tq, tq)
        k = k_ref[pl.ds(start, nk * tq), :].astype(BF16)
        v = v_ref[pl.ds(start, nk * tq), :].astype(BF16)
        mask = s_off + start < jnp.minimum(top * tq, q_pos)
        c, a = _sb_block(qs, k, v, carry_scr[...], acc_scr[...], later_mat, mask)
        carry_scr[...] = c
        acc_scr[...] = a
        return 0

    lax.fori_loop(0, (qi + nk) // nk, body, 0)
    o = _unstack_heads(acc_scr[...], GROUP)
    o_ref[...] = (o * _silu(gate_ref[...])).astype(o_ref.dtype)


def sb_attention_prompt(proj, batch, seq, tq=128):
    assert seq % tq == 0 and seq >= SB_BLOCKS_PER_STEP * tq
    nq = seq // tq
    gw = GROUP * HEAD_DIM
    return pl.pallas_call(
        functools.partial(_sb_prompt_kernel, tq=tq),
        grid=(batch, N_KV_HEADS, nq),
        in_specs=[pl.BlockSpec((tq, gw), lambda b, h, i: (b * nq + i, h)),
                  pl.BlockSpec((seq, HEAD_DIM), lambda b, h, i: (b, K_COL + h)),
                  pl.BlockSpec((seq, HEAD_DIM), lambda b, h, i: (b, V_COL + h)),
                  pl.BlockSpec((tq, gw), lambda b, h, i: (b * nq + i, GATE_COL + h))],
        out_specs=pl.BlockSpec((tq, gw), lambda b, h, i: (b * nq + i, h)),
        out_shape=jax.ShapeDtypeStruct((batch * seq, ATTN_WIDTH), BF16),
        scratch_shapes=[pltpu.VMEM((GROUP * tq, 1), F32), pltpu.VMEM((GROUP * tq, HEAD_DIM), F32)],
        compiler_params=_cparams("parallel", "parallel", "parallel"),
        name="sb_attention_prompt",
    )(proj, proj, proj, proj)


def _pad_rows(x, rows):
    return jnp.concatenate([x, jnp.zeros((rows - x.shape[0], x.shape[1]), x.dtype)], axis=0)


PAGES_PER_STEP = 8


def _page_specs(n_pages, page_offset, latest_first):
    n_steps = n_pages // PAGES_PER_STEP

    def index_map(b, p, pt, *, j):
        step = n_steps - 1 - p if latest_first else p
        return (page_offset + pt[b, step * PAGES_PER_STEP + j], 0, 0)

    return [pl.BlockSpec((1, PAGE_SIZE * N_KV_HEADS, HEAD_DIM), functools.partial(index_map, j=j))
            for j in range(PAGES_PER_STEP)]


def _head_pages(page_refs, h):
    return jnp.concatenate([r[0, pl.ds(h, PAGE_SIZE, stride=N_KV_HEADS), :] for r in page_refs], axis=0)


def _flat_pool(cache):
    return cache.reshape(cache.shape[0] * cache.shape[1], PAGE_SIZE * N_KV_HEADS, HEAD_DIM)


def _sb_sample_kernel(pt_ref, proj_ref, *refs, tq):
    del pt_ref
    k_refs, v_refs = refs[:PAGES_PER_STEP], refs[PAGES_PER_STEP:2 * PAGES_PER_STEP]
    o_ref, carry_scr, acc_scr = refs[2 * PAGES_PER_STEP:]
    p = pl.program_id(1)
    rows = GROUP * tq
    later_mat = _later_matrix(PAGE_SIZE)

    def per_head(f):
        return jnp.stack([f(h) for h in range(N_KV_HEADS)])

    def q_rows(h):
        q = proj_ref[:, h * GROUP * HEAD_DIM:(h + 1) * GROUP * HEAD_DIM]
        return (_stack_heads(q, GROUP) * SCALE).astype(BF16)

    def new_rows(col):
        return per_head(lambda h: _pad_rows(proj_ref[:, col + h * HEAD_DIM:col + (h + 1) * HEAD_DIM],
                                            PAGE_SIZE).astype(BF16))

    qs = per_head(q_rows)

    @pl.when(p == 0)
    def _():
        t_loc = lax.broadcasted_iota(jnp.int32, (rows, PAGE_SIZE), 0) & (tq - 1)
        s_loc = lax.broadcasted_iota(jnp.int32, (rows, PAGE_SIZE), 1)
        c, a = _sb_block(qs, new_rows(ATTN_WIDTH), new_rows(ATTN_WIDTH + KV_WIDTH),
                         jnp.zeros((N_KV_HEADS, rows, 1), F32), jnp.zeros((N_KV_HEADS, rows, HEAD_DIM), F32),
                         later_mat, s_loc < t_loc)
        carry_scr[...] = c
        acc_scr[...] = a

    c, a = _sb_block(qs, per_head(lambda h: _head_pages(k_refs, h).astype(BF16)),
                     per_head(lambda h: _head_pages(v_refs, h).astype(BF16)),
                     carry_scr[...], acc_scr[...], later_mat)
    carry_scr[...] = c
    acc_scr[...] = a

    @pl.when(p == pl.num_programs(1) - 1)
    def _():
        gc = ATTN_WIDTH + 2 * KV_WIDTH
        for h in range(N_KV_HEADS):
            sl = slice(h * GROUP * HEAD_DIM, (h + 1) * GROUP * HEAD_DIM)
            gate = proj_ref[:, gc + sl.start:gc + sl.stop]
            o_ref[:, sl] = _unstack_heads(acc_scr[h], GROUP) * _silu(gate)


def sb_attention_sample(proj, k_pool, v_pool, page_offset, page_table, tq):
    batch, n_pages = page_table.shape
    assert n_pages % PAGES_PER_STEP == 0
    rows = GROUP * tq
    specs = _page_specs(n_pages, page_offset, latest_first=True)
    return pl.pallas_call(
        functools.partial(_sb_sample_kernel, tq=tq),
        grid_spec=pltpu.PrefetchScalarGridSpec(
            num_scalar_prefetch=1,
            grid=(batch, n_pages // PAGES_PER_STEP),
            in_specs=[pl.BlockSpec((tq, ATTN_IN), lambda b, p, pt: (b, 0))] + specs + specs,
            out_specs=pl.BlockSpec((tq, ATTN_WIDTH), lambda b, p, pt: (b, 0)),
            scratch_shapes=[pltpu.VMEM((N_KV_HEADS, rows, 1), F32),
                            pltpu.VMEM((N_KV_HEADS, rows, HEAD_DIM), F32)]),
        out_shape=jax.ShapeDtypeStruct((batch * tq, ATTN_WIDTH), F32),
        compiler_params=_cparams("parallel", "arbitrary"),
        name="sb_attention_sample",
    )(page_table, proj, *([k_pool] * PAGES_PER_STEP), *([v_pool] * PAGES_PER_STEP))


def _gdn_prep_kernel(u_ref, c0_ref, w_ref, o_ref, carry_scr, *, tt, tc):
    c = pl.program_id(1)
    t = pl.program_id(2)

    @pl.when(t == 0)
    def _():
        carry_scr[...] = c0_ref[0]

    u = u_ref[...]
    prev = carry_scr[...]
    w = w_ref[...]
    row = lax.broadcasted_iota(jnp.int32, (SUBLANES, tc), 0)
    y = u * w[GDN_CONV - 1:GDN_CONV, :]
    for i in range(1, GDN_CONV):
        ru = pltpu.roll(u, i, 0)
        top = jnp.where(row < i, pltpu.roll(prev, i, 0), ru[:SUBLANES])
        shifted = top if tt == SUBLANES else jnp.concatenate([top, ru[SUBLANES:]], axis=0)
        y = y + shifted * w[GDN_CONV - 1 - i:GDN_CONV - i, :]
    carry_scr[...] = u[tt - SUBLANES:, :]
    a = _silu(y)

    n_q_tiles = GDN_KEY_WIDTH // tc

    @pl.when(c < 2 * n_q_tiles)
    def _():
        scale = jnp.where(c < n_q_tiles, GDN_HEAD_DIM ** -0.5, 1.0)
        for s in range(tc // GDN_HEAD_DIM):
            sl = slice(s * GDN_HEAD_DIM, (s + 1) * GDN_HEAD_DIM)
            x = a[:, sl]
            o_ref[:, sl] = x * lax.rsqrt(jnp.sum(x * x, axis=-1, keepdims=True) + EPS) * scale

    @pl.when(c >= 2 * n_q_tiles)
    def _():
        o_ref[...] = a


def gdn_prep(proj, conv_state, conv_w, batch, seq, tc=512):
    tt = min(seq, 256)
    nt = seq // tt
    c0 = jnp.pad(conv_state, ((0, 0), (SUBLANES - (GDN_CONV - 1), 0), (0, 0)))
    return pl.pallas_call(
        functools.partial(_gdn_prep_kernel, tt=tt, tc=tc),
        grid=(batch, GDN_CONV_CH // tc, nt),
        in_specs=[pl.BlockSpec((tt, tc), lambda b, c, t: (b * nt + t, c)),
                  pl.BlockSpec((1, SUBLANES, tc), lambda b, c, t: (b, 0, c)),
                  pl.BlockSpec((GDN_CONV, tc), lambda b, c, t: (0, c))],
        out_specs=pl.BlockSpec((tt, tc), lambda b, c, t: (b * nt + t, c)),
        out_shape=jax.ShapeDtypeStruct((batch * seq, GDN_CONV_CH), F32),
        scratch_shapes=[pltpu.VMEM((SUBLANES, tc), F32)],
        compiler_params=_cparams("parallel", "parallel", "arbitrary"),
        name="gdn_prep",
    )(proj, c0, conv_w)


GDN_ROWS = 128
GDN_REP = GDN_V_HEADS // GDN_K_HEADS


def _gdn_delta_kernel(q_ref, k_ref, v_ref, bd_ref, z_ref, alog_ref, dt_ref, onorm_ref, s0_ref,
                      o_ref, s_ref, u_scr, w_scr, in_scr, qd_scr, kdt_scr, gl_scr, gt_scr, *, seq):
    hk = pl.program_id(1)
    c = GDN_ROWS
    n_chunks = max(seq // c, 1)
    unroll = gt_scr.shape[0]
    n_doublings = max(math.ceil(math.log2(min(seq, c))) - 1, 0)
    ri = lax.broadcasted_iota(jnp.int32, (c, c), 0)
    ci = lax.broadcasted_iota(jnp.int32, (c, c), 1)
    lane = lax.broadcasted_iota(jnp.int32, (c, LANES), 1)
    row = lax.broadcasted_iota(jnp.int32, (c, 1), 0)
    tril = ri >= ci
    tril_f = jnp.where(tril, 1.0, 0.0)
    neg_a = -jnp.exp(alog_ref[...])

    def chunk_rows(i):
        return pl.ds(i * c, c) if isinstance(i, int) else pl.ds(pl.multiple_of(i * c, c), c)

    def rows(ref, i):
        if seq < c:
            return _pad_rows(ref[...], c)
        return ref[chunk_rows(i), :]

    def prepare(i, _):
        ms, rhss = [], []
        for j in range(unroll):
            ch = i * unroll + j
            sl = chunk_rows(ch)
            q, k, v2, raw = rows(q_ref, ch), rows(k_ref, ch), rows(v_ref, ch), rows(bd_ref, ch)
            k16 = k.astype(BF16)
            kk = _dot_nt(k16, k16)
            qk = _dot_nt(q.astype(BF16), k16)
            sig = jax.nn.sigmoid(raw)
            g_all = jnp.where(row < seq, neg_a * _softplus(raw + dt_ref[...]), 0.0)
            gcx = _dot(tril_f, g_all, HI)
            gt_scr[j] = gcx.T
            for e in range(GDN_REP):
                hv = hk * GDN_REP + e
                beta = jnp.sum(jnp.where(lane == hv, sig, 0.0), axis=1, keepdims=True)
                gc = jnp.sum(jnp.where(lane == hv + GDN_V_HEADS, gcx, 0.0), axis=1, keepdims=True)
                g_row = gt_scr[j, pl.ds(hv + GDN_V_HEADS, 1), :]
                decay = jnp.exp(jnp.where(tril, gc - g_row, NEG))
                ms.append(-jnp.where(ri > ci, kk * beta * decay, 0.0))
                egc = jnp.exp(gc)
                v = v2[:, e * GDN_HEAD_DIM:(e + 1) * GDN_HEAD_DIM]
                rhss.append(jnp.concatenate([v * beta, k * (beta * egc)], axis=1))
                g_last = gc[c - 1:c, :]
                in_scr[e, sl, :] = (qk * decay).astype(BF16)
                qd_scr[e, sl, :] = (q * egc).astype(BF16)
                kdt_scr[e, ch] = (k * jnp.exp(g_last - gc)).T.astype(BF16)
                gl_scr[e, ch] = jnp.broadcast_to(jnp.exp(g_last), (SUBLANES, LANES))
        nmat = jnp.stack(ms)
        pw = _split_bf16(nmat)
        for _ in range(n_doublings):
            sq = _bdot_split(pw, pw)
            pw = _split_bf16(sq)
            nmat = nmat + sq + _bdot_split(_split_bf16(nmat), pw)
        rhs = jnp.stack(rhss)
        sol = rhs + _bdot(nmat.astype(BF16), rhs.astype(BF16))
        for j in range(unroll):
            sl = chunk_rows(i * unroll + j)
            for e in range(GDN_REP):
                x = sol[j * GDN_REP + e]
                u_scr[e, sl, :] = x[:, :GDN_HEAD_DIM]
                w_scr[e, sl, :] = x[:, GDN_HEAD_DIM:].astype(BF16)
        return 0

    def advance(i, states):
        sl = chunk_rows(i)
        z2 = rows(z_ref, i)
        out = []
        for e in range(GDN_REP):
            s = states[e]
            s16 = s.astype(BF16)
            v_new = u_scr[e, sl, :] - _dot(w_scr[e, sl, :], s16)
            v16 = v_new.astype(BF16)
            o = _dot(qd_scr[e, sl, :], s16) + _dot(in_scr[e, sl, :], v16)
            out.append(s * gl_scr[e, i][0:1, :] + _dot(kdt_scr[e, i], v16))
            ms = jnp.mean(o * o, axis=-1, keepdims=True)
            z = z2[:, e * GDN_HEAD_DIM:(e + 1) * GDN_HEAD_DIM]
            o = (o * lax.rsqrt(ms + EPS) * onorm_ref[...] * _silu(z)).astype(o_ref.dtype)
            cols = slice(e * GDN_HEAD_DIM, (e + 1) * GDN_HEAD_DIM)
            if seq < c:
                o_ref[:, cols] = o[:seq]
            else:
                o_ref[sl, cols] = o
        return tuple(out)

    states = tuple(s0_ref[0, e] for e in range(GDN_REP))
    if n_chunks == 1:
        prepare(0, 0)
        states = advance(0, states)
    else:
        n_groups = n_chunks // unroll
        prepare(0, 0)

        def body(i, states):
            for j in range(unroll):
                states = advance((i - 1) * unroll + j, states)
            prepare(i, 0)
            return states

        states = lax.fori_loop(1, n_groups, body, states)
        for j in range(unroll):
            states = advance((n_groups - 1) * unroll + j, states)
    for e in range(GDN_REP):
        s_ref[0, e] = states[e]


def gdn_delta(act, proj, a_log, dt_bias, o_norm, s0, batch, seq, out_dtype):
    assert seq % GDN_ROWS == 0 or seq < GDN_ROWS
    pad = jnp.zeros((GDN_V_HEADS,), F32)
    tail = jnp.zeros((LANES - 2 * GDN_V_HEADS,), F32)
    alog = jnp.concatenate([pad, a_log, tail]).reshape(1, LANES)
    dt = jnp.concatenate([pad, dt_bias, tail]).reshape(1, LANES)
    blk = (seq, GDN_HEAD_DIM)
    wide = (seq, GDN_REP * GDN_HEAD_DIM)
    state_blk = (1, GDN_REP, GDN_HEAD_DIM, GDN_HEAD_DIM)
    rows = max(seq, GDN_ROWS)
    n_chunks = rows // GDN_ROWS
    return pl.pallas_call(
        functools.partial(_gdn_delta_kernel, seq=seq),
        grid=(batch, GDN_K_HEADS),
        in_specs=[pl.BlockSpec(blk, lambda b, h: (b, h)),
                  pl.BlockSpec(blk, lambda b, h: (b, GDN_K_HEADS + h)),
                  pl.BlockSpec(wide, lambda b, h: (b, 2 * GDN_K_HEADS // GDN_REP + h)),
                  pl.BlockSpec((seq, LANES), lambda b, h: (b, GDN_BD_COL)),
                  pl.BlockSpec(wide, lambda b, h: (b, GDN_Z_COL // GDN_REP + h)),
                  pl.BlockSpec((1, LANES), lambda b, h: (0, 0)),
                  pl.BlockSpec((1, LANES), lambda b, h: (0, 0)),
                  pl.BlockSpec((1, GDN_HEAD_DIM), lambda b, h: (0, 0)),
                  pl.BlockSpec(state_blk, lambda b, h: (b, h, 0, 0))],
        out_specs=[pl.BlockSpec(wide, lambda b, h: (b, h)),
                   pl.BlockSpec(state_blk, lambda b, h: (b, h, 0, 0))],
        scratch_shapes=[pltpu.VMEM((GDN_REP, rows, GDN_HEAD_DIM), F32),
                        pltpu.VMEM((GDN_REP, rows, GDN_HEAD_DIM), BF16),
                        pltpu.VMEM((GDN_REP, rows, GDN_ROWS), BF16),
                        pltpu.VMEM((GDN_REP, rows, GDN_HEAD_DIM), BF16),
                        pltpu.VMEM((GDN_REP, n_chunks, GDN_HEAD_DIM, GDN_ROWS), BF16),
                        pltpu.VMEM((GDN_REP, n_chunks, SUBLANES, LANES), F32),
                        pltpu.VMEM((2 if n_chunks % 2 == 0 else 1, GDN_ROWS, LANES), F32)],
        out_shape=[jax.ShapeDtypeStruct((batch * seq, GDN_VAL_WIDTH), out_dtype),
                   jax.ShapeDtypeStruct((batch, GDN_V_HEADS, GDN_HEAD_DIM, GDN_HEAD_DIM), F32)],
        compiler_params=_cparams("parallel", "parallel"),
        name="gdn_delta",
    )(act, act, act, proj, proj, alog, dt, o_norm.reshape(1, GDN_HEAD_DIM), s0)


def _rel_bucket(dist):
    max_exact = REL_BUCKETS // 2
    n = jnp.maximum(dist, 0)
    large = max_exact + (jnp.log(jnp.maximum(n, 1).astype(F32) / max_exact)
                         / math.log(REL_MAX_DIST / max_exact) * (REL_BUCKETS - max_exact)).astype(jnp.int32)
    large = jnp.minimum(large, REL_BUCKETS - 1)
    return jnp.where(n < max_exact, n, large)


def _bias_table_kernel(rbt_ref, onehot_ref, o_ref):
    o_ref[...] = _dot(rbt_ref[...], onehot_ref[...], HI)


def bias_by_distance(rel_bias, dist):
    bucket = _rel_bucket(dist)
    onehot = (bucket[None, :] == jnp.arange(REL_BUCKETS)[:, None]) & (dist[None, :] >= 0)
    return pl.pallas_call(
        _bias_table_kernel,
        out_shape=jax.ShapeDtypeStruct((N_HEADS, dist.shape[0]), F32),
        compiler_params=pltpu.CompilerParams(vmem_limit_bytes=VMEM_LIMIT),
        name="bias_table",
    )(rel_bias.T, onehot.astype(F32))


def _toeplitz(window_row, rows, shift):
    x = jnp.broadcast_to(window_row, (rows, window_row.shape[1]))
    return pltpu.roll(x, shift, 1, stride=1, stride_axis=0)


def _select_topk(gate, n_valid, n_blocks):
    lane = lax.broadcasted_iota(jnp.int32, gate.shape, 1)
    valid = lane < n_valid
    gm = jnp.where(valid, gate, -jnp.inf)
    cnt = jnp.zeros(gate.shape, jnp.int32)
    for m in range(n_blocks):
        col = gm[:, m:m + 1]
        beats = (col > gm) | ((col == gm) & (lane > m))
        cnt = cnt + jnp.where(beats, 1, 0)
    return jnp.where(valid & (cnt < MOBA_TOPK), 1.0, 0.0)


def _select_topk_t(gate_t, n_valid):
    blk = lax.broadcasted_iota(jnp.int32, gate_t.shape, 0)
    valid = blk < n_valid
    gm = jnp.where(valid, gate_t, -jnp.inf)
    cnt = jnp.zeros(gate_t.shape, jnp.int32)
    for m in range(gate_t.shape[0]):
        row = gm[m:m + 1, :]
        beats = (row > gm) | ((row == gm) & (blk > m))
        cnt = cnt + jnp.where(beats, 1, 0)
    return jnp.where(valid & (cnt < MOBA_TOPK), 1.0, 0.0)


def _moba_tile(qs, k, v, bias, mask, m, l, acc):
    batched = qs.ndim == 3
    s = jnp.einsum("hrd,hsd->hrs", qs, k, preferred_element_type=F32) if batched else _dot_nt(qs, k)
    s = jnp.where(mask, s * SCALE + bias, NEG)
    m_new = jnp.maximum(m, jnp.max(s, axis=-1, keepdims=True))
    p = jnp.where(mask, jnp.exp(s - m_new), 0.0)
    alpha = jnp.exp(m - m_new)
    l = alpha * l + jnp.sum(p, axis=-1, keepdims=True)
    p = p.astype(BF16)
    pv = jnp.einsum("hrs,hsd->hrd", p, v, preferred_element_type=F32) if batched else _dot(p, v)
    return m_new, l, alpha * acc + pv


def _moba_prompt_kernel(q_ref, k_ref, v_ref, gate_ref, tb_ref, o_ref,
                        means_scr, bias_scr, pen_scr, m_scr, l_scr, acc_scr, *, tq, n_blocks):
    h = pl.program_id(0)
    b = pl.program_id(1)
    qi = pl.program_id(2)
    rows = GROUP * tq

    @pl.when(qi == 0)
    def _():
        means_scr[...] = jnp.zeros_like(means_scr)
        means_scr[0:n_blocks, :] = jnp.mean(k_ref[...].reshape(n_blocks, MOBA_BLOCK, HEAD_DIM), axis=1)

    @pl.when(b == 0)
    def _():
        for g in range(GROUP):
            wrow = tb_ref[pl.ds((h * GROUP + g) * n_blocks + qi, 1), :]
            bias_scr[qi, pl.ds(g * tq, tq), :] = _toeplitz(wrow, tq, tq + 1)[:, :tq]

    qs32 = _stack_heads(q_ref[...], GROUP)
    qs = (qs32 * SCALE).astype(BF16)
    sel_t = _select_topk_t(_dot_nt(means_scr[...], qs32, HI), qi)
    pen_t = (sel_t - 1.0) * -NEG
    pen_scr[...] = jnp.concatenate([pen_t, jnp.zeros((LANES - pen_t.shape[0], rows), F32)], axis=0).T
    lane = lax.broadcasted_iota(jnp.int32, (rows, LANES), 1)
    t_loc = lax.broadcasted_iota(jnp.int32, (rows, tq), 0) & (tq - 1)
    s_loc = lax.broadcasted_iota(jnp.int32, (rows, tq), 1)

    def load(ref, blk):
        return ref[pl.ds(pl.multiple_of(blk * tq, tq), tq), :].astype(BF16)

    def update(s, v, m, l, acc):
        m_new = jnp.maximum(m, jnp.max(s, axis=1, keepdims=True))
        p = jnp.exp(s - m_new)
        alpha = jnp.exp(m - m_new)
        m_scr[...] = m_new
        l_scr[...] = alpha * l + jnp.sum(p, axis=1, keepdims=True)
        acc_scr[...] = alpha * acc + _dot(p.astype(BF16), v)

    s = jnp.where(s_loc <= t_loc, _dot_nt(qs, load(k_ref, qi)) + bias_scr[0], NEG)
    update(s, load(v_ref, qi), jnp.full((rows, 1), NEG, F32), jnp.zeros((rows, 1), F32),
           jnp.zeros((rows, HEAD_DIM), F32))

    def body(kb, _):
        pen = jnp.sum(jnp.where(lane == kb, pen_scr[...], 0.0), axis=1, keepdims=True)
        update(_dot_nt(qs, load(k_ref, kb)) + bias_scr[qi - kb] + pen, load(v_ref, kb),
               m_scr[...], l_scr[...], acc_scr[...])
        return 0

    lax.fori_loop(0, qi, body, 0)
    o = _unstack_heads(acc_scr[...] / l_scr[...], GROUP)
    o_ref[...] = (o * _silu(gate_ref[...])).astype(o_ref.dtype)


def moba_attention_prompt(proj, rel_bias, batch, seq):
    tq = MOBA_BLOCK
    assert seq % tq == 0
    nb = seq // tq
    gw = GROUP * HEAD_DIM
    rows = GROUP * tq
    dist = (jnp.arange(nb)[:, None] * tq + (tq - 1) - jnp.arange(2 * tq)[None, :]).reshape(-1)
    table = bias_by_distance(rel_bias, dist).reshape(N_HEADS * nb, 2 * tq)
    return pl.pallas_call(
        functools.partial(_moba_prompt_kernel, tq=tq, n_blocks=nb),
        grid=(N_KV_HEADS, batch, nb),
        in_specs=[pl.BlockSpec((tq, gw), lambda h, b, i: (b * nb + i, h)),
                  pl.BlockSpec((seq, HEAD_DIM), lambda h, b, i: (b, K_COL + h)),
                  pl.BlockSpec((seq, HEAD_DIM), lambda h, b, i: (b, V_COL + h)),
                  pl.BlockSpec((tq, gw), lambda h, b, i: (b * nb + i, GATE_COL + h)),
                  pl.BlockSpec((N_HEADS * nb, 2 * tq), lambda h, b, i: (0, 0))],
        out_specs=pl.BlockSpec((tq, gw), lambda h, b, i: (b * nb + i, h)),
        out_shape=jax.ShapeDtypeStruct((batch * seq, ATTN_WIDTH), BF16),
        scratch_shapes=[pltpu.VMEM((-(-nb // SUBLANES) * SUBLANES, HEAD_DIM), F32),
                        pltpu.VMEM((nb, rows, tq), F32),
                        pltpu.VMEM((rows, LANES), F32),
                        pltpu.VMEM((rows, 1), F32),
                        pltpu.VMEM((rows, 1), F32),
                        pltpu.VMEM((rows, HEAD_DIM), F32)],
        compiler_params=_cparams("arbitrary", "arbitrary", "arbitrary"),
        name="moba_attention_prompt",
    )(proj, proj, proj, proj, table)


PAGES_PER_BLOCK = MOBA_BLOCK // PAGE_SIZE
BLOCKS_PER_STEP = PAGES_PER_STEP // PAGES_PER_BLOCK


def _moba_means_kernel(pt_ref, *refs):
    del pt_ref
    page_refs, o_ref = refs[:-1], refs[-1]
    for blk in range(BLOCKS_PER_STEP):
        pages = page_refs[blk * PAGES_PER_BLOCK:(blk + 1) * PAGES_PER_BLOCK]
        for h in range(N_KV_HEADS):
            total = jnp.sum(_head_pages(pages, h), axis=0, keepdims=True)
            o_ref[0, blk, :, h * HEAD_DIM:(h + 1) * HEAD_DIM] = total * (1.0 / MOBA_BLOCK)


def moba_block_means(k_pool, page_offset, page_table):
    batch, n_pages = page_table.shape
    assert n_pages % PAGES_PER_STEP == 0
    nb = n_pages // PAGES_PER_BLOCK
    return pl.pallas_call(
        _moba_means_kernel,
        grid_spec=pltpu.PrefetchScalarGridSpec(
            num_scalar_prefetch=1,
            grid=(batch, n_pages // PAGES_PER_STEP),
            in_specs=_page_specs(n_pages, page_offset, latest_first=False),
            out_specs=pl.BlockSpec((1, BLOCKS_PER_STEP, 1, KV_WIDTH), lambda b, n, pt: (b, n, 0, 0))),
        out_shape=jax.ShapeDtypeStruct((batch, nb, 1, KV_WIDTH), F32),
        compiler_params=_cparams("parallel", "parallel"),
        name="moba_block_means",
    )(page_table, *([k_pool] * PAGES_PER_STEP))


def _moba_sample_kernel(pt_ref, proj_ref, means_ref, tb_ref, *refs, tq, n_pages):
    del pt_ref
    k_refs, v_refs = refs[:PAGES_PER_STEP], refs[PAGES_PER_STEP:2 * PAGES_PER_STEP]
    o_ref, sel_scr, m_scr, l_scr, acc_scr = refs[2 * PAGES_PER_STEP:]
    p = pl.program_id(1)
    rows = GROUP * tq
    n_past_blocks = n_pages // PAGES_PER_BLOCK
    lane = lax.broadcasted_iota(jnp.int32, (N_KV_HEADS, rows, LANES), 2)

    def q_rows(h):
        return _stack_heads(proj_ref[:, h * GROUP * HEAD_DIM:(h + 1) * GROUP * HEAD_DIM], GROUP)

    def bias_rows(h, page):
        tiles = []
        for g in range(GROUP):
            wrow = tb_ref[pl.ds((h * GROUP + g) * (n_pages + 1) + page, 1), :]
            tiles.append(_toeplitz(wrow, tq, PAGE_SIZE + 1)[:, :PAGE_SIZE])
        return jnp.concatenate(tiles, axis=0)

    def per_head(f):
        return jnp.stack([f(h) for h in range(N_KV_HEADS)])

    def new_rows(col):
        return per_head(lambda h: _pad_rows(proj_ref[:, col + h * HEAD_DIM:col + (h + 1) * HEAD_DIM],
                                            PAGE_SIZE).astype(BF16))

    qs = per_head(lambda h: q_rows(h).astype(BF16))

    @pl.when(p == 0)
    def _():
        t_loc = lax.broadcasted_iota(jnp.int32, (rows, PAGE_SIZE), 0) & (tq - 1)
        s_loc = lax.broadcasted_iota(jnp.int32, (rows, PAGE_SIZE), 1)
        for h in range(N_KV_HEADS):
            means = _pad_rows(means_ref[0, :, h * HEAD_DIM:(h + 1) * HEAD_DIM], LANES)
            sel_scr[h] = _select_topk(_dot_nt(q_rows(h), means, HI), n_past_blocks, n_past_blocks)
        m, l, acc = _moba_tile(qs, new_rows(ATTN_WIDTH), new_rows(ATTN_WIDTH + KV_WIDTH),
                               per_head(lambda h: bias_rows(h, n_pages)), s_loc <= t_loc,
                               jnp.full((N_KV_HEADS, rows, 1), NEG, F32), jnp.zeros((N_KV_HEADS, rows, 1), F32),
                               jnp.zeros((N_KV_HEADS, rows, HEAD_DIM), F32))
        m_scr[...] = m
        l_scr[...] = l
        acc_scr[...] = acc

    sel = sel_scr[...]
    mask = []
    for j in range(BLOCKS_PER_STEP):
        col = jnp.sum(jnp.where(lane == p * BLOCKS_PER_STEP + j, sel, 0.0), axis=-1, keepdims=True) > 0.5
        mask.append(jnp.broadcast_to(col, (N_KV_HEADS, rows, MOBA_BLOCK)))
    bias = per_head(lambda h: jnp.concatenate([bias_rows(h, p * PAGES_PER_STEP + j)
                                               for j in range(PAGES_PER_STEP)], axis=1))
    m, l, acc = _moba_tile(qs, per_head(lambda h: _head_pages(k_refs, h).astype(BF16)),
                           per_head(lambda h: _head_pages(v_refs, h).astype(BF16)), bias,
                           jnp.concatenate(mask, axis=-1), m_scr[...], l_scr[...], acc_scr[...])
    m_scr[...] = m
    l_scr[...] = l
    acc_scr[...] = acc

    @pl.when(p == pl.num_programs(1) - 1)
    def _():
        gc = ATTN_WIDTH + 2 * KV_WIDTH
        for h in range(N_KV_HEADS):
            sl = slice(h * GROUP * HEAD_DIM, (h + 1) * GROUP * HEAD_DIM)
            gate = proj_ref[:, gc + sl.start:gc + sl.stop]
            o_ref[:, sl] = _unstack_heads(acc_scr[h] / l_scr[h], GROUP) * _silu(gate)


def moba_attention_sample(proj, k_pool, v_pool, page_offset, page_table, rel_bias, tq):
    batch, n_pages = page_table.shape
    assert n_pages % PAGES_PER_STEP == 0 and tq <= MOBA_BLOCK
    rows = GROUP * tq
    nb = n_pages // PAGES_PER_BLOCK
    assert nb <= LANES
    means = moba_block_means(k_pool, page_offset, page_table).reshape(batch, nb, KV_WIDTH)
    dist = ((n_pages - jnp.arange(n_pages + 1))[:, None] * PAGE_SIZE + (PAGE_SIZE - 1)
            - jnp.arange(2 * PAGE_SIZE)[None, :]).reshape(-1)
    table = bias_by_distance(rel_bias, dist).reshape(N_HEADS * (n_pages + 1), 2 * PAGE_SIZE)
    specs = _page_specs(n_pages, page_offset, latest_first=False)
    return pl.pallas_call(
        functools.partial(_moba_sample_kernel, tq=tq, n_pages=n_pages),
        grid_spec=pltpu.PrefetchScalarGridSpec(
            num_scalar_prefetch=1,
            grid=(batch, n_pages // PAGES_PER_STEP),
            in_specs=[pl.BlockSpec((tq, ATTN_IN), lambda b, p, pt: (b, 0)),
                      pl.BlockSpec((1, nb, KV_WIDTH), lambda b, p, pt: (b, 0, 0)),
                      pl.BlockSpec((N_HEADS * (n_pages + 1), 2 * PAGE_SIZE), lambda b, p, pt: (0, 0))]
            + specs + specs,
            out_specs=pl.BlockSpec((tq, ATTN_WIDTH), lambda b, p, pt: (b, 0)),
            scratch_shapes=[pltpu.VMEM((N_KV_HEADS, rows, LANES), F32),
                            pltpu.VMEM((N_KV_HEADS, rows, 1), F32),
                            pltpu.VMEM((N_KV_HEADS, rows, 1), F32),
                            pltpu.VMEM((N_KV_HEADS, rows, HEAD_DIM), F32)]),
        out_shape=jax.ShapeDtypeStruct((batch * tq, ATTN_WIDTH), F32),
        compiler_params=_cparams("parallel", "arbitrary"),
        name="moba_attention_sample",
    )(page_table, proj, means, table, *([k_pool] * PAGES_PER_STEP), *([v_pool] * PAGES_PER_STEP))


N_MIXERS = 3
GDN_IN_PADDED = -(-GDN_IN // 512) * 512


def _new_kv(proj, batch, seq):
    k = proj[:, ATTN_WIDTH:ATTN_WIDTH + KV_WIDTH].reshape(batch, seq, N_KV_HEADS, HEAD_DIM)
    v = proj[:, ATTN_WIDTH + KV_WIDTH:ATTN_WIDTH + 2 * KV_WIDTH].reshape(batch, seq, N_KV_HEADS, HEAD_DIM)
    return k, v


def kernel(x_prompt, x_sample, cache_sb_k, cache_sb_v, state_gdn_conv, state_gdn_rec, cache_moba_k, cache_moba_v, page_table, norm_g, sb_w_in, sb_w_out, gdn_w_in, gdn_conv_w, gdn_a_log, gdn_dt_bias, gdn_o_norm, gdn_w_out, moba_w_in, moba_q_norm, moba_k_norm, moba_w_out, rel_bias):
    bp, tp, d = x_prompt.shape
    bs, ts, _ = x_sample.shape
    n_phys = cache_sb_k.shape[1]
    yp = x_prompt.reshape(bp * tp, d)
    ys = x_sample.reshape(bs * ts, d)
    outs = {name: [] for name in ("sb_kp", "sb_vp", "sb_ks", "sb_vs", "gdn_cp", "gdn_sp", "gdn_cs", "gdn_ss",
                                  "mb_kp", "mb_vp", "mb_ks", "mb_vs")}
    for layer in range(norm_g.shape[0]):
        kind = layer % N_MIXERS
        j = layer // N_MIXERS
        g = norm_g[layer]
        if kind == 0:
            w_in = sb_w_in[j].astype(BF16)
            w_out = sb_w_out[j].astype(BF16)
            pp = norm_matmul(yp, g, w_in)
            ps = norm_matmul(ys, g, w_in)
            op = sb_attention_prompt(pp, bp, tp)
            os_ = sb_attention_sample(ps, _flat_pool(cache_sb_k), _flat_pool(cache_sb_v), j * n_phys, page_table, ts)
            kp, vp = _new_kv(pp, bp, tp)
            ks, vs = _new_kv(ps, bs, ts)
            outs["sb_kp"].append(kp); outs["sb_vp"].append(vp); outs["sb_ks"].append(ks); outs["sb_vs"].append(vs)
        elif kind == 1:
            w_in = jnp.pad(gdn_w_in[j], ((0, 0), (0, GDN_IN_PADDED - GDN_IN))).astype(BF16)
            w_out = gdn_w_out[j].astype(BF16)
            pp = norm_matmul(yp, g, w_in)
            ps = norm_matmul(ys, g, w_in)
            act_p = gdn_prep(pp, jnp.zeros((bp, GDN_CONV - 1, GDN_CONV_CH), F32), gdn_conv_w[j], bp, tp)
            act_s = gdn_prep(ps, state_gdn_conv[j], gdn_conv_w[j], bs, ts)
            s0 = jnp.zeros((bp,) + state_gdn_rec.shape[2:], F32)
            op, sp = gdn_delta(act_p, pp, gdn_a_log[j], gdn_dt_bias[j], gdn_o_norm[j], s0, bp, tp, BF16)
            os_, ss = gdn_delta(act_s, ps, gdn_a_log[j], gdn_dt_bias[j], gdn_o_norm[j], state_gdn_rec[j],
                                bs, ts, F32)
            outs["gdn_cp"].append(pp.reshape(bp, tp, -1)[:, tp - (GDN_CONV - 1):, :GDN_CONV_CH])
            outs["gdn_cs"].append(ps.reshape(bs, ts, -1)[:, ts - (GDN_CONV - 1):, :GDN_CONV_CH])
            outs["gdn_sp"].append(sp); outs["gdn_ss"].append(ss)
        else:
            w_in = moba_w_in[j].astype(BF16)
            w_out = moba_w_out[j].astype(BF16)
            head_gain = jnp.concatenate([jnp.tile(moba_q_norm[j], N_HEADS), jnp.tile(moba_k_norm[j], N_KV_HEADS),
                                         jnp.ones((ATTN_IN - ATTN_WIDTH - KV_WIDTH,), F32)]).reshape(1, ATTN_IN)
            pp = norm_matmul(yp, g, w_in, head_gain, ATTN_WIDTH + KV_WIDTH)
            ps = norm_matmul(ys, g, w_in, head_gain, ATTN_WIDTH + KV_WIDTH)
            op = moba_attention_prompt(pp, rel_bias, bp, tp)
            os_ = moba_attention_sample(ps, _flat_pool(cache_moba_k), _flat_pool(cache_moba_v), j * n_phys,
                                        page_table, rel_bias, ts)
            kp, vp = _new_kv(pp, bp, tp)
            ks, vs = _new_kv(ps, bs, ts)
            outs["mb_kp"].append(kp); outs["mb_vp"].append(vp); outs["mb_ks"].append(ks); outs["mb_vs"].append(vs)
        yp = matmul_residual(op, w_out, yp)
        ys = matmul_residual(os_, w_out, ys)
    stack = lambda name: jnp.stack(outs[name])
    return (yp.reshape(bp, tp, d), ys.reshape(bs, ts, d),
            stack("sb_kp"), stack("sb_vp"), stack("sb_ks"), stack("sb_vs"),
            stack("gdn_cp"), stack("gdn_sp"), stack("gdn_cs"), stack("gdn_ss"),
            stack("mb_kp"), stack("mb_vp"), stack("mb_ks"), stack("mb_vs"))
```

```python
import functools
import math

import jax
import jax.numpy as jnp
from jax import lax
from jax.experimental import pallas as pl
from jax.experimental.pallas import tpu as pltpu

F32 = jnp.float32
BF16 = jnp.bfloat16
HI = lax.Precision.HIGHEST

LANES = 128
SUBLANES = 8
VMEM_LIMIT = 56 * 1024 * 1024

HEAD_DIM = 128
N_HEADS = 16
N_KV_HEADS = 4
GROUP = N_HEADS // N_KV_HEADS
ATTN_WIDTH = N_HEADS * HEAD_DIM
KV_WIDTH = N_KV_HEADS * HEAD_DIM
ATTN_IN = 2 * ATTN_WIDTH + 2 * KV_WIDTH
Q_COL = 0
K_COL = ATTN_WIDTH // HEAD_DIM
V_COL = K_COL + N_KV_HEADS
GATE_COL = (ATTN_WIDTH + 2 * KV_WIDTH) // (GROUP * HEAD_DIM)
PAGE_SIZE = 128

GDN_HEAD_DIM = 128
GDN_K_HEADS = 16
GDN_V_HEADS = 32
GDN_KEY_WIDTH = GDN_K_HEADS * GDN_HEAD_DIM
GDN_VAL_WIDTH = GDN_V_HEADS * GDN_HEAD_DIM
GDN_CONV_CH = 2 * GDN_KEY_WIDTH + GDN_VAL_WIDTH
GDN_CONV = 4
GDN_CHUNK = 64
GDN_IN = GDN_CONV_CH + GDN_VAL_WIDTH + 2 * GDN_V_HEADS
GDN_Z_COL = GDN_CONV_CH // GDN_HEAD_DIM
GDN_BD_COL = (GDN_CONV_CH + GDN_VAL_WIDTH) // LANES

MOBA_BLOCK = 256
MOBA_TOPK = 3
REL_BUCKETS = 32
REL_MAX_DIST = 4096

EPS = 1e-6
NEG = -1e30
SCALE = HEAD_DIM ** -0.5

_NT = (((1,), (1,)), ((), ()))


def _cparams(*sem):
    return pltpu.CompilerParams(dimension_semantics=sem, vmem_limit_bytes=VMEM_LIMIT)


def _softplus(z):
    return jnp.maximum(z, 0.0) + jnp.log1p(jnp.exp(-jnp.abs(z)))


def _silu(x):
    return x * jax.nn.sigmoid(x)


def _dot(a, b, precision=None):
    return jnp.dot(a, b, preferred_element_type=F32, precision=precision)


def _dot_nt(a, b, precision=None):
    return lax.dot_general(a, b, _NT, preferred_element_type=F32, precision=precision)


def _split_bf16(x):
    hi = x.astype(BF16)
    return hi, (x - hi.astype(F32)).astype(BF16)


def _bdot(a, b):
    return jnp.einsum("bij,bjk->bik", a, b, preferred_element_type=F32)


def _bdot_split(a, b):
    return _bdot(a[0], b[0]) + (_bdot(a[0], b[1]) + _bdot(a[1], b[0]))


def _norm_matmul_kernel(x_ref, g_ref, w_ref, hg_ref, o_ref, h_scr, *, n_norm_tiles, tn):
    j = pl.program_id(1)

    @pl.when(j == 0)
    def _():
        x = x_ref[...]
        ms = jnp.mean(x * x, axis=-1, keepdims=True)
        h_scr[...] = (x * lax.rsqrt(ms + EPS) * g_ref[...]).astype(BF16)

    acc = _dot(h_scr[...], w_ref[...])
    if n_norm_tiles == 0:
        o_ref[...] = acc
    else:
        @pl.when(j < n_norm_tiles)
        def _():
            for s in range(tn // HEAD_DIM):
                sl = slice(s * HEAD_DIM, (s + 1) * HEAD_DIM)
                a = acc[:, sl]
                ms = jnp.mean(a * a, axis=-1, keepdims=True)
                o_ref[:, sl] = a * lax.rsqrt(ms + EPS) * hg_ref[:, sl]

        @pl.when(j >= n_norm_tiles)
        def _():
            o_ref[...] = acc


def norm_matmul(x, g, w, head_gain=None, n_norm_cols=0, tn=512):
    m, d = x.shape
    n = w.shape[1]
    tm = min(m, 1024)
    assert m % tm == 0 and n % tn == 0 and n_norm_cols % tn == 0
    if head_gain is None:
        head_gain = jnp.ones((1, n), F32)
    return pl.pallas_call(
        functools.partial(_norm_matmul_kernel, n_norm_tiles=n_norm_cols // tn, tn=tn),
        grid=(m // tm, n // tn),
        in_specs=[pl.BlockSpec((tm, d), lambda i, j: (i, 0)),
                  pl.BlockSpec((1, d), lambda i, j: (0, 0)),
                  pl.BlockSpec((d, tn), lambda i, j: (0, j)),
                  pl.BlockSpec((1, tn), lambda i, j: (0, j))],
        out_specs=pl.BlockSpec((tm, tn), lambda i, j: (i, j)),
        out_shape=jax.ShapeDtypeStruct((m, n), F32),
        scratch_shapes=[pltpu.VMEM((tm, d), BF16)],
        compiler_params=_cparams("parallel", "arbitrary"),
        name="norm_matmul",
    )(x, g.reshape(1, d), w, head_gain)


def _matmul_residual_kernel(a_ref, w_ref, r_ref, o_ref):
    o_ref[...] = r_ref[...] + _dot(a_ref[...].astype(BF16), w_ref[...])


def matmul_residual(a, w, res, tn=512):
    m, k = a.shape
    n = w.shape[1]
    tm = min(m, 1024)
    assert m % tm == 0 and n % tn == 0
    return pl.pallas_call(
        _matmul_residual_kernel,
        grid=(m // tm, n // tn),
        in_specs=[pl.BlockSpec((tm, k), lambda i, j: (i, 0)),
                  pl.BlockSpec((k, tn), lambda i, j: (0, j)),
                  pl.BlockSpec((tm, tn), lambda i, j: (i, j))],
        out_specs=pl.BlockSpec((tm, tn), lambda i, j: (i, j)),
        out_shape=jax.ShapeDtypeStruct((m, n), F32),
        compiler_params=_cparams("parallel", "parallel"),
        name="matmul_residual",
    )(a, w, res)


def _stack_heads(x, n):
    return jnp.concatenate([x[:, g * HEAD_DIM:(g + 1) * HEAD_DIM] for g in range(n)], axis=0)


def _unstack_heads(x, n):
    t = x.shape[0] // n
    return jnp.concatenate([x[g * t:(g + 1) * t] for g in range(n)], axis=1)


def _later_matrix(tk):
    r = lax.broadcasted_iota(jnp.int32, (2 * tk, tk), 0) & (tk - 1)
    c = lax.broadcasted_iota(jnp.int32, (2 * tk, tk), 1)
    return jnp.where(r > c, -1.0, 0.0).astype(BF16)


def _sb_block(qs, k, v, carry, acc, later_mat, mask=None):
    tk = later_mat.shape[1]
    n = k.shape[-2] // tk
    batched = qs.ndim == 3
    z = jnp.einsum("hrd,hsd->hrs", qs, k, preferred_element_type=F32) if batched else _dot_nt(qs, k)
    if mask is not None:
        z = jnp.where(mask, z, NEG)
    sp = jnp.maximum(z, 0.0) + jnp.log(1.0 + jnp.exp(-jnp.abs(z)))
    hi = sp.astype(BF16)
    lo = (sp - hi.astype(F32)).astype(BF16)
    later, total = [], []
    for j in range(n):
        sl = slice(j * tk, (j + 1) * tk)
        split = jnp.concatenate([hi[..., sl], lo[..., sl]], axis=-1)
        lt = _dot(split.reshape(-1, 2 * tk), later_mat).reshape(z.shape[:-1] + (tk,))
        later.append(lt)
        total.append(lt[..., :1] - sp[..., j * tk:j * tk + 1])
    for j in reversed(range(n)):
        later[j] = later[j] + carry
        carry = carry + total[j]
    later = later[0] if n == 1 else jnp.concatenate(later, axis=-1)
    w = jnp.exp(z - sp + later).astype(BF16)
    pv = jnp.einsum("hrs,hsd->hrd", w, v, preferred_element_type=F32) if batched else _dot(w, v)
    return carry, acc + pv


SB_BLOCKS_PER_STEP = 4


def _sb_prompt_kernel(q_ref, k_ref, v_ref, gate_ref, o_ref, carry_scr, acc_scr, *, tq):
    qi = pl.program_id(2)
    rows = GROUP * tq
    nk = SB_BLOCKS_PER_STEP
    qs = (_stack_heads(q_ref[...], GROUP) * SCALE).astype(BF16)
    later_mat = _later_matrix(tq)
    q_pos = qi * tq + (lax.broadcasted_iota(jnp.int32, (rows, 1), 0) & (tq - 1))
    s_off = lax.broadcasted_iota(jnp.int32, (rows, nk * tq), 1)
    carry_scr[...] = jnp.zeros_like(carry_scr)
    acc_scr[...] = jnp.zeros_like(acc_scr)

    def body(i, _):
        top = qi + 1 - nk * i
        start = pl.multiple_of(jnp.maximum(top - nk, 0) * tq, tq)
        k = k_ref[pl.ds(start, nk * tq), :].astype(BF16)
        v = v_ref[pl.ds(start, nk * tq), :].astype(BF16)
        mask = s_off + start < jnp.minimum(top * tq, q_pos)
        c, a = _sb_block(qs, k, v, carry_scr[...], acc_scr[...], later_mat, mask)
        carry_scr[...] = c
        acc_scr[...] = a
        return 0

    lax.fori_loop(0, (qi + nk) // nk, body, 0)
    o = _unstack_heads(acc_scr[...], GROUP)
    o_ref[...] = (o * _silu(gate_ref[...])).astype(o_ref.dtype)


def sb_attention_prompt(proj, batch, seq, tq=128):
    assert seq % tq == 0 and seq >= SB_BLOCKS_PER_STEP * tq
    nq = seq // tq
    gw = GROUP * HEAD_DIM
    return pl.pallas_call(
        functools.partial(_sb_prompt_kernel, tq=tq),
        grid=(batch, N_KV_HEADS, nq),
        in_specs=[pl.BlockSpec((tq, gw), lambda b, h, i: (b * nq + i, h)),
                  pl.BlockSpec((seq, HEAD_DIM), lambda b, h, i: (b, K_COL + h)),
                  pl.BlockSpec((seq, HEAD_DIM), lambda b, h, i: (b, V_COL + h)),
                  pl.BlockSpec((tq, gw), lambda b, h, i: (b * nq + i, GATE_COL + h))],
        out_specs=pl.BlockSpec((tq, gw), lambda b, h, i: (b * nq + i, h)),
        out_shape=jax.ShapeDtypeStruct((batch * seq, ATTN_WIDTH), BF16),
        scratch_shapes=[pltpu.VMEM((GROUP * tq, 1), F32), pltpu.VMEM((GROUP * tq, HEAD_DIM), F32)],
        compiler_params=_cparams("parallel", "parallel", "parallel"),
        name="sb_attention_prompt",
    )(proj, proj, proj, proj)


def _pad_rows(x, rows):
    return jnp.concatenate([x, jnp.zeros((rows - x.shape[0], x.shape[1]), x.dtype)], axis=0)


PAGES_PER_STEP = 8


def _page_specs(n_pages, page_offset, latest_first):
    n_steps = n_pages // PAGES_PER_STEP

    def index_map(b, p, pt, *, j):
        step = n_steps - 1 - p if latest_first else p
        return (page_offset + pt[b, step * PAGES_PER_STEP + j], 0, 0)

    return [pl.BlockSpec((1, PAGE_SIZE * N_KV_HEADS, HEAD_DIM), functools.partial(index_map, j=j))
            for j in range(PAGES_PER_STEP)]


def _head_pages(page_refs, h):
    return jnp.concatenate([r[0, pl.ds(h, PAGE_SIZE, stride=N_KV_HEADS), :] for r in page_refs], axis=0)


def _flat_pool(cache):
    return cache.reshape(cache.shape[0] * cache.shape[1], PAGE_SIZE * N_KV_HEADS, HEAD_DIM)


def _sb_sample_kernel(pt_ref, proj_ref, *refs, tq):
    del pt_ref
    k_refs, v_refs = refs[:PAGES_PER_STEP], refs[PAGES_PER_STEP:2 * PAGES_PER_STEP]
    o_ref, carry_scr, acc_scr = refs[2 * PAGES_PER_STEP:]
    p = pl.program_id(1)
    rows = GROUP * tq
    later_mat = _later_matrix(PAGE_SIZE)

    def per_head(f):
        return jnp.stack([f(h) for h in range(N_KV_HEADS)])

    def q_rows(h):
        q = proj_ref[:, h * GROUP * HEAD_DIM:(h + 1) * GROUP * HEAD_DIM]
        return (_stack_heads(q, GROUP) * SCALE).astype(BF16)

    def new_rows(col):
        return per_head(lambda h: _pad_rows(proj_ref[:, col + h * HEAD_DIM:col + (h + 1) * HEAD_DIM],
                                            PAGE_SIZE).astype(BF16))

    qs = per_head(q_rows)

    @pl.when(p == 0)
    def _():
        t_loc = lax.broadcasted_iota(jnp.int32, (rows, PAGE_SIZE), 0) & (tq - 1)
        s_loc = lax.broadcasted_iota(jnp.int32, (rows, PAGE_SIZE), 1)
        c, a = _sb_block(qs, new_rows(ATTN_WIDTH), new_rows(ATTN_WIDTH + KV_WIDTH),
                         jnp.zeros((N_KV_HEADS, rows, 1), F32), jnp.zeros((N_KV_HEADS, rows, HEAD_DIM), F32),
                         later_mat, s_loc < t_loc)
        carry_scr[...] = c
        acc_scr[...] = a

    c, a = _sb_block(qs, per_head(lambda h: _head_pages(k_refs, h).astype(BF16)),
                     per_head(lambda h: _head_pages(v_refs, h).astype(BF16)),
                     carry_scr[...], acc_scr[...], later_mat)
    carry_scr[...] = c
    acc_scr[...] = a

    @pl.when(p == pl.num_programs(1) - 1)
    def _():
        gc = ATTN_WIDTH + 2 * KV_WIDTH
        for h in range(N_KV_HEADS):
            sl = slice(h * GROUP * HEAD_DIM, (h + 1) * GROUP * HEAD_DIM)
            gate = proj_ref[:, gc + sl.start:gc + sl.stop]
            o_ref[:, sl] = _unstack_heads(acc_scr[h], GROUP) * _silu(gate)


def sb_attention_sample(proj, k_pool, v_pool, page_offset, page_table, tq):
    batch, n_pages = page_table.shape
    assert n_pages % PAGES_PER_STEP == 0
    rows = GROUP * tq
    specs = _page_specs(n_pages, page_offset, latest_first=True)
    return pl.pallas_call(
        functools.partial(_sb_sample_kernel, tq=tq),
        grid_spec=pltpu.PrefetchScalarGridSpec(
            num_scalar_prefetch=1,
            grid=(batch, n_pages // PAGES_PER_STEP),
            in_specs=[pl.BlockSpec((tq, ATTN_IN), lambda b, p, pt: (b, 0))] + specs + specs,
            out_specs=pl.BlockSpec((tq, ATTN_WIDTH), lambda b, p, pt: (b, 0)),
            scratch_shapes=[pltpu.VMEM((N_KV_HEADS, rows, 1), F32),
                            pltpu.VMEM((N_KV_HEADS, rows, HEAD_DIM), F32)]),
        out_shape=jax.ShapeDtypeStruct((batch * tq, ATTN_WIDTH), F32),
        compiler_params=_cparams("parallel", "arbitrary"),
        name="sb_attention_sample",
    )(page_table, proj, *([k_pool] * PAGES_PER_STEP), *([v_pool] * PAGES_PER_STEP))


GDN_ROWS = 128
GDN_REP = GDN_V_HEADS // GDN_K_HEADS


def _gdn_delta_kernel(q_ref, k_ref, v_ref, bd_ref, z_ref, cq_ref, ck_ref, cv_ref, wq_ref, wk_ref, wv_ref,
                      alog_ref, dt_ref, onorm_ref, s0_ref,
                      o_ref, s_ref, u_scr, w_scr, in_scr, qd_scr, kdt_scr, gl_scr, gt_scr, *, seq):
    hk = pl.program_id(1)
    c = GDN_ROWS
    n_chunks = max(seq // c, 1)
    unroll = gt_scr.shape[0]
    n_doublings = max(math.ceil(math.log2(min(seq, c))) - 1, 0)
    ri = lax.broadcasted_iota(jnp.int32, (c, c), 0)
    ci = lax.broadcasted_iota(jnp.int32, (c, c), 1)
    lane = lax.broadcasted_iota(jnp.int32, (c, LANES), 1)
    row = lax.broadcasted_iota(jnp.int32, (c, 1), 0)
    tril = ri >= ci
    tril_f = jnp.where(tril, 1.0, 0.0)
    neg_a = -jnp.exp(alog_ref[...])

    def chunk_rows(i):
        return pl.ds(i * c, c) if isinstance(i, int) else pl.ds(pl.multiple_of(i * c, c), c)

    def rows(ref, i):
        if seq < c:
            return _pad_rows(ref[...], c)
        return ref[chunk_rows(i), :]

    def conv_silu(ref, prev_ref, w_ref, i):
        n = min(seq, c)
        if isinstance(i, int) and i == 0:
            ext = jnp.concatenate([prev_ref[0], ref[pl.ds(0, n), :]], axis=0)
        elif isinstance(i, int):
            ext = ref[pl.ds(i * c - SUBLANES, c + SUBLANES), :]
        else:
            ext = ref[pl.ds(pl.multiple_of(i * c - SUBLANES, SUBLANES), c + SUBLANES), :]
        w = w_ref[...]
        y = ext[SUBLANES:] * w[GDN_CONV - 1:GDN_CONV, :]
        for tap in range(1, GDN_CONV):
            y = y + pltpu.roll(ext, tap, 0)[SUBLANES:] * w[GDN_CONV - 1 - tap:GDN_CONV - tap, :]
        y = _silu(y)
        return y if n == c else _pad_rows(y, c)

    def l2_normalised(x, scale):
        return x * (lax.rsqrt(jnp.sum(x * x, axis=-1, keepdims=True) + EPS) * scale)

    def prepare(i, _):
        ms, rhss = [], []
        for j in range(unroll):
            ch = i * unroll + j
            sl = chunk_rows(ch)
            q = l2_normalised(conv_silu(q_ref, cq_ref, wq_ref, ch), GDN_HEAD_DIM ** -0.5)
            k = l2_normalised(conv_silu(k_ref, ck_ref, wk_ref, ch), 1.0)
            v2 = conv_silu(v_ref, cv_ref, wv_ref, ch)
            raw = rows(bd_ref, ch)
            k16 = k.astype(BF16)
            kk = _dot_nt(k16, k16)
            qk = _dot_nt(q.astype(BF16), k16)
            sig = jax.nn.sigmoid(raw)
            g_all = jnp.where(row < seq, neg_a * _softplus(raw + dt_ref[...]), 0.0)
            gcx = _dot(tril_f, g_all, HI)
            gt_scr[j] = gcx.T
            for e in range(GDN_REP):
                hv = hk * GDN_REP + e
                beta = jnp.sum(jnp.where(lane == hv, sig, 0.0), axis=1, keepdims=True)
                gc = jnp.sum(jnp.where(lane == hv + GDN_V_HEADS, gcx, 0.0), axis=1, keepdims=True)
                g_row = gt_scr[j, pl.ds(hv + GDN_V_HEADS, 1), :]
                decay = jnp.exp(jnp.where(tril, gc - g_row, NEG))
                ms.append(-jnp.where(ri > ci, kk * beta * decay, 0.0))
                egc = jnp.exp(gc)
                v = v2[:, e * GDN_HEAD_DIM:(e + 1) * GDN_HEAD_DIM]
                rhss.append(jnp.concatenate([v * beta, k * (beta * egc)], axis=1))
                g_last = gc[c - 1:c, :]
                in_scr[e, sl, :] = (qk * decay).astype(BF16)
                qd_scr[e, sl, :] = (q * egc).astype(BF16)
                kdt_scr[e, ch] = (k * jnp.exp(g_last - gc)).T.astype(BF16)
                gl_scr[e, ch] = jnp.broadcast_to(jnp.exp(g_last), (SUBLANES, LANES))
        nmat = jnp.stack(ms)
        pw = _split_bf16(nmat)
        for _ in range(n_doublings):
            sq = _bdot_split(pw, pw)
            pw = _split_bf16(sq)
            nmat = nmat + sq + _bdot_split(_split_bf16(nmat), pw)
        rhs = jnp.stack(rhss)
        sol = rhs + _bdot(nmat.astype(BF16), rhs.astype(BF16))
        for j in range(unroll):
            sl = chunk_rows(i * unroll + j)
            for e in range(GDN_REP):
                x = sol[j * GDN_REP + e]
                u_scr[e, sl, :] = x[:, :GDN_HEAD_DIM]
                w_scr[e, sl, :] = x[:, GDN_HEAD_DIM:].astype(BF16)
        return 0

    def advance(i, states):
        sl = chunk_rows(i)
        z2 = rows(z_ref, i)
        out = []
        for e in range(GDN_REP):
            s = states[e]
            s16 = s.astype(BF16)
            v_new = u_scr[e, sl, :] - _dot(w_scr[e, sl, :], s16)
            v16 = v_new.astype(BF16)
            o = _dot(qd_scr[e, sl, :], s16) + _dot(in_scr[e, sl, :], v16)
            out.append(s * gl_scr[e, i][0:1, :] + _dot(kdt_scr[e, i], v16))
            ms = jnp.mean(o * o, axis=-1, keepdims=True)
            z = z2[:, e * GDN_HEAD_DIM:(e + 1) * GDN_HEAD_DIM]
            o = (o * lax.rsqrt(ms + EPS) * onorm_ref[...] * _silu(z)).astype(o_ref.dtype)
            cols = slice(e * GDN_HEAD_DIM, (e + 1) * GDN_HEAD_DIM)
            if seq < c:
                o_ref[:, cols] = o[:seq]
            else:
                o_ref[sl, cols] = o
        return tuple(out)

    states = tuple(s0_ref[0, e] for e in range(GDN_REP))
    if n_chunks == 1:
        prepare(0, 0)
        states = advance(0, states)
    else:
        n_groups = n_chunks // unroll
        prepare(0, 0)

        def body(i, states):
            for j in range(unroll):
                states = advance((i - 1) * unroll + j, states)
            prepare(i, 0)
            return states

        states = lax.fori_loop(1, n_groups, body, states)
        for j in range(unroll):
            states = advance((n_groups - 1) * unroll + j, states)
    for e in range(GDN_REP):
        s_ref[0, e] = states[e]


def gdn_delta(proj, conv_state, conv_w, a_log, dt_bias, o_norm, s0, batch, seq, out_dtype):
    assert (seq % GDN_ROWS == 0 or seq < GDN_ROWS) and seq % SUBLANES == 0
    prev = jnp.pad(conv_state, ((0, 0), (SUBLANES - (GDN_CONV - 1), 0), (0, 0)))
    vcol = 2 * GDN_K_HEADS // GDN_REP
    pad = jnp.zeros((GDN_V_HEADS,), F32)
    tail = jnp.zeros((LANES - 2 * GDN_V_HEADS,), F32)
    alog = jnp.concatenate([pad, a_log, tail]).reshape(1, LANES)
    dt = jnp.concatenate([pad, dt_bias, tail]).reshape(1, LANES)
    blk = (seq, GDN_HEAD_DIM)
    wide = (seq, GDN_REP * GDN_HEAD_DIM)
    state_blk = (1, GDN_REP, GDN_HEAD_DIM, GDN_HEAD_DIM)
    rows = max(seq, GDN_ROWS)
    n_chunks = rows // GDN_ROWS
    return pl.pallas_call(
        functools.partial(_gdn_delta_kernel, seq=seq),
        grid=(batch, GDN_K_HEADS),
        in_specs=[pl.BlockSpec(blk, lambda b, h: (b, h)),
                  pl.BlockSpec(blk, lambda b, h: (b, GDN_K_HEADS + h)),
                  pl.BlockSpec(wide, lambda b, h: (b, vcol + h)),
                  pl.BlockSpec((seq, LANES), lambda b, h: (b, GDN_BD_COL)),
                  pl.BlockSpec(wide, lambda b, h: (b, GDN_Z_COL // GDN_REP + h)),
                  pl.BlockSpec((1, SUBLANES, blk[1]), lambda b, h: (b, 0, h)),
                  pl.BlockSpec((1, SUBLANES, blk[1]), lambda b, h: (b, 0, GDN_K_HEADS + h)),
                  pl.BlockSpec((1, SUBLANES, wide[1]), lambda b, h: (b, 0, vcol + h)),
                  pl.BlockSpec((GDN_CONV, blk[1]), lambda b, h: (0, h)),
                  pl.BlockSpec((GDN_CONV, blk[1]), lambda b, h: (0, GDN_K_HEADS + h)),
                  pl.BlockSpec((GDN_CONV, wide[1]), lambda b, h: (0, vcol + h)),
                  pl.BlockSpec((1, LANES), lambda b, h: (0, 0)),
                  pl.BlockSpec((1, LANES), lambda b, h: (0, 0)),
                  pl.BlockSpec((1, GDN_HEAD_DIM), lambda b, h: (0, 0)),
                  pl.BlockSpec(state_blk, lambda b, h: (b, h, 0, 0))],
        out_specs=[pl.BlockSpec(wide, lambda b, h: (b, h)),
                   pl.BlockSpec(state_blk, lambda b, h: (b, h, 0, 0))],
        scratch_shapes=[pltpu.VMEM((GDN_REP, rows, GDN_HEAD_DIM), F32),
                        pltpu.VMEM((GDN_REP, rows, GDN_HEAD_DIM), BF16),
                        pltpu.VMEM((GDN_REP, rows, GDN_ROWS), BF16),
                        pltpu.VMEM((GDN_REP, rows, GDN_HEAD_DIM), BF16),
                        pltpu.VMEM((GDN_REP, n_chunks, GDN_HEAD_DIM, GDN_ROWS), BF16),
                        pltpu.VMEM((GDN_REP, n_chunks, SUBLANES, LANES), F32),
                        pltpu.VMEM((2 if n_chunks % 2 == 0 else 1, GDN_ROWS, LANES), F32)],
        out_shape=[jax.ShapeDtypeStruct((batch * seq, GDN_VAL_WIDTH), out_dtype),
                   jax.ShapeDtypeStruct((batch, GDN_V_HEADS, GDN_HEAD_DIM, GDN_HEAD_DIM), F32)],
        compiler_params=_cparams("parallel", "parallel"),
        name="gdn_delta",
    )(proj, proj, proj, proj, proj, prev, prev, prev, conv_w, conv_w, conv_w,
      alog, dt, o_norm.reshape(1, GDN_HEAD_DIM), s0)


def _rel_bucket(dist):
    max_exact = REL_BUCKETS // 2
    n = jnp.maximum(dist, 0)
    large = max_exact + (jnp.log(jnp.maximum(n, 1).astype(F32) / max_exact)
                         / math.log(REL_MAX_DIST / max_exact) * (REL_BUCKETS - max_exact)).astype(jnp.int32)
    large = jnp.minimum(large, REL_BUCKETS - 1)
    return jnp.where(n < max_exact, n, large)


def _bias_table_kernel(rbt_ref, onehot_ref, o_ref):
    o_ref[...] = _dot(rbt_ref[...], onehot_ref[...], HI)


def bias_by_distance(rel_bias, dist):
    bucket = _rel_bucket(dist)
    onehot = (bucket[None, :] == jnp.arange(REL_BUCKETS)[:, None]) & (dist[None, :] >= 0)
    return pl.pallas_call(
        _bias_table_kernel,
        out_shape=jax.ShapeDtypeStruct((N_HEADS, dist.shape[0]), F32),
        compiler_params=pltpu.CompilerParams(vmem_limit_bytes=VMEM_LIMIT),
        name="bias_table",
    )(rel_bias.T, onehot.astype(F32))


def _toeplitz(window_row, rows, shift):
    x = jnp.broadcast_to(window_row, (rows, window_row.shape[1]))
    return pltpu.roll(x, shift, 1, stride=1, stride_axis=0)


def _select_topk(gate, n_valid, n_blocks):
    lane = lax.broadcasted_iota(jnp.int32, gate.shape, 1)
    valid = lane < n_valid
    gm = jnp.where(valid, gate, -jnp.inf)
    cnt = jnp.zeros(gate.shape, jnp.int32)
    for m in range(n_blocks):
        col = gm[:, m:m + 1]
        beats = (col > gm) | ((col == gm) & (lane > m))
        cnt = cnt + jnp.where(beats, 1, 0)
    return jnp.where(valid & (cnt < MOBA_TOPK), 1.0, 0.0)


def _select_topk_t(gate_t, n_valid):
    blk = lax.broadcasted_iota(jnp.int32, gate_t.shape, 0)
    valid = blk < n_valid
    gm = jnp.where(valid, gate_t, -jnp.inf)
    cnt = jnp.zeros(gate_t.shape, jnp.int32)
    for m in range(gate_t.shape[0]):
        row = gm[m:m + 1, :]
        beats = (row > gm) | ((row == gm) & (blk > m))
        cnt = cnt + jnp.where(beats, 1, 0)
    return jnp.where(valid & (cnt < MOBA_TOPK), 1.0, 0.0)


def _moba_tile(qs, k, v, bias, mask, m, l, acc):
    batched = qs.ndim == 3
    s = jnp.einsum("hrd,hsd->hrs", qs, k, preferred_element_type=F32) if batched else _dot_nt(qs, k)
    s = jnp.where(mask, s * SCALE + bias, NEG)
    m_new = jnp.maximum(m, jnp.max(s, axis=-1, keepdims=True))
    p = jnp.where(mask, jnp.exp(s - m_new), 0.0)
    alpha = jnp.exp(m - m_new)
    l = alpha * l + jnp.sum(p, axis=-1, keepdims=True)
    p = p.astype(BF16)
    pv = jnp.einsum("hrs,hsd->hrd", p, v, preferred_element_type=F32) if batched else _dot(p, v)
    return m_new, l, alpha * acc + pv


def _moba_prompt_kernel(q_ref, k_ref, v_ref, gate_ref, tb_ref, o_ref,
                        means_scr, bias_scr, pen_scr, m_scr, l_scr, acc_scr, *, tq, n_blocks):
    h = pl.program_id(0)
    b = pl.program_id(1)
    qi = pl.program_id(2)
    rows = GROUP * tq

    @pl.when(qi == 0)
    def _():
        means_scr[...] = jnp.zeros_like(means_scr)
        means_scr[0:n_blocks, :] = jnp.mean(k_ref[...].reshape(n_blocks, MOBA_BLOCK, HEAD_DIM), axis=1)

    @pl.when(b == 0)
    def _():
        for g in range(GROUP):
            wrow = tb_ref[pl.ds((h * GROUP + g) * n_blocks + qi, 1), :]
            bias_scr[qi, pl.ds(g * tq, tq), :] = _toeplitz(wrow, tq, tq + 1)[:, :tq]

    qs32 = _stack_heads(q_ref[...], GROUP)
    qs = (qs32 * SCALE).astype(BF16)
    sel_t = _select_topk_t(_dot_nt(means_scr[...], qs32, HI), qi)
    pen_t = (sel_t - 1.0) * -NEG
    pen_scr[...] = jnp.concatenate([pen_t, jnp.zeros((LANES - pen_t.shape[0], rows), F32)], axis=0).T
    lane = lax.broadcasted_iota(jnp.int32, (rows, LANES), 1)
    t_loc = lax.broadcasted_iota(jnp.int32, (rows, tq), 0) & (tq - 1)
    s_loc = lax.broadcasted_iota(jnp.int32, (rows, tq), 1)

    def load(ref, blk):
        return ref[pl.ds(pl.multiple_of(blk * tq, tq), tq), :].astype(BF16)

    def update(s, v, m, l, acc):
        m_new = jnp.maximum(m, jnp.max(s, axis=1, keepdims=True))
        p = jnp.exp(s - m_new)
        alpha = jnp.exp(m - m_new)
        m_scr[...] = m_new
        l_scr[...] = alpha * l + jnp.sum(p, axis=1, keepdims=True)
        acc_scr[...] = alpha * acc + _dot(p.astype(BF16), v)

    s = jnp.where(s_loc <= t_loc, _dot_nt(qs, load(k_ref, qi)) + bias_scr[0], NEG)
    update(s, load(v_ref, qi), jnp.full((rows, 1), NEG, F32), jnp.zeros((rows, 1), F32),
           jnp.zeros((rows, HEAD_DIM), F32))

    def body(kb, _):
        pen = jnp.sum(jnp.where(lane == kb, pen_scr[...], 0.0), axis=1, keepdims=True)
        update(_dot_nt(qs, load(k_ref, kb)) + bias_scr[qi - kb] + pen, load(v_ref, kb),
               m_scr[...], l_scr[...], acc_scr[...])
        return 0

    lax.fori_loop(0, qi, body, 0)
    o = _unstack_heads(acc_scr[...] / l_scr[...], GROUP)
    o_ref[...] = (o * _silu(gate_ref[...])).astype(o_ref.dtype)


def moba_attention_prompt(proj, rel_bias, batch, seq):
    tq = MOBA_BLOCK
    assert seq % tq == 0
    nb = seq // tq
    gw = GROUP * HEAD_DIM
    rows = GROUP * tq
    dist = (jnp.arange(nb)[:, None] * tq + (tq - 1) - jnp.arange(2 * tq)[None, :]).reshape(-1)
    table = bias_by_distance(rel_bias, dist).reshape(N_HEADS * nb, 2 * tq)
    return pl.pallas_call(
        functools.partial(_moba_prompt_kernel, tq=tq, n_blocks=nb),
        grid=(N_KV_HEADS, batch, nb),
        in_specs=[pl.BlockSpec((tq, gw), lambda h, b, i: (b * nb + i, h)),
                  pl.BlockSpec((seq, HEAD_DIM), lambda h, b, i: (b, K_COL + h)),
                  pl.BlockSpec((seq, HEAD_DIM), lambda h, b, i: (b, V_COL + h)),
                  pl.BlockSpec((tq, gw), lambda h, b, i: (b * nb + i, GATE_COL + h)),
                  pl.BlockSpec((N_HEADS * nb, 2 * tq), lambda h, b, i: (0, 0))],
        out_specs=pl.BlockSpec((tq, gw), lambda h, b, i: (b * nb + i, h)),
        out_shape=jax.ShapeDtypeStruct((batch * seq, ATTN_WIDTH), BF16),
        scratch_shapes=[pltpu.VMEM((-(-nb // SUBLANES) * SUBLANES, HEAD_DIM), F32),
                        pltpu.VMEM((nb, rows, tq), F32),
                        pltpu.VMEM((rows, LANES), F32),
                        pltpu.VMEM((rows, 1), F32),
                        pltpu.VMEM((rows, 1), F32),
                        pltpu.VMEM((rows, HEAD_DIM), F32)],
        compiler_params=_cparams("arbitrary", "arbitrary", "arbitrary"),
        name="moba_attention_prompt",
    )(proj, proj, proj, proj, table)


PAGES_PER_BLOCK = MOBA_BLOCK // PAGE_SIZE
BLOCKS_PER_STEP = PAGES_PER_STEP // PAGES_PER_BLOCK


def _moba_means_kernel(pt_ref, *refs):
    del pt_ref
    page_refs, o_ref = refs[:-1], refs[-1]
    for blk in range(BLOCKS_PER_STEP):
        pages = page_refs[blk * PAGES_PER_BLOCK:(blk + 1) * PAGES_PER_BLOCK]
        for h in range(N_KV_HEADS):
            total = jnp.sum(_head_pages(pages, h), axis=0, keepdims=True)
            o_ref[0, blk, :, h * HEAD_DIM:(h + 1) * HEAD_DIM] = total * (1.0 / MOBA_BLOCK)


def moba_block_means(k_pool, page_offset, page_table):
    batch, n_pages = page_table.shape
    assert n_pages % PAGES_PER_STEP == 0
    nb = n_pages // PAGES_PER_BLOCK
    return pl.pallas_call(
        _moba_means_kernel,
        grid_spec=pltpu.PrefetchScalarGridSpec(
            num_scalar_prefetch=1,
            grid=(batch, n_pages // PAGES_PER_STEP),
            in_specs=_page_specs(n_pages, page_offset, latest_first=False),
            out_specs=pl.BlockSpec((1, BLOCKS_PER_STEP, 1, KV_WIDTH), lambda b, n, pt: (b, n, 0, 0))),
        out_shape=jax.ShapeDtypeStruct((batch, nb, 1, KV_WIDTH), F32),
        compiler_params=_cparams("parallel", "parallel"),
        name="moba_block_means",
    )(page_table, *([k_pool] * PAGES_PER_STEP))


def _moba_sample_kernel(pt_ref, proj_ref, means_ref, tb_ref, *refs, tq, n_pages):
    del pt_ref
    k_refs, v_refs = refs[:PAGES_PER_STEP], refs[PAGES_PER_STEP:2 * PAGES_PER_STEP]
    o_ref, sel_scr, m_scr, l_scr, acc_scr = refs[2 * PAGES_PER_STEP:]
    p = pl.program_id(1)
    rows = GROUP * tq
    n_past_blocks = n_pages // PAGES_PER_BLOCK
    lane = lax.broadcasted_iota(jnp.int32, (N_KV_HEADS, rows, LANES), 2)

    def q_rows(h):
        return _stack_heads(proj_ref[:, h * GROUP * HEAD_DIM:(h + 1) * GROUP * HEAD_DIM], GROUP)

    def bias_rows(h, page):
        tiles = []
        for g in range(GROUP):
            wrow = tb_ref[pl.ds((h * GROUP + g) * (n_pages + 1) + page, 1), :]
            tiles.append(_toeplitz(wrow, tq, PAGE_SIZE + 1)[:, :PAGE_SIZE])
        return jnp.concatenate(tiles, axis=0)

    def per_head(f):
        return jnp.stack([f(h) for h in range(N_KV_HEADS)])

    def new_rows(col):
        return per_head(lambda h: _pad_rows(proj_ref[:, col + h * HEAD_DIM:col + (h + 1) * HEAD_DIM],
                                            PAGE_SIZE).astype(BF16))

    qs = per_head(lambda h: q_rows(h).astype(BF16))

    @pl.when(p == 0)
    def _():
        t_loc = lax.broadcasted_iota(jnp.int32, (rows, PAGE_SIZE), 0) & (tq - 1)
        s_loc = lax.broadcasted_iota(jnp.int32, (rows, PAGE_SIZE), 1)
        for h in range(N_KV_HEADS):
            means = _pad_rows(means_ref[0, :, h * HEAD_DIM:(h + 1) * HEAD_DIM], LANES)
            sel_scr[h] = _select_topk(_dot_nt(q_rows(h), means, HI), n_past_blocks, n_past_blocks)
        m, l, acc = _moba_tile(qs, new_rows(ATTN_WIDTH), new_rows(ATTN_WIDTH + KV_WIDTH),
                               per_head(lambda h: bias_rows(h, n_pages)), s_loc <= t_loc,
                               jnp.full((N_KV_HEADS, rows, 1), NEG, F32), jnp.zeros((N_KV_HEADS, rows, 1), F32),
                               jnp.zeros((N_KV_HEADS, rows, HEAD_DIM), F32))
        m_scr[...] = m
        l_scr[...] = l
        acc_scr[...] = acc

    sel = sel_scr[...]
    mask = []
    for j in range(BLOCKS_PER_STEP):
        col = jnp.sum(jnp.where(lane == p * BLOCKS_PER_STEP + j, sel, 0.0), axis=-1, keepdims=True) > 0.5
        mask.append(jnp.broadcast_to(col, (N_KV_HEADS, rows, MOBA_BLOCK)))
    bias = per_head(lambda h: jnp.concatenate([bias_rows(h, p * PAGES_PER_STEP + j)
                                               for j in range(PAGES_PER_STEP)], axis=1))
    m, l, acc = _moba_tile(qs, per_head(lambda h: _head_pages(k_refs, h).astype(BF16)),
                           per_head(lambda h: _head_pages(v_refs, h).astype(BF16)), bias,
                           jnp.concatenate(mask, axis=-1), m_scr[...], l_scr[...], acc_scr[...])
    m_scr[...] = m
    l_scr[...] = l
    acc_scr[...] = acc

    @pl.when(p == pl.num_programs(1) - 1)
    def _():
        gc = ATTN_WIDTH + 2 * KV_WIDTH
        for h in range(N_KV_HEADS):
            sl = slice(h * GROUP * HEAD_DIM, (h + 1) * GROUP * HEAD_DIM)
            gate = proj_ref[:, gc + sl.start:gc + sl.stop]
            o_ref[:, sl] = _unstack_heads(acc_scr[h] / l_scr[h], GROUP) * _silu(gate)


def moba_attention_sample(proj, k_pool, v_pool, page_offset, page_table, rel_bias, tq):
    batch, n_pages = page_table.shape
    assert n_pages % PAGES_PER_STEP == 0 and tq <= MOBA_BLOCK
    rows = GROUP * tq
    nb = n_pages // PAGES_PER_BLOCK
    assert nb <= LANES
    means = moba_block_means(k_pool, page_offset, page_table).reshape(batch, nb, KV_WIDTH)
    dist = ((n_pages - jnp.arange(n_pages + 1))[:, None] * PAGE_SIZE + (PAGE_SIZE - 1)
            - jnp.arange(2 * PAGE_SIZE)[None, :]).reshape(-1)
    table = bias_by_distance(rel_bias, dist).reshape(N_HEADS * (n_pages + 1), 2 * PAGE_SIZE)
    specs = _page_specs(n_pages, page_offset, latest_first=False)
    return pl.pallas_call(
        functools.partial(_moba_sample_kernel, tq=tq, n_pages=n_pages),
        grid_spec=pltpu.PrefetchScalarGridSpec(
            num_scalar_prefetch=1,
            grid=(batch, n_pages // PAGES_PER_STEP),
            in_specs=[pl.BlockSpec((tq, ATTN_IN), lambda b, p, pt: (b, 0)),
                      pl.BlockSpec((1, nb, KV_WIDTH), lambda b, p, pt: (b, 0, 0)),
                      pl.BlockSpec((N_HEADS * (n_pages + 1), 2 * PAGE_SIZE), lambda b, p, pt: (0, 0))]
            + specs + specs,
            out_specs=pl.BlockSpec((tq, ATTN_WIDTH), lambda b, p, pt: (b, 0)),
            scratch_shapes=[pltpu.VMEM((N_KV_HEADS, rows, LANES), F32),
                            pltpu.VMEM((N_KV_HEADS, rows, 1), F32),
                            pltpu.VMEM((N_KV_HEADS, rows, 1), F32),
                            pltpu.VMEM((N_KV_HEADS, rows, HEAD_DIM), F32)]),
        out_shape=jax.ShapeDtypeStruct((batch * tq, ATTN_WIDTH), F32),
        compiler_params=_cparams("parallel", "arbitrary"),
        name="moba_attention_sample",
    )(page_table, proj, means, table, *([k_pool] * PAGES_PER_STEP), *([v_pool] * PAGES_PER_STEP))


N_MIXERS = 3
GDN_IN_PADDED = -(-GDN_IN // 512) * 512


def _new_kv(proj, batch, seq):
    k = proj[:, ATTN_WIDTH:ATTN_WIDTH + KV_WIDTH].reshape(batch, seq, N_KV_HEADS, HEAD_DIM)
    v = proj[:, ATTN_WIDTH + KV_WIDTH:ATTN_WIDTH + 2 * KV_WIDTH].reshape(batch, seq, N_KV_HEADS, HEAD_DIM)
    return k, v


def kernel(x_prompt, x_sample, cache_sb_k, cache_sb_v, state_gdn_conv, state_gdn_rec, cache_moba_k, cache_moba_v, page_table, norm_g, sb_w_in, sb_w_out, gdn_w_in, gdn_conv_w, gdn_a_log, gdn_dt_bias, gdn_o_norm, gdn_w_out, moba_w_in, moba_q_norm, moba_k_norm, moba_w_out, rel_bias):
    bp, tp, d = x_prompt.shape
    bs, ts, _ = x_sample.shape
    n_phys = cache_sb_k.shape[1]
    yp = x_prompt.reshape(bp * tp, d)
    ys = x_sample.reshape(bs * ts, d)
    outs = {name: [] for name in ("sb_kp", "sb_vp", "sb_ks", "sb_vs", "gdn_cp", "gdn_sp", "gdn_cs", "gdn_ss",
                                  "mb_kp", "mb_vp", "mb_ks", "mb_vs")}
    for layer in range(norm_g.shape[0]):
        kind = layer % N_MIXERS
        j = layer // N_MIXERS
        g = norm_g[layer]
        if kind == 0:
            w_in = sb_w_in[j].astype(BF16)
            w_out = sb_w_out[j].astype(BF16)
            pp = norm_matmul(yp, g, w_in)
            ps = norm_matmul(ys, g, w_in)
            op = sb_attention_prompt(pp, bp, tp)
            os_ = sb_attention_sample(ps, _flat_pool(cache_sb_k), _flat_pool(cache_sb_v), j * n_phys, page_table, ts)
            kp, vp = _new_kv(pp, bp, tp)
            ks, vs = _new_kv(ps, bs, ts)
            outs["sb_kp"].append(kp); outs["sb_vp"].append(vp); outs["sb_ks"].append(ks); outs["sb_vs"].append(vs)
        elif kind == 1:
            w_in = jnp.pad(gdn_w_in[j], ((0, 0), (0, GDN_IN_PADDED - GDN_IN))).astype(BF16)
            w_out = gdn_w_out[j].astype(BF16)
            pp = norm_matmul(yp, g, w_in)
            ps = norm_matmul(ys, g, w_in)
            conv0 = jnp.zeros((bp, GDN_CONV - 1, GDN_CONV_CH), F32)
            s0 = jnp.zeros((bp,) + state_gdn_rec.shape[2:], F32)
            op, sp = gdn_delta(pp, conv0, gdn_conv_w[j], gdn_a_log[j], gdn_dt_bias[j], gdn_o_norm[j], s0,
                               bp, tp, BF16)
            os_, ss = gdn_delta(ps, state_gdn_conv[j], gdn_conv_w[j], gdn_a_log[j], gdn_dt_bias[j],
                                gdn_o_norm[j], state_gdn_rec[j], bs, ts, F32)
            outs["gdn_cp"].append(pp.reshape(bp, tp, -1)[:, tp - (GDN_CONV - 1):, :GDN_CONV_CH])
            outs["gdn_cs"].append(ps.reshape(bs, ts, -1)[:, ts - (GDN_CONV - 1):, :GDN_CONV_CH])
            outs["gdn_sp"].append(sp); outs["gdn_ss"].append(ss)
        else:
            w_in = moba_w_in[j].astype(BF16)
            w_out = moba_w_out[j].astype(BF16)
            head_gain = jnp.concatenate([jnp.tile(moba_q_norm[j], N_HEADS), jnp.tile(moba_k_norm[j], N_KV_HEADS),
                                         jnp.ones((ATTN_IN - ATTN_WIDTH - KV_WIDTH,), F32)]).reshape(1, ATTN_IN)
            pp = norm_matmul(yp, g, w_in, head_gain, ATTN_WIDTH + KV_WIDTH)
            ps = norm_matmul(ys, g, w_in, head_gain, ATTN_WIDTH + KV_WIDTH)
            op = moba_attention_prompt(pp, rel_bias, bp, tp)
            os_ = moba_attention_sample(ps, _flat_pool(cache_moba_k), _flat_pool(cache_moba_v), j * n_phys,
                                        page_table, rel_bias, ts)
            kp, vp = _new_kv(pp, bp, tp)
            ks, vs = _new_kv(ps, bs, ts)
            outs["mb_kp"].append(kp); outs["mb_vp"].append(vp); outs["mb_ks"].append(ks); outs["mb_vs"].append(vs)
        yp = matmul_residual(op, w_out, yp)
        ys = matmul_residual(os_, w_out, ys)
    stack = lambda name: jnp.stack(outs[name])
    return (yp.reshape(bp, tp, d), ys.reshape(bs, ts, d),
            stack("sb_kp"), stack("sb_vp"), stack("sb_ks"), stack("sb_vs"),
            stack("gdn_cp"), stack("gdn_sp"), stack("gdn_cs"), stack("gdn_ss"),
            stack("mb_kp"), stack("mb_vp"), stack("mb_ks"), stack("mb_vs"))
```

```python
import functools
import math

import jax
import jax.numpy as jnp
from jax import lax
from jax.experimental import pallas as pl
from jax.experimental.pallas import tpu as pltpu

F32 = jnp.float32
BF16 = jnp.bfloat16
HI = lax.Precision.HIGHEST

LANES = 128
SUBLANES = 8
VMEM_LIMIT = 56 * 1024 * 1024

HEAD_DIM = 128
N_HEADS = 16
N_KV_HEADS = 4
GROUP = N_HEADS // N_KV_HEADS
ATTN_WIDTH = N_HEADS * HEAD_DIM
KV_WIDTH = N_KV_HEADS * HEAD_DIM
ATTN_IN = 2 * ATTN_WIDTH + 2 * KV_WIDTH
Q_COL = 0
K_COL = ATTN_WIDTH // HEAD_DIM
V_COL = K_COL + N_KV_HEADS
GATE_COL = (ATTN_WIDTH + 2 * KV_WIDTH) // (GROUP * HEAD_DIM)
PAGE_SIZE = 128

GDN_HEAD_DIM = 128
GDN_K_HEADS = 16
GDN_V_HEADS = 32
GDN_KEY_WIDTH = GDN_K_HEADS * GDN_HEAD_DIM
GDN_VAL_WIDTH = GDN_V_HEADS * GDN_HEAD_DIM
GDN_CONV_CH = 2 * GDN_KEY_WIDTH + GDN_VAL_WIDTH
GDN_CONV = 4
GDN_CHUNK = 64
GDN_IN = GDN_CONV_CH + GDN_VAL_WIDTH + 2 * GDN_V_HEADS
GDN_Z_COL = GDN_CONV_CH // GDN_HEAD_DIM
GDN_BD_COL = (GDN_CONV_CH + GDN_VAL_WIDTH) // LANES

MOBA_BLOCK = 256
MOBA_TOPK = 3
REL_BUCKETS = 32
REL_MAX_DIST = 4096

EPS = 1e-6
NEG = -1e30
SCALE = HEAD_DIM ** -0.5

_NT = (((1,), (1,)), ((), ()))


def _cparams(*sem):
    return pltpu.CompilerParams(dimension_semantics=sem, vmem_limit_bytes=VMEM_LIMIT)


def _softplus(z):
    return jnp.maximum(z, 0.0) + jnp.log1p(jnp.exp(-jnp.abs(z)))


def _silu(x):
    return x * jax.nn.sigmoid(x)


def _dot(a, b, precision=None):
    return jnp.dot(a, b, preferred_element_type=F32, precision=precision)


def _dot_nt(a, b, precision=None):
    return lax.dot_general(a, b, _NT, preferred_element_type=F32, precision=precision)


def _split_bf16(x):
    hi = x.astype(BF16)
    return hi, (x - hi.astype(F32)).astype(BF16)


def _bdot(a, b):
    return jnp.einsum("bij,bjk->bik", a, b, preferred_element_type=F32)


def _bdot_split(a, b):
    return _bdot(a[0], b[0]) + (_bdot(a[0], b[1]) + _bdot(a[1], b[0]))


def _norm_matmul_kernel(x_ref, g_ref, w_ref, hg_ref, o_ref, h_scr, *, n_norm_tiles, tn):
    j = pl.program_id(1)

    @pl.when(j == 0)
    def _():
        x = x_ref[...]
        ms = jnp.mean(x * x, axis=-1, keepdims=True)
        h_scr[...] = (x * lax.rsqrt(ms + EPS) * g_ref[...]).astype(BF16)

    acc = _dot(h_scr[...], w_ref[...])
    if n_norm_tiles == 0:
        o_ref[...] = acc
    else:
        @pl.when(j < n_norm_tiles)
        def _():
            for s in range(tn // HEAD_DIM):
                sl = slice(s * HEAD_DIM, (s + 1) * HEAD_DIM)
                a = acc[:, sl]
                ms = jnp.mean(a * a, axis=-1, keepdims=True)
                o_ref[:, sl] = a * lax.rsqrt(ms + EPS) * hg_ref[:, sl]

        @pl.when(j >= n_norm_tiles)
        def _():
            o_ref[...] = acc


def norm_matmul(x, g, w, head_gain=None, n_norm_cols=0, tn=512):
    m, d = x.shape
    n = w.shape[1]
    tm = min(m, 1024)
    assert m % tm == 0 and n % tn == 0 and n_norm_cols % tn == 0
    if head_gain is None:
        head_gain = jnp.ones((1, n), F32)
    return pl.pallas_call(
        functools.partial(_norm_matmul_kernel, n_norm_tiles=n_norm_cols // tn, tn=tn),
        grid=(m // tm, n // tn),
        in_specs=[pl.BlockSpec((tm, d), lambda i, j: (i, 0)),
                  pl.BlockSpec((1, d), lambda i, j: (0, 0)),
                  pl.BlockSpec((d, tn), lambda i, j: (0, j)),
                  pl.BlockSpec((1, tn), lambda i, j: (0, j))],
        out_specs=pl.BlockSpec((tm, tn), lambda i, j: (i, j)),
        out_shape=jax.ShapeDtypeStruct((m, n), F32),
        scratch_shapes=[pltpu.VMEM((tm, d), BF16)],
        compiler_params=_cparams("parallel", "arbitrary"),
        name="norm_matmul",
    )(x, g.reshape(1, d), w, head_gain)


def _matmul_residual_kernel(a_ref, w_ref, r_ref, o_ref):
    o_ref[...] = r_ref[...] + _dot(a_ref[...].astype(BF16), w_ref[...])


def matmul_residual(a, w, res, tn=512):
    m, k = a.shape
    n = w.shape[1]
    tm = min(m, 1024)
    assert m % tm == 0 and n % tn == 0
    return pl.pallas_call(
        _matmul_residual_kernel,
        grid=(m // tm, n // tn),
        in_specs=[pl.BlockSpec((tm, k), lambda i, j: (i, 0)),
                  pl.BlockSpec((k, tn), lambda i, j: (0, j)),
                  pl.BlockSpec((tm, tn), lambda i, j: (i, j))],
        out_specs=pl.BlockSpec((tm, tn), lambda i, j: (i, j)),
        out_shape=jax.ShapeDtypeStruct((m, n), F32),
        compiler_params=_cparams("parallel", "parallel"),
        name="matmul_residual",
    )(a, w, res)


def _stack_heads(x, n):
    return jnp.concatenate([x[:, g * HEAD_DIM:(g + 1) * HEAD_DIM] for g in range(n)], axis=0)


def _unstack_heads(x, n):
    t = x.shape[0] // n
    return jnp.concatenate([x[g * t:(g + 1) * t] for g in range(n)], axis=1)


def _later_matrix(tk):
    r = lax.broadcasted_iota(jnp.int32, (2 * tk, tk), 0) & (tk - 1)
    c = lax.broadcasted_iota(jnp.int32, (2 * tk, tk), 1)
    return jnp.where(r > c, -1.0, 0.0).astype(BF16)


def _sb_block(qs, k, v, carry, acc, later_mat, mask=None):
    tk = later_mat.shape[1]
    n = k.shape[-2] // tk
    batched = qs.ndim == 3
    z = jnp.einsum("hrd,hsd->hrs", qs, k, preferred_element_type=F32) if batched else _dot_nt(qs, k)
    if mask is not None:
        z = jnp.where(mask, z, NEG)
    sp = jnp.maximum(z, 0.0) + jnp.log(1.0 + jnp.exp(-jnp.abs(z)))
    hi = sp.astype(BF16)
    lo = (sp - hi.astype(F32)).astype(BF16)
    later, total = [], []
    for j in range(n):
        sl = slice(j * tk, (j + 1) * tk)
        split = jnp.concatenate([hi[..., sl], lo[..., sl]], axis=-1)
        lt = _dot(split.reshape(-1, 2 * tk), later_mat).reshape(z.shape[:-1] + (tk,))
        later.append(lt)
        total.append(lt[..., :1] - sp[..., j * tk:j * tk + 1])
    for j in reversed(range(n)):
        later[j] = later[j] + carry
        carry = carry + total[j]
    later = later[0] if n == 1 else jnp.concatenate(later, axis=-1)
    w = jnp.exp(z - sp + later).astype(BF16)
    pv = jnp.einsum("hrs,hsd->hrd", w, v, preferred_element_type=F32) if batched else _dot(w, v)
    return carry, acc + pv


SB_BLOCKS_PER_STEP = 4


def _sb_prompt_kernel(q_ref, k_ref, v_ref, gate_ref, o_ref, carry_scr, acc_scr, *, tq):
    qi = pl.program_id(2)
    rows = GROUP * tq
    nk = SB_BLOCKS_PER_STEP
    qs = (_stack_heads(q_ref[...], GROUP) * SCALE).astype(BF16)
    later_mat = _later_matrix(tq)
    q_pos = qi * tq + (lax.broadcasted_iota(jnp.int32, (rows, 1), 0) & (tq - 1))
    s_off = lax.broadcasted_iota(jnp.int32, (rows, nk * tq), 1)
    carry_scr[...] = jnp.zeros_like(carry_scr)
    acc_scr[...] = jnp.zeros_like(acc_scr)

    def body(i, _):
        top = qi + 1 - nk * i
        start = pl.multiple_of(jnp.maximum(top - nk, 0) * tq, tq)
        k = k_ref[pl.ds(start, nk * tq), :].astype(BF16)
        v = v_ref[pl.ds(start, nk * tq), :].astype(BF16)
        mask = s_off + start < jnp.minimum(top * tq, q_pos)
        c, a = _sb_block(qs, k, v, carry_scr[...], acc_scr[...], later_mat, mask)
        carry_scr[...] = c
        acc_scr[...] = a
        return 0

    lax.fori_loop(0, (qi + nk) // nk, body, 0)
    o = _unstack_heads(acc_scr[...], GROUP)
    o_ref[...] = (o * _silu(gate_ref[...])).astype(o_ref.dtype)


def sb_attention_prompt(proj, batch, seq, tq=128):
    assert seq % tq == 0 and seq >= SB_BLOCKS_PER_STEP * tq
    nq = seq // tq
    gw = GROUP * HEAD_DIM
    return pl.pallas_call(
        functools.partial(_sb_prompt_kernel, tq=tq),
        grid=(batch, N_KV_HEADS, nq),
        in_specs=[pl.BlockSpec((tq, gw), lambda b, h, i: (b * nq + i, h)),
                  pl.BlockSpec((seq, HEAD_DIM), lambda b, h, i: (b, K_COL + h)),
                  pl.BlockSpec((seq, HEAD_DIM), lambda b, h, i: (b, V_COL + h)),
                  pl.BlockSpec((tq, gw), lambda b, h, i: (b * nq + i, GATE_COL + h))],
        out_specs=pl.BlockSpec((tq, gw), lambda b, h, i: (b * nq + i, h)),
        out_shape=jax.ShapeDtypeStruct((batch * seq, ATTN_WIDTH), BF16),
        scratch_shapes=[pltpu.VMEM((GROUP * tq, 1), F32), pltpu.VMEM((GROUP * tq, HEAD_DIM), F32)],
        compiler_params=_cparams("parallel", "parallel", "parallel"),
        name="sb_attention_prompt",
    )(proj, proj, proj, proj)


def _pad_rows(x, rows):
    return jnp.concatenate([x, jnp.zeros((rows - x.shape[0], x.shape[1]), x.dtype)], axis=0)


PAGES_PER_STEP = 8


def _page_specs(n_pages, page_offset, latest_first):
    n_steps = n_pages // PAGES_PER_STEP

    def index_map(b, p, pt, *, j):
        step = n_steps - 1 - p if latest_first else p
        return (page_offset + pt[b, step * PAGES_PER_STEP + j], 0, 0)

    return [pl.BlockSpec((1, PAGE_SIZE * N_KV_HEADS, HEAD_DIM), functools.partial(index_map, j=j))
            for j in range(PAGES_PER_STEP)]


def _head_pages(page_refs, h):
    return jnp.concatenate([r[0, pl.ds(h, PAGE_SIZE, stride=N_KV_HEADS), :] for r in page_refs], axis=0)


def _flat_pool(cache):
    return cache.reshape(cache.shape[0] * cache.shape[1], PAGE_SIZE * N_KV_HEADS, HEAD_DIM)


def _sb_sample_kernel(pt_ref, proj_ref, *refs, tq):
    del pt_ref
    k_refs, v_refs = refs[:PAGES_PER_STEP], refs[PAGES_PER_STEP:2 * PAGES_PER_STEP]
    o_ref, carry_scr, acc_scr = refs[2 * PAGES_PER_STEP:]
    p = pl.program_id(1)
    rows = GROUP * tq
    later_mat = _later_matrix(PAGE_SIZE)

    def per_head(f):
        return jnp.stack([f(h) for h in range(N_KV_HEADS)])

    def q_rows(h):
        q = proj_ref[:, h * GROUP * HEAD_DIM:(h + 1) * GROUP * HEAD_DIM]
        return (_stack_heads(q, GROUP) * SCALE).astype(BF16)

    def new_rows(col):
        return per_head(lambda h: _pad_rows(proj_ref[:, col + h * HEAD_DIM:col + (h + 1) * HEAD_DIM],
                                            PAGE_SIZE).astype(BF16))

    qs = per_head(q_rows)

    @pl.when(p == 0)
    def _():
        t_loc = lax.broadcasted_iota(jnp.int32, (rows, PAGE_SIZE), 0) & (tq - 1)
        s_loc = lax.broadcasted_iota(jnp.int32, (rows, PAGE_SIZE), 1)
        c, a = _sb_block(qs, new_rows(ATTN_WIDTH), new_rows(ATTN_WIDTH + KV_WIDTH),
                         jnp.zeros((N_KV_HEADS, rows, 1), F32), jnp.zeros((N_KV_HEADS, rows, HEAD_DIM), F32),
                         later_mat, s_loc < t_loc)
        carry_scr[...] = c
        acc_scr[...] = a

    c, a = _sb_block(qs, per_head(lambda h: _head_pages(k_refs, h).astype(BF16)),
                     per_head(lambda h: _head_pages(v_refs, h).astype(BF16)),
                     carry_scr[...], acc_scr[...], later_mat)
    carry_scr[...] = c
    acc_scr[...] = a

    @pl.when(p == pl.num_programs(1) - 1)
    def _():
        gc = ATTN_WIDTH + 2 * KV_WIDTH
        for h in range(N_KV_HEADS):
            sl = slice(h * GROUP * HEAD_DIM, (h + 1) * GROUP * HEAD_DIM)
            gate = proj_ref[:, gc + sl.start:gc + sl.stop]
            o_ref[:, sl] = _unstack_heads(acc_scr[h], GROUP) * _silu(gate)


def sb_attention_sample(proj, k_pool, v_pool, page_offset, page_table, tq):
    batch, n_pages = page_table.shape
    assert n_pages % PAGES_PER_STEP == 0
    rows = GROUP * tq
    specs = _page_specs(n_pages, page_offset, latest_first=True)
    return pl.pallas_call(
        functools.partial(_sb_sample_kernel, tq=tq),
        grid_spec=pltpu.PrefetchScalarGridSpec(
            num_scalar_prefetch=1,
            grid=(batch, n_pages // PAGES_PER_STEP),
            in_specs=[pl.BlockSpec((tq, ATTN_IN), lambda b, p, pt: (b, 0))] + specs + specs,
            out_specs=pl.BlockSpec((tq, ATTN_WIDTH), lambda b, p, pt: (b, 0)),
            scratch_shapes=[pltpu.VMEM((N_KV_HEADS, rows, 1), F32),
                            pltpu.VMEM((N_KV_HEADS, rows, HEAD_DIM), F32)]),
        out_shape=jax.ShapeDtypeStruct((batch * tq, ATTN_WIDTH), F32),
        compiler_params=_cparams("parallel", "arbitrary"),
        name="sb_attention_sample",
    )(page_table, proj, *([k_pool] * PAGES_PER_STEP), *([v_pool] * PAGES_PER_STEP))


GDN_ROWS = 128
GDN_REP = GDN_V_HEADS // GDN_K_HEADS


def _gdn_delta_kernel(q_ref, k_ref, v_ref, bd_ref, z_ref, cq_ref, ck_ref, cv_ref, wq_ref, wk_ref, wv_ref,
                      alog_ref, dt_ref, onorm_ref, s0_ref,
                      o_ref, s_ref, u_scr, w_scr, in_scr, qd_scr, kdt_scr, gl_scr, gt_scr, *, seq):
    hk = pl.program_id(1)
    c = GDN_ROWS
    n_chunks = max(seq // c, 1)
    unroll = gt_scr.shape[0]
    n_doublings = max(math.ceil(math.log2(min(seq, c))) - 1, 0)
    ri = lax.broadcasted_iota(jnp.int32, (c, c), 0)
    ci = lax.broadcasted_iota(jnp.int32, (c, c), 1)
    lane = lax.broadcasted_iota(jnp.int32, (c, LANES), 1)
    row = lax.broadcasted_iota(jnp.int32, (c, 1), 0)
    tril = ri >= ci
    tril_b = jnp.where(tril, 1.0, 0.0).astype(BF16)
    neg_a = -jnp.exp(alog_ref[...])

    def chunk_rows(i):
        return pl.ds(i * c, c) if isinstance(i, int) else pl.ds(pl.multiple_of(i * c, c), c)

    def rows(ref, i):
        if seq < c:
            return _pad_rows(ref[...], c)
        return ref[chunk_rows(i), :]

    def conv_silu(ref, prev_ref, w_ref, i):
        n = min(seq, c)
        if isinstance(i, int) and i == 0:
            ext = jnp.concatenate([prev_ref[0], ref[pl.ds(0, n), :]], axis=0)
        elif isinstance(i, int):
            ext = ref[pl.ds(i * c - SUBLANES, c + SUBLANES), :]
        else:
            ext = ref[pl.ds(pl.multiple_of(i * c - SUBLANES, SUBLANES), c + SUBLANES), :]
        w = w_ref[...]
        y = ext[SUBLANES:] * w[GDN_CONV - 1:GDN_CONV, :]
        for tap in range(1, GDN_CONV):
            y = y + pltpu.roll(ext, tap, 0)[SUBLANES:] * w[GDN_CONV - 1 - tap:GDN_CONV - tap, :]
        y = _silu(y)
        return y if n == c else _pad_rows(y, c)

    def l2_normalised(x, scale):
        return x * (lax.rsqrt(jnp.sum(x * x, axis=-1, keepdims=True) + EPS) * scale)

    def prepare(i, _):
        ms, rhss = [], []
        for j in range(unroll):
            ch = i * unroll + j
            sl = chunk_rows(ch)
            q = l2_normalised(conv_silu(q_ref, cq_ref, wq_ref, ch), GDN_HEAD_DIM ** -0.5)
            k = l2_normalised(conv_silu(k_ref, ck_ref, wk_ref, ch), 1.0)
            v2 = conv_silu(v_ref, cv_ref, wv_ref, ch)
            raw = rows(bd_ref, ch)
            k16 = k.astype(BF16)
            kk = _dot_nt(k16, k16)
            qk = _dot_nt(q.astype(BF16), k16)
            sig = jax.nn.sigmoid(raw)
            g_all = jnp.where(row < seq, neg_a * _softplus(raw + dt_ref[...]), 0.0)
            g_hi = g_all.astype(BF16)
            g_mid, g_lo = _split_bf16(g_all - g_hi.astype(F32))
            gcx = _dot(tril_b, g_hi) + (_dot(tril_b, g_mid) + _dot(tril_b, g_lo))
            gt_scr[j] = gcx.T
            for e in range(GDN_REP):
                hv = hk * GDN_REP + e
                beta = jnp.sum(jnp.where(lane == hv, sig, 0.0), axis=1, keepdims=True)
                gc = jnp.sum(jnp.where(lane == hv + GDN_V_HEADS, gcx, 0.0), axis=1, keepdims=True)
                g_row = gt_scr[j, pl.ds(hv + GDN_V_HEADS, 1), :]
                decay = jnp.exp(jnp.where(tril, gc - g_row, NEG))
                ms.append(-jnp.where(ri > ci, kk * beta * decay, 0.0))
                egc = jnp.exp(gc)
                v = v2[:, e * GDN_HEAD_DIM:(e + 1) * GDN_HEAD_DIM]
                rhss.append(jnp.concatenate([v * beta, k * (beta * egc)], axis=1))
                g_last = gc[c - 1:c, :]
                in_scr[e, sl, :] = (qk * decay).astype(BF16)
                qd_scr[e, sl, :] = (q * egc).astype(BF16)
                kdt_scr[e, ch] = (k * jnp.exp(g_last - gc)).T.astype(BF16)
                gl_scr[e, ch] = jnp.broadcast_to(jnp.exp(g_last), (SUBLANES, LANES))
        nmat = jnp.stack(ms)
        pw = _split_bf16(nmat)
        for _ in range(n_doublings):
            sq = _bdot_split(pw, pw)
            pw = _split_bf16(sq)
            nmat = nmat + sq + _bdot_split(_split_bf16(nmat), pw)
        rhs = jnp.stack(rhss)
        sol = rhs + _bdot(nmat.astype(BF16), rhs.astype(BF16))
        for j in range(unroll):
            sl = chunk_rows(i * unroll + j)
            for e in range(GDN_REP):
                x = sol[j * GDN_REP + e]
                u_scr[e, sl, :] = x[:, :GDN_HEAD_DIM]
                w_scr[e, sl, :] = x[:, GDN_HEAD_DIM:].astype(BF16)
        return 0

    def advance(i, states):
        sl = chunk_rows(i)
        z2 = rows(z_ref, i)
        out = []
        for e in range(GDN_REP):
            s = states[e]
            s16 = s.astype(BF16)
            v_new = u_scr[e, sl, :] - _dot(w_scr[e, sl, :], s16)
            v16 = v_new.astype(BF16)
            o = _dot(qd_scr[e, sl, :], s16) + _dot(in_scr[e, sl, :], v16)
            out.append(s * gl_scr[e, i][0:1, :] + _dot(kdt_scr[e, i], v16))
            ms = jnp.mean(o * o, axis=-1, keepdims=True)
            z = z2[:, e * GDN_HEAD_DIM:(e + 1) * GDN_HEAD_DIM]
            o = (o * lax.rsqrt(ms + EPS) * onorm_ref[...] * _silu(z)).astype(o_ref.dtype)
            cols = slice(e * GDN_HEAD_DIM, (e + 1) * GDN_HEAD_DIM)
            if seq < c:
                o_ref[:, cols] = o[:seq]
            else:
                o_ref[sl, cols] = o
        return tuple(out)

    states = tuple(s0_ref[0, e] for e in range(GDN_REP))
    if n_chunks == 1:
        prepare(0, 0)
        states = advance(0, states)
    else:
        n_groups = n_chunks // unroll
        prepare(0, 0)

        def body(i, states):
            for j in range(unroll):
                states = advance((i - 1) * unroll + j, states)
            prepare(i, 0)
            return states

        states = lax.fori_loop(1, n_groups, body, states)
        for j in range(unroll):
            states = advance((n_groups - 1) * unroll + j, states)
    for e in range(GDN_REP):
        s_ref[0, e] = states[e]


def gdn_delta(proj, conv_state, conv_w, a_log, dt_bias, o_norm, s0, batch, seq, out_dtype):
    assert (seq % GDN_ROWS == 0 or seq < GDN_ROWS) and seq % SUBLANES == 0
    prev = jnp.pad(conv_state, ((0, 0), (SUBLANES - (GDN_CONV - 1), 0), (0, 0)))
    vcol = 2 * GDN_K_HEADS // GDN_REP
    pad = jnp.zeros((GDN_V_HEADS,), F32)
    tail = jnp.zeros((LANES - 2 * GDN_V_HEADS,), F32)
    alog = jnp.concatenate([pad, a_log, tail]).reshape(1, LANES)
    dt = jnp.concatenate([pad, dt_bias, tail]).reshape(1, LANES)
    blk = (seq, GDN_HEAD_DIM)
    wide = (seq, GDN_REP * GDN_HEAD_DIM)
    state_blk = (1, GDN_REP, GDN_HEAD_DIM, GDN_HEAD_DIM)
    rows = max(seq, GDN_ROWS)
    n_chunks = rows // GDN_ROWS
    return pl.pallas_call(
        functools.partial(_gdn_delta_kernel, seq=seq),
        grid=(batch, GDN_K_HEADS),
        in_specs=[pl.BlockSpec(blk, lambda b, h: (b, h)),
                  pl.BlockSpec(blk, lambda b, h: (b, GDN_K_HEADS + h)),
                  pl.BlockSpec(wide, lambda b, h: (b, vcol + h)),
                  pl.BlockSpec((seq, LANES), lambda b, h: (b, GDN_BD_COL)),
                  pl.BlockSpec(wide, lambda b, h: (b, GDN_Z_COL // GDN_REP + h)),
                  pl.BlockSpec((1, SUBLANES, blk[1]), lambda b, h: (b, 0, h)),
                  pl.BlockSpec((1, SUBLANES, blk[1]), lambda b, h: (b, 0, GDN_K_HEADS + h)),
                  pl.BlockSpec((1, SUBLANES, wide[1]), lambda b, h: (b, 0, vcol + h)),
                  pl.BlockSpec((GDN_CONV, blk[1]), lambda b, h: (0, h)),
                  pl.BlockSpec((GDN_CONV, blk[1]), lambda b, h: (0, GDN_K_HEADS + h)),
                  pl.BlockSpec((GDN_CONV, wide[1]), lambda b, h: (0, vcol + h)),
                  pl.BlockSpec((1, LANES), lambda b, h: (0, 0)),
                  pl.BlockSpec((1, LANES), lambda b, h: (0, 0)),
                  pl.BlockSpec((1, GDN_HEAD_DIM), lambda b, h: (0, 0)),
                  pl.BlockSpec(state_blk, lambda b, h: (b, h, 0, 0))],
        out_specs=[pl.BlockSpec(wide, lambda b, h: (b, h)),
                   pl.BlockSpec(state_blk, lambda b, h: (b, h, 0, 0))],
        scratch_shapes=[pltpu.VMEM((GDN_REP, rows, GDN_HEAD_DIM), F32),
                        pltpu.VMEM((GDN_REP, rows, GDN_HEAD_DIM), BF16),
                        pltpu.VMEM((GDN_REP, rows, GDN_ROWS), BF16),
                        pltpu.VMEM((GDN_REP, rows, GDN_HEAD_DIM), BF16),
                        pltpu.VMEM((GDN_REP, n_chunks, GDN_HEAD_DIM, GDN_ROWS), BF16),
                        pltpu.VMEM((GDN_REP, n_chunks, SUBLANES, LANES), F32),
                        pltpu.VMEM((2 if n_chunks % 2 == 0 else 1, GDN_ROWS, LANES), F32)],
        out_shape=[jax.ShapeDtypeStruct((batch * seq, GDN_VAL_WIDTH), out_dtype),
                   jax.ShapeDtypeStruct((batch, GDN_V_HEADS, GDN_HEAD_DIM, GDN_HEAD_DIM), F32)],
        compiler_params=_cparams("parallel", "parallel"),
        name="gdn_delta",
    )(proj, proj, proj, proj, proj, prev, prev, prev, conv_w, conv_w, conv_w,
      alog, dt, o_norm.reshape(1, GDN_HEAD_DIM), s0)


def _rel_bucket(dist):
    max_exact = REL_BUCKETS // 2
    n = jnp.maximum(dist, 0)
    large = max_exact + (jnp.log(jnp.maximum(n, 1).astype(F32) / max_exact)
                         / math.log(REL_MAX_DIST / max_exact) * (REL_BUCKETS - max_exact)).astype(jnp.int32)
    large = jnp.minimum(large, REL_BUCKETS - 1)
    return jnp.where(n < max_exact, n, large)


def _bias_table_kernel(rbt_ref, onehot_ref, o_ref):
    o_ref[...] = _dot(rbt_ref[...], onehot_ref[...], HI)


def bias_by_distance(rel_bias, dist):
    bucket = _rel_bucket(dist)
    onehot = (bucket[None, :] == jnp.arange(REL_BUCKETS)[:, None]) & (dist[None, :] >= 0)
    return pl.pallas_call(
        _bias_table_kernel,
        out_shape=jax.ShapeDtypeStruct((N_HEADS, dist.shape[0]), F32),
        compiler_params=pltpu.CompilerParams(vmem_limit_bytes=VMEM_LIMIT),
        name="bias_table",
    )(rel_bias.T, onehot.astype(F32))


def _toeplitz(window_row, rows, shift):
    x = jnp.broadcast_to(window_row, (rows, window_row.shape[1]))
    return pltpu.roll(x, shift, 1, stride=1, stride_axis=0)


def _select_topk(gate, n_valid, n_blocks):
    lane = lax.broadcasted_iota(jnp.int32, gate.shape, 1)
    valid = lane < n_valid
    gm = jnp.where(valid, gate, -jnp.inf)
    cnt = jnp.zeros(gate.shape, jnp.int32)
    for m in range(n_blocks):
        col = gm[:, m:m + 1]
        beats = (col > gm) | ((col == gm) & (lane > m))
        cnt = cnt + jnp.where(beats, 1, 0)
    return jnp.where(valid & (cnt < MOBA_TOPK), 1.0, 0.0)


def _select_topk_t(gate_t, n_valid):
    blk = lax.broadcasted_iota(jnp.int32, gate_t.shape, 0)
    valid = blk < n_valid
    gm = jnp.where(valid, gate_t, -jnp.inf)
    cnt = jnp.zeros(gate_t.shape, jnp.int32)
    for m in range(gate_t.shape[0]):
        row = gm[m:m + 1, :]
        beats = (row > gm) | ((row == gm) & (blk > m))
        cnt = cnt + jnp.where(beats, 1, 0)
    return jnp.where(valid & (cnt < MOBA_TOPK), 1.0, 0.0)


def _moba_tile(qs, k, v, bias, mask, m, l, acc):
    batched = qs.ndim == 3
    s = jnp.einsum("hrd,hsd->hrs", qs, k, preferred_element_type=F32) if batched else _dot_nt(qs, k)
    s = jnp.where(mask, s * SCALE + bias, NEG)
    m_new = jnp.maximum(m, jnp.max(s, axis=-1, keepdims=True))
    p = jnp.where(mask, jnp.exp(s - m_new), 0.0)
    alpha = jnp.exp(m - m_new)
    l = alpha * l + jnp.sum(p, axis=-1, keepdims=True)
    p = p.astype(BF16)
    pv = jnp.einsum("hrs,hsd->hrd", p, v, preferred_element_type=F32) if batched else _dot(p, v)
    return m_new, l, alpha * acc + pv


def _moba_prompt_kernel(q_ref, k_ref, v_ref, gate_ref, tb_ref, o_ref,
                        means_scr, bias_scr, pen_scr, m_scr, l_scr, acc_scr, *, tq, n_blocks):
    h = pl.program_id(0)
    b = pl.program_id(1)
    qi = pl.program_id(2)
    rows = GROUP * tq

    @pl.when(qi == 0)
    def _():
        means_scr[...] = jnp.zeros_like(means_scr)
        means_scr[0:n_blocks, :] = jnp.mean(k_ref[...].reshape(n_blocks, MOBA_BLOCK, HEAD_DIM), axis=1)

    @pl.when(b == 0)
    def _():
        for g in range(GROUP):
            wrow = tb_ref[pl.ds((h * GROUP + g) * n_blocks + qi, 1), :]
            bias_scr[qi, pl.ds(g * tq, tq), :] = _toeplitz(wrow, tq, tq + 1)[:, :tq]

    qs32 = _stack_heads(q_ref[...], GROUP)
    qs = (qs32 * SCALE).astype(BF16)
    sel_t = _select_topk_t(_dot_nt(means_scr[...], qs32, HI), qi)
    pen_t = (sel_t - 1.0) * -NEG
    pen_scr[...] = jnp.concatenate([pen_t, jnp.zeros((LANES - pen_t.shape[0], rows), F32)], axis=0).T
    lane = lax.broadcasted_iota(jnp.int32, (rows, LANES), 1)
    t_loc = lax.broadcasted_iota(jnp.int32, (rows, tq), 0) & (tq - 1)
    s_loc = lax.broadcasted_iota(jnp.int32, (rows, tq), 1)

    def load(ref, blk):
        return ref[pl.ds(pl.multiple_of(blk * tq, tq), tq), :].astype(BF16)

    def update(s, v, m, l, acc):
        m_new = jnp.maximum(m, jnp.max(s, axis=1, keepdims=True))
        p = jnp.exp(s - m_new)
        alpha = jnp.exp(m - m_new)
        m_scr[...] = m_new
        l_scr[...] = alpha * l + jnp.sum(p, axis=1, keepdims=True)
        acc_scr[...] = alpha * acc + _dot(p.astype(BF16), v)

    s = jnp.where(s_loc <= t_loc, _dot_nt(qs, load(k_ref, qi)) + bias_scr[0], NEG)
    update(s, load(v_ref, qi), jnp.full((rows, 1), NEG, F32), jnp.zeros((rows, 1), F32),
           jnp.zeros((rows, HEAD_DIM), F32))

    def penalty(kb):
        return jnp.sum(jnp.where(lane == kb, pen_scr[...], 0.0), axis=1, keepdims=True)

    def pair(i, _):
        kb = 2 * i
        start = pl.multiple_of(kb * tq, tq)
        z = _dot_nt(qs, k_ref[pl.ds(start, 2 * tq), :].astype(BF16))
        s = jnp.concatenate([z[:, :tq] + bias_scr[qi - kb] + penalty(kb),
                             z[:, tq:] + bias_scr[qi - kb - 1] + penalty(kb + 1)], axis=1)
        update(s, v_ref[pl.ds(start, 2 * tq), :].astype(BF16), m_scr[...], l_scr[...], acc_scr[...])
        return 0

    lax.fori_loop(0, qi // 2, pair, 0)

    @pl.when(qi % 2 == 1)
    def _():
        kb = qi - 1
        update(_dot_nt(qs, load(k_ref, kb)) + bias_scr[1] + penalty(kb), load(v_ref, kb),
               m_scr[...], l_scr[...], acc_scr[...])
    o = _unstack_heads(acc_scr[...] / l_scr[...], GROUP)
    o_ref[...] = (o * _silu(gate_ref[...])).astype(o_ref.dtype)


def moba_attention_prompt(proj, rel_bias, batch, seq):
    tq = MOBA_BLOCK
    assert seq % tq == 0
    nb = seq // tq
    gw = GROUP * HEAD_DIM
    rows = GROUP * tq
    dist = (jnp.arange(nb)[:, None] * tq + (tq - 1) - jnp.arange(2 * tq)[None, :]).reshape(-1)
    table = bias_by_distance(rel_bias, dist).reshape(N_HEADS * nb, 2 * tq)
    return pl.pallas_call(
        functools.partial(_moba_prompt_kernel, tq=tq, n_blocks=nb),
        grid=(N_KV_HEADS, batch, nb),
        in_specs=[pl.BlockSpec((tq, gw), lambda h, b, i: (b * nb + i, h)),
                  pl.BlockSpec((seq, HEAD_DIM), lambda h, b, i: (b, K_COL + h)),
                  pl.BlockSpec((seq, HEAD_DIM), lambda h, b, i: (b, V_COL + h)),
                  pl.BlockSpec((tq, gw), lambda h, b, i: (b * nb + i, GATE_COL + h)),
                  pl.BlockSpec((N_HEADS * nb, 2 * tq), lambda h, b, i: (0, 0))],
        out_specs=pl.BlockSpec((tq, gw), lambda h, b, i: (b * nb + i, h)),
        out_shape=jax.ShapeDtypeStruct((batch * seq, ATTN_WIDTH), BF16),
        scratch_shapes=[pltpu.VMEM((-(-nb // SUBLANES) * SUBLANES, HEAD_DIM), F32),
                        pltpu.VMEM((nb, rows, tq), F32),
                        pltpu.VMEM((rows, LANES), F32),
                        pltpu.VMEM((rows, 1), F32),
                        pltpu.VMEM((rows, 1), F32),
                        pltpu.VMEM((rows, HEAD_DIM), F32)],
        compiler_params=_cparams("arbitrary", "arbitrary", "arbitrary"),
        name="moba_attention_prompt",
    )(proj, proj, proj, proj, table)


PAGES_PER_BLOCK = MOBA_BLOCK // PAGE_SIZE
BLOCKS_PER_STEP = PAGES_PER_STEP // PAGES_PER_BLOCK


def _moba_means_kernel(pt_ref, *refs):
    del pt_ref
    page_refs, o_ref = refs[:-1], refs[-1]
    for blk in range(BLOCKS_PER_STEP):
        pages = page_refs[blk * PAGES_PER_BLOCK:(blk + 1) * PAGES_PER_BLOCK]
        for h in range(N_KV_HEADS):
            total = jnp.sum(_head_pages(pages, h), axis=0, keepdims=True)
            o_ref[0, blk, :, h * HEAD_DIM:(h + 1) * HEAD_DIM] = total * (1.0 / MOBA_BLOCK)


def moba_block_means(k_pool, page_offset, page_table):
    batch, n_pages = page_table.shape
    assert n_pages % PAGES_PER_STEP == 0
    nb = n_pages // PAGES_PER_BLOCK
    return pl.pallas_call(
        _moba_means_kernel,
        grid_spec=pltpu.PrefetchScalarGridSpec(
            num_scalar_prefetch=1,
            grid=(batch, n_pages // PAGES_PER_STEP),
            in_specs=_page_specs(n_pages, page_offset, latest_first=False),
            out_specs=pl.BlockSpec((1, BLOCKS_PER_STEP, 1, KV_WIDTH), lambda b, n, pt: (b, n, 0, 0))),
        out_shape=jax.ShapeDtypeStruct((batch, nb, 1, KV_WIDTH), F32),
        compiler_params=_cparams("parallel", "parallel"),
        name="moba_block_means",
    )(page_table, *([k_pool] * PAGES_PER_STEP))


def _moba_sample_kernel(pt_ref, proj_ref, means_ref, tb_ref, *refs, tq, n_pages):
    del pt_ref
    k_refs, v_refs = refs[:PAGES_PER_STEP], refs[PAGES_PER_STEP:2 * PAGES_PER_STEP]
    o_ref, sel_scr, m_scr, l_scr, acc_scr = refs[2 * PAGES_PER_STEP:]
    p = pl.program_id(1)
    rows = GROUP * tq
    n_past_blocks = n_pages // PAGES_PER_BLOCK
    lane = lax.broadcasted_iota(jnp.int32, (N_KV_HEADS, rows, LANES), 2)

    def q_rows(h):
        return _stack_heads(proj_ref[:, h * GROUP * HEAD_DIM:(h + 1) * GROUP * HEAD_DIM], GROUP)

    def bias_rows(h, page):
        tiles = []
        for g in range(GROUP):
            wrow = tb_ref[pl.ds((h * GROUP + g) * (n_pages + 1) + page, 1), :]
            tiles.append(_toeplitz(wrow, tq, PAGE_SIZE + 1)[:, :PAGE_SIZE])
        return jnp.concatenate(tiles, axis=0)

    def per_head(f):
        return jnp.stack([f(h) for h in range(N_KV_HEADS)])

    def new_rows(col):
        return per_head(lambda h: _pad_rows(proj_ref[:, col + h * HEAD_DIM:col + (h + 1) * HEAD_DIM],
                                            PAGE_SIZE).astype(BF16))

    qs = per_head(lambda h: q_rows(h).astype(BF16))

    @pl.when(p == 0)
    def _():
        t_loc = lax.broadcasted_iota(jnp.int32, (rows, PAGE_SIZE), 0) & (tq - 1)
        s_loc = lax.broadcasted_iota(jnp.int32, (rows, PAGE_SIZE), 1)
        for h in range(N_KV_HEADS):
            means = _pad_rows(means_ref[0, :, h * HEAD_DIM:(h + 1) * HEAD_DIM], LANES)
            sel_scr[h] = _select_topk(_dot_nt(q_rows(h), means, HI), n_past_blocks, n_past_blocks)
        m, l, acc = _moba_tile(qs, new_rows(ATTN_WIDTH), new_rows(ATTN_WIDTH + KV_WIDTH),
                               per_head(lambda h: bias_rows(h, n_pages)), s_loc <= t_loc,
                               jnp.full((N_KV_HEADS, rows, 1), NEG, F32), jnp.zeros((N_KV_HEADS, rows, 1), F32),
                               jnp.zeros((N_KV_HEADS, rows, HEAD_DIM), F32))
        m_scr[...] = m
        l_scr[...] = l
        acc_scr[...] = acc

    sel = sel_scr[...]
    mask = []
    for j in range(BLOCKS_PER_STEP):
        col = jnp.sum(jnp.where(lane == p * BLOCKS_PER_STEP + j, sel, 0.0), axis=-1, keepdims=True) > 0.5
        mask.append(jnp.broadcast_to(col, (N_KV_HEADS, rows, MOBA_BLOCK)))
    bias = per_head(lambda h: jnp.concatenate([bias_rows(h, p * PAGES_PER_STEP + j)
                                               for j in range(PAGES_PER_STEP)], axis=1))
    m, l, acc = _moba_tile(qs, per_head(lambda h: _head_pages(k_refs, h).astype(BF16)),
                           per_head(lambda h: _head_pages(v_refs, h).astype(BF16)), bias,
                           jnp.concatenate(mask, axis=-1), m_scr[...], l_scr[...], acc_scr[...])
    m_scr[...] = m
    l_scr[...] = l
    acc_scr[...] = acc

    @pl.when(p == pl.num_programs(1) - 1)
    def _():
        gc = ATTN_WIDTH + 2 * KV_WIDTH
        for h in range(N_KV_HEADS):
            sl = slice(h * GROUP * HEAD_DIM, (h + 1) * GROUP * HEAD_DIM)
            gate = proj_ref[:, gc + sl.start:gc + sl.stop]
            o_ref[:, sl] = _unstack_heads(acc_scr[h] / l_scr[h], GROUP) * _silu(gate)


def moba_attention_sample(proj, k_pool, v_pool, page_offset, page_table, rel_bias, tq):
    batch, n_pages = page_table.shape
    assert n_pages % PAGES_PER_STEP == 0 and tq <= MOBA_BLOCK
    rows = GROUP * tq
    nb = n_pages // PAGES_PER_BLOCK
    assert nb <= LANES
    means = moba_block_means(k_pool, page_offset, page_table).reshape(batch, nb, KV_WIDTH)
    dist = ((n_pages - jnp.arange(n_pages + 1))[:, None] * PAGE_SIZE + (PAGE_SIZE - 1)
            - jnp.arange(2 * PAGE_SIZE)[None, :]).reshape(-1)
    table = bias_by_distance(rel_bias, dist).reshape(N_HEADS * (n_pages + 1), 2 * PAGE_SIZE)
    specs = _page_specs(n_pages, page_offset, latest_first=False)
    return pl.pallas_call(
        functools.partial(_moba_sample_kernel, tq=tq, n_pages=n_pages),
        grid_spec=pltpu.PrefetchScalarGridSpec(
            num_scalar_prefetch=1,
            grid=(batch, n_pages // PAGES_PER_STEP),
            in_specs=[pl.BlockSpec((tq, ATTN_IN), lambda b, p, pt: (b, 0)),
                      pl.BlockSpec((1, nb, KV_WIDTH), lambda b, p, pt: (b, 0, 0)),
                      pl.BlockSpec((N_HEADS * (n_pages + 1), 2 * PAGE_SIZE), lambda b, p, pt: (0, 0))]
            + specs + specs,
            out_specs=pl.BlockSpec((tq, ATTN_WIDTH), lambda b, p, pt: (b, 0)),
            scratch_shapes=[pltpu.VMEM((N_KV_HEADS, rows, LANES), F32),
                            pltpu.VMEM((N_KV_HEADS, rows, 1), F32),
                            pltpu.VMEM((N_KV_HEADS, rows, 1), F32),
                            pltpu.VMEM((N_KV_HEADS, rows, HEAD_DIM), F32)]),
        out_shape=jax.ShapeDtypeStruct((batch * tq, ATTN_WIDTH), F32),
        compiler_params=_cparams("parallel", "arbitrary"),
        name="moba_attention_sample",
    )(page_table, proj, means, table, *([k_pool] * PAGES_PER_STEP), *([v_pool] * PAGES_PER_STEP))


N_MIXERS = 3
GDN_IN_PADDED = -(-GDN_IN // 512) * 512


def _new_kv(proj, batch, seq):
    k = proj[:, ATTN_WIDTH:ATTN_WIDTH + KV_WIDTH].reshape(batch, seq, N_KV_HEADS, HEAD_DIM)
    v = proj[:, ATTN_WIDTH + KV_WIDTH:ATTN_WIDTH + 2 * KV_WIDTH].reshape(batch, seq, N_KV_HEADS, HEAD_DIM)
    return k, v


def kernel(x_prompt, x_sample, cache_sb_k, cache_sb_v, state_gdn_conv, state_gdn_rec, cache_moba_k, cache_moba_v, page_table, norm_g, sb_w_in, sb_w_out, gdn_w_in, gdn_conv_w, gdn_a_log, gdn_dt_bias, gdn_o_norm, gdn_w_out, moba_w_in, moba_q_norm, moba_k_norm, moba_w_out, rel_bias):
    bp, tp, d = x_prompt.shape
    bs, ts, _ = x_sample.shape
    n_phys = cache_sb_k.shape[1]
    yp = x_prompt.reshape(bp * tp, d)
    ys = x_sample.reshape(bs * ts, d)
    outs = {name: [] for name in ("sb_kp", "sb_vp", "sb_ks", "sb_vs", "gdn_cp", "gdn_sp", "gdn_cs", "gdn_ss",
                                  "mb_kp", "mb_vp", "mb_ks", "mb_vs")}
    for layer in range(norm_g.shape[0]):
        kind = layer % N_MIXERS
        j = layer // N_MIXERS
        g = norm_g[layer]
        if kind == 0:
            w_in = sb_w_in[j].astype(BF16)
            w_out = sb_w_out[j].astype(BF16)
            pp = norm_matmul(yp, g, w_in)
            ps = norm_matmul(ys, g, w_in)
            op = sb_attention_prompt(pp, bp, tp)
            os_ = sb_attention_sample(ps, _flat_pool(cache_sb_k), _flat_pool(cache_sb_v), j * n_phys, page_table, ts)
            kp, vp = _new_kv(pp, bp, tp)
            ks, vs = _new_kv(ps, bs, ts)
            outs["sb_kp"].append(kp); outs["sb_vp"].append(vp); outs["sb_ks"].append(ks); outs["sb_vs"].append(vs)
        elif kind == 1:
            w_in = jnp.pad(gdn_w_in[j], ((0, 0), (0, GDN_IN_PADDED - GDN_IN))).astype(BF16)
            w_out = gdn_w_out[j].astype(BF16)
            pp = norm_matmul(yp, g, w_in)
            ps = norm_matmul(ys, g, w_in)
            conv0 = jnp.zeros((bp, GDN_CONV - 1, GDN_CONV_CH), F32)
            s0 = jnp.zeros((bp,) + state_gdn_rec.shape[2:], F32)
            op, sp = gdn_delta(pp, conv0, gdn_conv_w[j], gdn_a_log[j], gdn_dt_bias[j], gdn_o_norm[j], s0,
                               bp, tp, BF16)
            os_, ss = gdn_delta(ps, state_gdn_conv[j], gdn_conv_w[j], gdn_a_log[j], gdn_dt_bias[j],
                                gdn_o_norm[j], state_gdn_rec[j], bs, ts, F32)
            outs["gdn_cp"].append(pp.reshape(bp, tp, -1)[:, tp - (GDN_CONV - 1):, :GDN_CONV_CH])
            outs["gdn_cs"].append(ps.reshape(bs, ts, -1)[:, ts - (GDN_CONV - 1):, :GDN_CONV_CH])
            outs["gdn_sp"].append(sp); outs["gdn_ss"].append(ss)
        else:
            w_in = moba_w_in[j].astype(BF16)
            w_out = moba_w_out[j].astype(BF16)
            head_gain = jnp.concatenate([jnp.tile(moba_q_norm[j], N_HEADS), jnp.tile(moba_k_norm[j], N_KV_HEADS),
                                         jnp.ones((ATTN_IN - ATTN_WIDTH - KV_WIDTH,), F32)]).reshape(1, ATTN_IN)
            pp = norm_matmul(yp, g, w_in, head_gain, ATTN_WIDTH + KV_WIDTH)
            ps = norm_matmul(ys, g, w_in, head_gain, ATTN_WIDTH + KV_WIDTH)
            op = moba_attention_prompt(pp, rel_bias, bp, tp)
            os_ = moba_attention_sample(ps, _flat_pool(cache_moba_k), _flat_pool(cache_moba_v), j * n_phys,
                                        page_table, rel_bias, ts)
            kp, vp = _new_kv(pp, bp, tp)
            ks, vs = _new_kv(ps, bs, ts)
            outs["mb_kp"].append(kp); outs["mb_vp"].append(vp); outs["mb_ks"].append(ks); outs["mb_vs"].append(vs)
        yp = matmul_residual(op, w_out, yp)
        ys = matmul_residual(os_, w_out, ys)
    stack = lambda name: jnp.stack(outs[name])
    return (yp.reshape(bp, tp, d), ys.reshape(bs, ts, d),
            stack("sb_kp"), stack("sb_vp"), stack("sb_ks"), stack("sb_vs"),
            stack("gdn_cp"), stack("gdn_sp"), stack("gdn_cs"), stack("gdn_ss"),
            stack("mb_kp"), stack("mb_vp"), stack("mb_ks"), stack("mb_vs"))
```

```python
import functools
import math

import jax
import jax.numpy as jnp
from jax import lax
from jax.experimental import pallas as pl
from jax.experimental.pallas import tpu as pltpu

F32 = jnp.float32
BF16 = jnp.bfloat16
HI = lax.Precision.HIGHEST

LANES = 128
SUBLANES = 8
VMEM_LIMIT = 56 * 1024 * 1024

HEAD_DIM = 128
N_HEADS = 16
N_KV_HEADS = 4
GROUP = N_HEADS // N_KV_HEADS
ATTN_WIDTH = N_HEADS * HEAD_DIM
KV_WIDTH = N_KV_HEADS * HEAD_DIM
ATTN_IN = 2 * ATTN_WIDTH + 2 * KV_WIDTH
Q_COL = 0
K_COL = ATTN_WIDTH // HEAD_DIM
V_COL = K_COL + N_KV_HEADS
GATE_COL = (ATTN_WIDTH + 2 * KV_WIDTH) // (GROUP * HEAD_DIM)
PAGE_SIZE = 128

GDN_HEAD_DIM = 128
GDN_K_HEADS = 16
GDN_V_HEADS = 32
GDN_KEY_WIDTH = GDN_K_HEADS * GDN_HEAD_DIM
GDN_VAL_WIDTH = GDN_V_HEADS * GDN_HEAD_DIM
GDN_CONV_CH = 2 * GDN_KEY_WIDTH + GDN_VAL_WIDTH
GDN_CONV = 4
GDN_CHUNK = 64
GDN_IN = GDN_CONV_CH + GDN_VAL_WIDTH + 2 * GDN_V_HEADS
GDN_Z_COL = GDN_CONV_CH // GDN_HEAD_DIM
GDN_BD_COL = (GDN_CONV_CH + GDN_VAL_WIDTH) // LANES

MOBA_BLOCK = 256
MOBA_TOPK = 3
REL_BUCKETS = 32
REL_MAX_DIST = 4096

EPS = 1e-6
NEG = -1e30
SCALE = HEAD_DIM ** -0.5

_NT = (((1,), (1,)), ((), ()))


def _cparams(*sem):
    return pltpu.CompilerParams(dimension_semantics=sem, vmem_limit_bytes=VMEM_LIMIT)


def _softplus(z):
    return jnp.maximum(z, 0.0) + jnp.log1p(jnp.exp(-jnp.abs(z)))


def _silu(x):
    return x * jax.nn.sigmoid(x)


def _dot(a, b, precision=None):
    return jnp.dot(a, b, preferred_element_type=F32, precision=precision)


def _dot_nt(a, b, precision=None):
    return lax.dot_general(a, b, _NT, preferred_element_type=F32, precision=precision)


def _split_bf16(x):
    hi = x.astype(BF16)
    return hi, (x - hi.astype(F32)).astype(BF16)


def _bdot(a, b):
    return jnp.einsum("bij,bjk->bik", a, b, preferred_element_type=F32)


def _bdot_split(a, b):
    return _bdot(a[0], b[0]) + (_bdot(a[0], b[1]) + _bdot(a[1], b[0]))


def _norm_matmul_kernel(x_ref, g_ref, w_ref, hg_ref, o_ref, h_scr, *, n_norm_tiles, tn):
    j = pl.program_id(1)

    @pl.when(j == 0)
    def _():
        x = x_ref[...]
        ms = jnp.mean(x * x, axis=-1, keepdims=True)
        h_scr[...] = (x * lax.rsqrt(ms + EPS) * g_ref[...]).astype(BF16)

    acc = _dot(h_scr[...], w_ref[...])
    if n_norm_tiles == 0:
        o_ref[...] = acc
    else:
        @pl.when(j < n_norm_tiles)
        def _():
            for s in range(tn // HEAD_DIM):
                sl = slice(s * HEAD_DIM, (s + 1) * HEAD_DIM)
                a = acc[:, sl]
                ms = jnp.mean(a * a, axis=-1, keepdims=True)
                o_ref[:, sl] = a * lax.rsqrt(ms + EPS) * hg_ref[:, sl]

        @pl.when(j >= n_norm_tiles)
        def _():
            o_ref[...] = acc


def norm_matmul(x, g, w, head_gain=None, n_norm_cols=0, tn=512):
    m, d = x.shape
    n = w.shape[1]
    tm = min(m, 1024)
    assert m % tm == 0 and n % tn == 0 and n_norm_cols % tn == 0
    if head_gain is None:
        head_gain = jnp.ones((1, n), F32)
    return pl.pallas_call(
        functools.partial(_norm_matmul_kernel, n_norm_tiles=n_norm_cols // tn, tn=tn),
        grid=(m // tm, n // tn),
        in_specs=[pl.BlockSpec((tm, d), lambda i, j: (i, 0)),
                  pl.BlockSpec((1, d), lambda i, j: (0, 0)),
                  pl.BlockSpec((d, tn), lambda i, j: (0, j)),
                  pl.BlockSpec((1, tn), lambda i, j: (0, j))],
        out_specs=pl.BlockSpec((tm, tn), lambda i, j: (i, j)),
        out_shape=jax.ShapeDtypeStruct((m, n), F32),
        scratch_shapes=[pltpu.VMEM((tm, d), BF16)],
        compiler_params=_cparams("parallel", "arbitrary"),
        name="norm_matmul",
    )(x, g.reshape(1, d), w, head_gain)


def _matmul_residual_kernel(a_ref, w_ref, r_ref, o_ref):
    o_ref[...] = r_ref[...] + _dot(a_ref[...].astype(BF16), w_ref[...])


def matmul_residual(a, w, res, tn=512):
    m, k = a.shape
    n = w.shape[1]
    tm = min(m, 1024)
    assert m % tm == 0 and n % tn == 0
    return pl.pallas_call(
        _matmul_residual_kernel,
        grid=(m // tm, n // tn),
        in_specs=[pl.BlockSpec((tm, k), lambda i, j: (i, 0)),
                  pl.BlockSpec((k, tn), lambda i, j: (0, j)),
                  pl.BlockSpec((tm, tn), lambda i, j: (i, j))],
        out_specs=pl.BlockSpec((tm, tn), lambda i, j: (i, j)),
        out_shape=jax.ShapeDtypeStruct((m, n), F32),
        compiler_params=_cparams("parallel", "parallel"),
        name="matmul_residual",
    )(a, w, res)


def _stack_heads(x, n):
    return jnp.concatenate([x[:, g * HEAD_DIM:(g + 1) * HEAD_DIM] for g in range(n)], axis=0)


def _unstack_heads(x, n):
    t = x.shape[0] // n
    return jnp.concatenate([x[g * t:(g + 1) * t] for g in range(n)], axis=1)


def _later_matrix(tk):
    r = lax.broadcasted_iota(jnp.int32, (2 * tk, tk), 0) & (tk - 1)
    c = lax.broadcasted_iota(jnp.int32, (2 * tk, tk), 1)
    return jnp.where(r > c, -1.0, 0.0).astype(BF16)


def _sb_block(qs, k, v, carry, acc, later_mat, mask=None):
    tk = later_mat.shape[1]
    n = k.shape[-2] // tk
    batched = qs.ndim == 3
    z = jnp.einsum("hrd,hsd->hrs", qs, k, preferred_element_type=F32) if batched else _dot_nt(qs, k)
    if mask is not None:
        z = jnp.where(mask, z, NEG)
    sp = jnp.maximum(z, 0.0) + jnp.log(1.0 + jnp.exp(-jnp.abs(z)))
    hi = sp.astype(BF16)
    lo = (sp - hi.astype(F32)).astype(BF16)
    later, total = [], []
    for j in range(n):
        sl = slice(j * tk, (j + 1) * tk)
        split = jnp.concatenate([hi[..., sl], lo[..., sl]], axis=-1)
        lt = _dot(split.reshape(-1, 2 * tk), later_mat).reshape(z.shape[:-1] + (tk,))
        later.append(lt)
        total.append(lt[..., :1] - sp[..., j * tk:j * tk + 1])
    for j in reversed(range(n)):
        later[j] = later[j] + carry
        carry = carry + total[j]
    later = later[0] if n == 1 else jnp.concatenate(later, axis=-1)
    w = jnp.exp(z - sp + later).astype(BF16)
    pv = jnp.einsum("hrs,hsd->hrd", w, v, preferred_element_type=F32) if batched else _dot(w, v)
    return carry, acc + pv


SB_BLOCKS_PER_STEP = 4


def _sb_prompt_kernel(q_ref, k_ref, v_ref, gate_ref, o_ref, carry_scr, acc_scr, *, tq):
    qi = pl.program_id(2)
    rows = GROUP * tq
    nk = SB_BLOCKS_PER_STEP
    qs = (_stack_heads(q_ref[...], GROUP) * SCALE).astype(BF16)
    later_mat = _later_matrix(tq)
    q_pos = qi * tq + (lax.broadcasted_iota(jnp.int32, (rows, 1), 0) & (tq - 1))

    def run(first_blk, n, carry, acc, masked):
        start = pl.multiple_of(first_blk * tq, tq)
        k = k_ref[pl.ds(start, n * tq), :].astype(BF16)
        v = v_ref[pl.ds(start, n * tq), :].astype(BF16)
        mask = None
        if masked:
            mask = lax.broadcasted_iota(jnp.int32, (rows, n * tq), 1) + start < q_pos
        c, a = _sb_block(qs, k, v, carry, acc, later_mat, mask)
        carry_scr[...] = c
        acc_scr[...] = a

    n_first = qi % nk + 1
    for n in range(1, nk + 1):
        @pl.when(n_first == n)
        def _():
            run(qi + 1 - n, n, jnp.zeros((rows, 1), F32), jnp.zeros((rows, HEAD_DIM), F32), masked=True)

    def body(i, _):
        run(qi + 1 - n_first - nk * (i + 1), nk, carry_scr[...], acc_scr[...], masked=False)
        return 0

    lax.fori_loop(0, (qi + 1 - n_first) // nk, body, 0)
    o = _unstack_heads(acc_scr[...], GROUP)
    o_ref[...] = (o * _silu(gate_ref[...])).astype(o_ref.dtype)


def sb_attention_prompt(proj, batch, seq, tq=128):
    assert seq % tq == 0 and seq >= SB_BLOCKS_PER_STEP * tq
    nq = seq // tq
    gw = GROUP * HEAD_DIM
    return pl.pallas_call(
        functools.partial(_sb_prompt_kernel, tq=tq),
        grid=(batch, N_KV_HEADS, nq),
        in_specs=[pl.BlockSpec((tq, gw), lambda b, h, i: (b * nq + i, h)),
                  pl.BlockSpec((seq, HEAD_DIM), lambda b, h, i: (b, K_COL + h)),
                  pl.BlockSpec((seq, HEAD_DIM), lambda b, h, i: (b, V_COL + h)),
                  pl.BlockSpec((tq, gw), lambda b, h, i: (b * nq + i, GATE_COL + h))],
        out_specs=pl.BlockSpec((tq, gw), lambda b, h, i: (b * nq + i, h)),
        out_shape=jax.ShapeDtypeStruct((batch * seq, ATTN_WIDTH), BF16),
        scratch_shapes=[pltpu.VMEM((GROUP * tq, 1), F32), pltpu.VMEM((GROUP * tq, HEAD_DIM), F32)],
        compiler_params=_cparams("parallel", "parallel", "parallel"),
        name="sb_attention_prompt",
    )(proj, proj, proj, proj)


def _pad_rows(x, rows):
    return jnp.concatenate([x, jnp.zeros((rows - x.shape[0], x.shape[1]), x.dtype)], axis=0)


PAGES_PER_STEP = 16


def _page_specs(n_pages, page_offset, latest_first):
    n_steps = n_pages // PAGES_PER_STEP

    def index_map(b, p, pt, *, j):
        step = n_steps - 1 - p if latest_first else p
        return (page_offset + pt[b, step * PAGES_PER_STEP + j], 0, 0)

    return [pl.BlockSpec((1, PAGE_SIZE * N_KV_HEADS, HEAD_DIM), functools.partial(index_map, j=j))
            for j in range(PAGES_PER_STEP)]


def _head_pages(page_refs, h):
    return jnp.concatenate([r[0, pl.ds(h, PAGE_SIZE, stride=N_KV_HEADS), :] for r in page_refs], axis=0)


def _flat_pool(cache):
    return cache.reshape(cache.shape[0] * cache.shape[1], PAGE_SIZE * N_KV_HEADS, HEAD_DIM)


def _sb_sample_kernel(pt_ref, proj_ref, *refs, tq):
    del pt_ref
    k_refs, v_refs = refs[:PAGES_PER_STEP], refs[PAGES_PER_STEP:2 * PAGES_PER_STEP]
    o_ref, carry_scr, acc_scr = refs[2 * PAGES_PER_STEP:]
    p = pl.program_id(1)
    rows = GROUP * tq
    later_mat = _later_matrix(PAGE_SIZE)

    def per_head(f):
        return jnp.stack([f(h) for h in range(N_KV_HEADS)])

    def q_rows(h):
        q = proj_ref[:, h * GROUP * HEAD_DIM:(h + 1) * GROUP * HEAD_DIM]
        return (_stack_heads(q, GROUP) * SCALE).astype(BF16)

    def new_rows(col):
        return per_head(lambda h: _pad_rows(proj_ref[:, col + h * HEAD_DIM:col + (h + 1) * HEAD_DIM],
                                            PAGE_SIZE).astype(BF16))

    qs = per_head(q_rows)

    @pl.when(p == 0)
    def _():
        t_loc = lax.broadcasted_iota(jnp.int32, (rows, PAGE_SIZE), 0) & (tq - 1)
        s_loc = lax.broadcasted_iota(jnp.int32, (rows, PAGE_SIZE), 1)
        c, a = _sb_block(qs, new_rows(ATTN_WIDTH), new_rows(ATTN_WIDTH + KV_WIDTH),
                         jnp.zeros((N_KV_HEADS, rows, 1), F32), jnp.zeros((N_KV_HEADS, rows, HEAD_DIM), F32),
                         later_mat, s_loc < t_loc)
        carry_scr[...] = c
        acc_scr[...] = a

    c, a = _sb_block(qs, per_head(lambda h: _head_pages(k_refs, h).astype(BF16)),
                     per_head(lambda h: _head_pages(v_refs, h).astype(BF16)),
                     carry_scr[...], acc_scr[...], later_mat)
    carry_scr[...] = c
    acc_scr[...] = a

    @pl.when(p == pl.num_programs(1) - 1)
    def _():
        gc = ATTN_WIDTH + 2 * KV_WIDTH
        for h in range(N_KV_HEADS):
            sl = slice(h * GROUP * HEAD_DIM, (h + 1) * GROUP * HEAD_DIM)
            gate = proj_ref[:, gc + sl.start:gc + sl.stop]
            o_ref[:, sl] = _unstack_heads(acc_scr[h], GROUP) * _silu(gate)


def sb_attention_sample(proj, k_pool, v_pool, page_offset, page_table, tq):
    batch, n_pages = page_table.shape
    assert n_pages % PAGES_PER_STEP == 0
    rows = GROUP * tq
    specs = _page_specs(n_pages, page_offset, latest_first=True)
    return pl.pallas_call(
        functools.partial(_sb_sample_kernel, tq=tq),
        grid_spec=pltpu.PrefetchScalarGridSpec(
            num_scalar_prefetch=1,
            grid=(batch, n_pages // PAGES_PER_STEP),
            in_specs=[pl.BlockSpec((tq, ATTN_IN), lambda b, p, pt: (b, 0))] + specs + specs,
            out_specs=pl.BlockSpec((tq, ATTN_WIDTH), lambda b, p, pt: (b, 0)),
            scratch_shapes=[pltpu.VMEM((N_KV_HEADS, rows, 1), F32),
                            pltpu.VMEM((N_KV_HEADS, rows, HEAD_DIM), F32)]),
        out_shape=jax.ShapeDtypeStruct((batch * tq, ATTN_WIDTH), F32),
        compiler_params=_cparams("parallel", "arbitrary"),
        name="sb_attention_sample",
    )(page_table, proj, *([k_pool] * PAGES_PER_STEP), *([v_pool] * PAGES_PER_STEP))


GDN_ROWS = 128
GDN_REP = GDN_V_HEADS // GDN_K_HEADS


def _gdn_delta_kernel(q_ref, k_ref, v_ref, bd_ref, z_ref, cq_ref, ck_ref, cv_ref, wq_ref, wk_ref, wv_ref,
                      alog_ref, dt_ref, onorm_ref, s0_ref,
                      o_ref, s_ref, u_scr, w_scr, in_scr, qd_scr, kdt_scr, gl_scr, gt_scr, *, seq):
    hk = pl.program_id(1)
    c = GDN_ROWS
    n_chunks = max(seq // c, 1)
    unroll = gt_scr.shape[0]
    n_doublings = max(math.ceil(math.log2(min(seq, c))) - 1, 0)
    ri = lax.broadcasted_iota(jnp.int32, (c, c), 0)
    ci = lax.broadcasted_iota(jnp.int32, (c, c), 1)
    lane = lax.broadcasted_iota(jnp.int32, (c, LANES), 1)
    row = lax.broadcasted_iota(jnp.int32, (c, 1), 0)
    tril = ri >= ci
    tril_b = jnp.where(tril, 1.0, 0.0).astype(BF16)
    neg_a = -jnp.exp(alog_ref[...])

    def chunk_rows(i):
        return pl.ds(i * c, c) if isinstance(i, int) else pl.ds(pl.multiple_of(i * c, c), c)

    def rows(ref, i):
        if seq < c:
            return _pad_rows(ref[...], c)
        return ref[chunk_rows(i), :]

    def conv_silu(ref, prev_ref, w_ref, i):
        n = min(seq, c)
        if isinstance(i, int) and i == 0:
            ext = jnp.concatenate([prev_ref[0], ref[pl.ds(0, n), :]], axis=0)
        elif isinstance(i, int):
            ext = ref[pl.ds(i * c - SUBLANES, c + SUBLANES), :]
        else:
            ext = ref[pl.ds(pl.multiple_of(i * c - SUBLANES, SUBLANES), c + SUBLANES), :]
        w = w_ref[...]
        y = ext[SUBLANES:] * w[GDN_CONV - 1:GDN_CONV, :]
        for tap in range(1, GDN_CONV):
            y = y + pltpu.roll(ext, tap, 0)[SUBLANES:] * w[GDN_CONV - 1 - tap:GDN_CONV - tap, :]
        y = _silu(y)
        return y if n == c else _pad_rows(y, c)

    def l2_normalised(x, scale):
        return x * (lax.rsqrt(jnp.sum(x * x, axis=-1, keepdims=True) + EPS) * scale)

    def prepare(i, _):
        ms, rhss = [], []
        for j in range(unroll):
            ch = i * unroll + j
            sl = chunk_rows(ch)
            q = l2_normalised(conv_silu(q_ref, cq_ref, wq_ref, ch), GDN_HEAD_DIM ** -0.5)
            k = l2_normalised(conv_silu(k_ref, ck_ref, wk_ref, ch), 1.0)
            v2 = conv_silu(v_ref, cv_ref, wv_ref, ch)
            raw = rows(bd_ref, ch)
            k16 = k.astype(BF16)
            kk = _dot_nt(k16, k16)
            qk = _dot_nt(q.astype(BF16), k16)
            sig = jax.nn.sigmoid(raw)
            g_all = jnp.where(row < seq, neg_a * _softplus(raw + dt_ref[...]), 0.0)
            g_hi = g_all.astype(BF16)
            g_mid, g_lo = _split_bf16(g_all - g_hi.astype(F32))
            gcx = _dot(tril_b, g_hi) + (_dot(tril_b, g_mid) + _dot(tril_b, g_lo))
            gt_scr[j] = gcx.T
            for e in range(GDN_REP):
                hv = hk * GDN_REP + e
                beta = jnp.sum(jnp.where(lane == hv, sig, 0.0), axis=1, keepdims=True)
                gc = jnp.sum(jnp.where(lane == hv + GDN_V_HEADS, gcx, 0.0), axis=1, keepdims=True)
                g_row = gt_scr[j, pl.ds(hv + GDN_V_HEADS, 1), :]
                decay = jnp.exp(jnp.where(tril, gc - g_row, NEG))
                ms.append(-jnp.where(ri > ci, kk * beta * decay, 0.0))
                egc = jnp.exp(gc)
                v = v2[:, e * GDN_HEAD_DIM:(e + 1) * GDN_HEAD_DIM]
                rhss.append(jnp.concatenate([v * beta, k * (beta * egc)], axis=1))
                g_last = gc[c - 1:c, :]
                in_scr[e, sl, :] = (qk * decay).astype(BF16)
                qd_scr[e, sl, :] = (q * egc).astype(BF16)
                kdt_scr[e, ch] = (k * jnp.exp(g_last - gc)).T.astype(BF16)
                gl_scr[e, ch] = jnp.broadcast_to(jnp.exp(g_last), (SUBLANES, LANES))
        nmat = jnp.stack(ms)
        pw = _split_bf16(nmat)
        for _ in range(n_doublings):
            sq = _bdot_split(pw, pw)
            pw = _split_bf16(sq)
            nmat = nmat + sq + _bdot_split(_split_bf16(nmat), pw)
        rhs = jnp.stack(rhss)
        sol = rhs + _bdot(nmat.astype(BF16), rhs.astype(BF16))
        for j in range(unroll):
            sl = chunk_rows(i * unroll + j)
            for e in range(GDN_REP):
                x = sol[j * GDN_REP + e]
                u_scr[e, sl, :] = x[:, :GDN_HEAD_DIM]
                w_scr[e, sl, :] = x[:, GDN_HEAD_DIM:].astype(BF16)
        return 0

    def advance(i, states):
        sl = chunk_rows(i)
        z2 = rows(z_ref, i)
        out = []
        for e in range(GDN_REP):
            s = states[e]
            s16 = s.astype(BF16)
            v_new = u_scr[e, sl, :] - _dot(w_scr[e, sl, :], s16)
            v16 = v_new.astype(BF16)
            o = _dot(qd_scr[e, sl, :], s16) + _dot(in_scr[e, sl, :], v16)
            out.append(s * gl_scr[e, i][0:1, :] + _dot(kdt_scr[e, i], v16))
            ms = jnp.mean(o * o, axis=-1, keepdims=True)
            z = z2[:, e * GDN_HEAD_DIM:(e + 1) * GDN_HEAD_DIM]
            o = (o * lax.rsqrt(ms + EPS) * onorm_ref[...] * _silu(z)).astype(o_ref.dtype)
            cols = slice(e * GDN_HEAD_DIM, (e + 1) * GDN_HEAD_DIM)
            if seq < c:
                o_ref[:, cols] = o[:seq]
            else:
                o_ref[sl, cols] = o
        return tuple(out)

    states = tuple(s0_ref[0, e] for e in range(GDN_REP))
    if n_chunks == 1:
        prepare(0, 0)
        states = advance(0, states)
    else:
        n_groups = n_chunks // unroll
        prepare(0, 0)

        def body(i, states):
            for j in range(unroll):
                states = advance((i - 1) * unroll + j, states)
            prepare(i, 0)
            return states

        states = lax.fori_loop(1, n_groups, body, states)
        for j in range(unroll):
            states = advance((n_groups - 1) * unroll + j, states)
    for e in range(GDN_REP):
        s_ref[0, e] = states[e]


def gdn_delta(proj, conv_state, conv_w, a_log, dt_bias, o_norm, s0, batch, seq, out_dtype):
    assert (seq % GDN_ROWS == 0 or seq < GDN_ROWS) and seq % SUBLANES == 0
    prev = jnp.pad(conv_state, ((0, 0), (SUBLANES - (GDN_CONV - 1), 0), (0, 0)))
    vcol = 2 * GDN_K_HEADS // GDN_REP
    pad = jnp.zeros((GDN_V_HEADS,), F32)
    tail = jnp.zeros((LANES - 2 * GDN_V_HEADS,), F32)
    alog = jnp.concatenate([pad, a_log, tail]).reshape(1, LANES)
    dt = jnp.concatenate([pad, dt_bias, tail]).reshape(1, LANES)
    blk = (seq, GDN_HEAD_DIM)
    wide = (seq, GDN_REP * GDN_HEAD_DIM)
    state_blk = (1, GDN_REP, GDN_HEAD_DIM, GDN_HEAD_DIM)
    rows = max(seq, GDN_ROWS)
    n_chunks = rows // GDN_ROWS
    return pl.pallas_call(
        functools.partial(_gdn_delta_kernel, seq=seq),
        grid=(batch, GDN_K_HEADS),
        in_specs=[pl.BlockSpec(blk, lambda b, h: (b, h)),
                  pl.BlockSpec(blk, lambda b, h: (b, GDN_K_HEADS + h)),
                  pl.BlockSpec(wide, lambda b, h: (b, vcol + h)),
                  pl.BlockSpec((seq, LANES), lambda b, h: (b, GDN_BD_COL)),
                  pl.BlockSpec(wide, lambda b, h: (b, GDN_Z_COL // GDN_REP + h)),
                  pl.BlockSpec((1, SUBLANES, blk[1]), lambda b, h: (b, 0, h)),
                  pl.BlockSpec((1, SUBLANES, blk[1]), lambda b, h: (b, 0, GDN_K_HEADS + h)),
                  pl.BlockSpec((1, SUBLANES, wide[1]), lambda b, h: (b, 0, vcol + h)),
                  pl.BlockSpec((GDN_CONV, blk[1]), lambda b, h: (0, h)),
                  pl.BlockSpec((GDN_CONV, blk[1]), lambda b, h: (0, GDN_K_HEADS + h)),
                  pl.BlockSpec((GDN_CONV, wide[1]), lambda b, h: (0, vcol + h)),
                  pl.BlockSpec((1, LANES), lambda b, h: (0, 0)),
                  pl.BlockSpec((1, LANES), lambda b, h: (0, 0)),
                  pl.BlockSpec((1, GDN_HEAD_DIM), lambda b, h: (0, 0)),
                  pl.BlockSpec(state_blk, lambda b, h: (b, h, 0, 0))],
        out_specs=[pl.BlockSpec(wide, lambda b, h: (b, h)),
                   pl.BlockSpec(state_blk, lambda b, h: (b, h, 0, 0))],
        scratch_shapes=[pltpu.VMEM((GDN_REP, rows, GDN_HEAD_DIM), F32),
                        pltpu.VMEM((GDN_REP, rows, GDN_HEAD_DIM), BF16),
                        pltpu.VMEM((GDN_REP, rows, GDN_ROWS), BF16),
                        pltpu.VMEM((GDN_REP, rows, GDN_HEAD_DIM), BF16),
                        pltpu.VMEM((GDN_REP, n_chunks, GDN_HEAD_DIM, GDN_ROWS), BF16),
                        pltpu.VMEM((GDN_REP, n_chunks, SUBLANES, LANES), F32),
                        pltpu.VMEM((2 if n_chunks % 2 == 0 else 1, GDN_ROWS, LANES), F32)],
        out_shape=[jax.ShapeDtypeStruct((batch * seq, GDN_VAL_WIDTH), out_dtype),
                   jax.ShapeDtypeStruct((batch, GDN_V_HEADS, GDN_HEAD_DIM, GDN_HEAD_DIM), F32)],
        compiler_params=_cparams("parallel", "parallel"),
        name="gdn_delta",
    )(proj, proj, proj, proj, proj, prev, prev, prev, conv_w, conv_w, conv_w,
      alog, dt, o_norm.reshape(1, GDN_HEAD_DIM), s0)


def _rel_bucket(dist):
    max_exact = REL_BUCKETS // 2
    n = jnp.maximum(dist, 0)
    large = max_exact + (jnp.log(jnp.maximum(n, 1).astype(F32) / max_exact)
                         / math.log(REL_MAX_DIST / max_exact) * (REL_BUCKETS - max_exact)).astype(jnp.int32)
    large = jnp.minimum(large, REL_BUCKETS - 1)
    return jnp.where(n < max_exact, n, large)


def _bias_table_kernel(rbt_ref, onehot_ref, o_ref):
    o_ref[...] = _dot(rbt_ref[...], onehot_ref[...], HI)


def bias_by_distance(rel_bias, dist):
    bucket = _rel_bucket(dist)
    onehot = (bucket[None, :] == jnp.arange(REL_BUCKETS)[:, None]) & (dist[None, :] >= 0)
    return pl.pallas_call(
        _bias_table_kernel,
        out_shape=jax.ShapeDtypeStruct((N_HEADS, dist.shape[0]), F32),
        compiler_params=pltpu.CompilerParams(vmem_limit_bytes=VMEM_LIMIT),
        name="bias_table",
    )(rel_bias.T, onehot.astype(F32))


def _toeplitz(window_row, rows, shift):
    x = jnp.broadcast_to(window_row, (rows, window_row.shape[1]))
    return pltpu.roll(x, shift, 1, stride=1, stride_axis=0)


def _select_topk(gate, n_valid, n_blocks):
    lane = lax.broadcasted_iota(jnp.int32, gate.shape, 1)
    valid = lane < n_valid
    gm = jnp.where(valid, gate, -jnp.inf)
    cnt = jnp.zeros(gate.shape, jnp.int32)
    for m in range(n_blocks):
        col = gm[:, m:m + 1]
        beats = (col > gm) | ((col == gm) & (lane > m))
        cnt = cnt + jnp.where(beats, 1, 0)
    return jnp.where(valid & (cnt < MOBA_TOPK), 1.0, 0.0)


def _select_topk_t(gate_t, n_valid):
    blk = lax.broadcasted_iota(jnp.int32, gate_t.shape, 0)
    valid = blk < n_valid
    gm = jnp.where(valid, gate_t, -jnp.inf)
    cnt = jnp.zeros(gate_t.shape, jnp.int32)
    for m in range(gate_t.shape[0]):
        row = gm[m:m + 1, :]
        beats = (row > gm) | ((row == gm) & (blk > m))
        cnt = cnt + jnp.where(beats, 1, 0)
    return jnp.where(valid & (cnt < MOBA_TOPK), 1.0, 0.0)


def _moba_tile(qs, k, v, bias, mask, m, l, acc):
    batched = qs.ndim == 3
    s = jnp.einsum("hrd,hsd->hrs", qs, k, preferred_element_type=F32) if batched else _dot_nt(qs, k)
    s = jnp.where(mask, s * SCALE + bias, NEG)
    m_new = jnp.maximum(m, jnp.max(s, axis=-1, keepdims=True))
    p = jnp.where(mask, jnp.exp(s - m_new), 0.0)
    alpha = jnp.exp(m - m_new)
    l = alpha * l + jnp.sum(p, axis=-1, keepdims=True)
    p = p.astype(BF16)
    pv = jnp.einsum("hrs,hsd->hrd", p, v, preferred_element_type=F32) if batched else _dot(p, v)
    return m_new, l, alpha * acc + pv


def _moba_prompt_kernel(q_ref, k_ref, v_ref, gate_ref, tb_ref, o_ref,
                        means_scr, bias_scr, pen_scr, m_scr, l_scr, acc_scr, *, tq, n_blocks):
    h = pl.program_id(0)
    b = pl.program_id(1)
    qi = pl.program_id(2)
    rows = GROUP * tq

    @pl.when(qi == 0)
    def _():
        means_scr[...] = jnp.zeros_like(means_scr)
        means_scr[0:n_blocks, :] = jnp.mean(k_ref[...].reshape(n_blocks, MOBA_BLOCK, HEAD_DIM), axis=1)

    @pl.when(b == 0)
    def _():
        for g in range(GROUP):
            wrow = tb_ref[pl.ds((h * GROUP + g) * n_blocks + qi, 1), :]
            bias_scr[qi, pl.ds(g * tq, tq), :] = _toeplitz(wrow, tq, tq + 1)[:, :tq]

    qs32 = _stack_heads(q_ref[...], GROUP)
    qs = (qs32 * SCALE).astype(BF16)
    sel_t = _select_topk_t(_dot_nt(means_scr[...], qs32, HI), qi)
    pen_t = (sel_t - 1.0) * -NEG
    pen_scr[...] = jnp.concatenate([pen_t, jnp.zeros((LANES - pen_t.shape[0], rows), F32)], axis=0).T
    lane = lax.broadcasted_iota(jnp.int32, (rows, LANES), 1)
    t_loc = lax.broadcasted_iota(jnp.int32, (rows, tq), 0) & (tq - 1)
    s_loc = lax.broadcasted_iota(jnp.int32, (rows, tq), 1)

    def load(ref, blk):
        return ref[pl.ds(pl.multiple_of(blk * tq, tq), tq), :].astype(BF16)

    def update(s, v, m, l, acc):
        m_new = jnp.maximum(m, jnp.max(s, axis=1, keepdims=True))
        p = jnp.exp(s - m_new)
        alpha = jnp.exp(m - m_new)
        m_scr[...] = m_new
        l_scr[...] = alpha * l + jnp.sum(p, axis=1, keepdims=True)
        acc_scr[...] = alpha * acc + _dot(p.astype(BF16), v)

    s = jnp.where(s_loc <= t_loc, _dot_nt(qs, load(k_ref, qi)) + bias_scr[0], NEG)
    update(s, load(v_ref, qi), jnp.full((rows, 1), NEG, F32), jnp.zeros((rows, 1), F32),
           jnp.zeros((rows, HEAD_DIM), F32))

    def penalty(kb):
        return jnp.sum(jnp.where(lane == kb, pen_scr[...], 0.0), axis=1, keepdims=True)

    def pair(i, _):
        kb = 2 * i
        start = pl.multiple_of(kb * tq, tq)
        z = _dot_nt(qs, k_ref[pl.ds(start, 2 * tq), :].astype(BF16))
        s = jnp.concatenate([z[:, :tq] + bias_scr[qi - kb] + penalty(kb),
                             z[:, tq:] + bias_scr[qi - kb - 1] + penalty(kb + 1)], axis=1)
        update(s, v_ref[pl.ds(start, 2 * tq), :].astype(BF16), m_scr[...], l_scr[...], acc_scr[...])
        return 0

    lax.fori_loop(0, qi // 2, pair, 0)

    @pl.when(qi % 2 == 1)
    def _():
        kb = qi - 1
        update(_dot_nt(qs, load(k_ref, kb)) + bias_scr[1] + penalty(kb), load(v_ref, kb),
               m_scr[...], l_scr[...], acc_scr[...])
    o = _unstack_heads(acc_scr[...] / l_scr[...], GROUP)
    o_ref[...] = (o * _silu(gate_ref[...])).astype(o_ref.dtype)


def moba_attention_prompt(proj, rel_bias, batch, seq):
    tq = MOBA_BLOCK
    assert seq % tq == 0
    nb = seq // tq
    gw = GROUP * HEAD_DIM
    rows = GROUP * tq
    dist = (jnp.arange(nb)[:, None] * tq + (tq - 1) - jnp.arange(2 * tq)[None, :]).reshape(-1)
    table = bias_by_distance(rel_bias, dist).reshape(N_HEADS * nb, 2 * tq)
    return pl.pallas_call(
        functools.partial(_moba_prompt_kernel, tq=tq, n_blocks=nb),
        grid=(N_KV_HEADS, batch, nb),
        in_specs=[pl.BlockSpec((tq, gw), lambda h, b, i: (b * nb + i, h)),
                  pl.BlockSpec((seq, HEAD_DIM), lambda h, b, i: (b, K_COL + h)),
                  pl.BlockSpec((seq, HEAD_DIM), lambda h, b, i: (b, V_COL + h)),
                  pl.BlockSpec((tq, gw), lambda h, b, i: (b * nb + i, GATE_COL + h)),
                  pl.BlockSpec((N_HEADS * nb, 2 * tq), lambda h, b, i: (0, 0))],
        out_specs=pl.BlockSpec((tq, gw), lambda h, b, i: (b * nb + i, h)),
        out_shape=jax.ShapeDtypeStruct((batch * seq, ATTN_WIDTH), BF16),
        scratch_shapes=[pltpu.VMEM((-(-nb // SUBLANES) * SUBLANES, HEAD_DIM), F32),
                        pltpu.VMEM((nb, rows, tq), F32),
                        pltpu.VMEM((rows, LANES), F32),
                        pltpu.VMEM((rows, 1), F32),
                        pltpu.VMEM((rows, 1), F32),
                        pltpu.VMEM((rows, HEAD_DIM), F32)],
        compiler_params=_cparams("arbitrary", "arbitrary", "arbitrary"),
        name="moba_attention_prompt",
    )(proj, proj, proj, proj, table)


PAGES_PER_BLOCK = MOBA_BLOCK // PAGE_SIZE
BLOCKS_PER_STEP = PAGES_PER_STEP // PAGES_PER_BLOCK


def _moba_means_kernel(pt_ref, *refs):
    del pt_ref
    page_refs, o_ref = refs[:-1], refs[-1]
    for blk in range(BLOCKS_PER_STEP):
        pages = page_refs[blk * PAGES_PER_BLOCK:(blk + 1) * PAGES_PER_BLOCK]
        for h in range(N_KV_HEADS):
            total = jnp.sum(_head_pages(pages, h), axis=0, keepdims=True)
            o_ref[0, blk, :, h * HEAD_DIM:(h + 1) * HEAD_DIM] = total * (1.0 / MOBA_BLOCK)


def moba_block_means(k_pool, page_offset, page_table):
    batch, n_pages = page_table.shape
    assert n_pages % PAGES_PER_STEP == 0
    nb = n_pages // PAGES_PER_BLOCK
    return pl.pallas_call(
        _moba_means_kernel,
        grid_spec=pltpu.PrefetchScalarGridSpec(
            num_scalar_prefetch=1,
            grid=(batch, n_pages // PAGES_PER_STEP),
            in_specs=_page_specs(n_pages, page_offset, latest_first=False),
            out_specs=pl.BlockSpec((1, BLOCKS_PER_STEP, 1, KV_WIDTH), lambda b, n, pt: (b, n, 0, 0))),
        out_shape=jax.ShapeDtypeStruct((batch, nb, 1, KV_WIDTH), F32),
        compiler_params=_cparams("parallel", "parallel"),
        name="moba_block_means",
    )(page_table, *([k_pool] * PAGES_PER_STEP))


def _moba_sample_kernel(pt_ref, proj_ref, means_ref, tb_ref, *refs, tq, n_pages):
    del pt_ref
    k_refs, v_refs = refs[:PAGES_PER_STEP], refs[PAGES_PER_STEP:2 * PAGES_PER_STEP]
    o_ref, sel_scr, m_scr, l_scr, acc_scr = refs[2 * PAGES_PER_STEP:]
    p = pl.program_id(1)
    rows = GROUP * tq
    n_past_blocks = n_pages // PAGES_PER_BLOCK
    lane = lax.broadcasted_iota(jnp.int32, (N_KV_HEADS, rows, LANES), 2)

    def q_rows(h):
        return _stack_heads(proj_ref[:, h * GROUP * HEAD_DIM:(h + 1) * GROUP * HEAD_DIM], GROUP)

    def bias_rows(h, page):
        tiles = []
        for g in range(GROUP):
            wrow = tb_ref[pl.ds((h * GROUP + g) * (n_pages + 1) + page, 1), :]
            tiles.append(_toeplitz(wrow, tq, PAGE_SIZE + 1)[:, :PAGE_SIZE])
        return jnp.concatenate(tiles, axis=0)

    def per_head(f):
        return jnp.stack([f(h) for h in range(N_KV_HEADS)])

    def new_rows(col):
        return per_head(lambda h: _pad_rows(proj_ref[:, col + h * HEAD_DIM:col + (h + 1) * HEAD_DIM],
                                            PAGE_SIZE).astype(BF16))

    qs = per_head(lambda h: q_rows(h).astype(BF16))

    @pl.when(p == 0)
    def _():
        t_loc = lax.broadcasted_iota(jnp.int32, (rows, PAGE_SIZE), 0) & (tq - 1)
        s_loc = lax.broadcasted_iota(jnp.int32, (rows, PAGE_SIZE), 1)
        for h in range(N_KV_HEADS):
            means = _pad_rows(means_ref[0, :, h * HEAD_DIM:(h + 1) * HEAD_DIM], LANES)
            sel_scr[h] = _select_topk(_dot_nt(q_rows(h), means, HI), n_past_blocks, n_past_blocks)
        m, l, acc = _moba_tile(qs, new_rows(ATTN_WIDTH), new_rows(ATTN_WIDTH + KV_WIDTH),
                               per_head(lambda h: bias_rows(h, n_pages)), s_loc <= t_loc,
                               jnp.full((N_KV_HEADS, rows, 1), NEG, F32), jnp.zeros((N_KV_HEADS, rows, 1), F32),
                               jnp.zeros((N_KV_HEADS, rows, HEAD_DIM), F32))
        m_scr[...] = m
        l_scr[...] = l
        acc_scr[...] = acc

    sel = sel_scr[...]
    mask = []
    for j in range(BLOCKS_PER_STEP):
        col = jnp.sum(jnp.where(lane == p * BLOCKS_PER_STEP + j, sel, 0.0), axis=-1, keepdims=True) > 0.5
        mask.append(jnp.broadcast_to(col, (N_KV_HEADS, rows, MOBA_BLOCK)))
    bias = per_head(lambda h: jnp.concatenate([bias_rows(h, p * PAGES_PER_STEP + j)
                                               for j in range(PAGES_PER_STEP)], axis=1))
    m, l, acc = _moba_tile(qs, per_head(lambda h: _head_pages(k_refs, h).astype(BF16)),
                           per_head(lambda h: _head_pages(v_refs, h).astype(BF16)), bias,
                           jnp.concatenate(mask, axis=-1), m_scr[...], l_scr[...], acc_scr[...])
    m_scr[...] = m
    l_scr[...] = l
    acc_scr[...] = acc

    @pl.when(p == pl.num_programs(1) - 1)
    def _():
        gc = ATTN_WIDTH + 2 * KV_WIDTH
        for h in range(N_KV_HEADS):
            sl = slice(h * GROUP * HEAD_DIM, (h + 1) * GROUP * HEAD_DIM)
            gate = proj_ref[:, gc + sl.start:gc + sl.stop]
            o_ref[:, sl] = _unstack_heads(acc_scr[h] / l_scr[h], GROUP) * _silu(gate)


def moba_attention_sample(proj, k_pool, v_pool, page_offset, page_table, rel_bias, tq):
    batch, n_pages = page_table.shape
    assert n_pages % PAGES_PER_STEP == 0 and tq <= MOBA_BLOCK
    rows = GROUP * tq
    nb = n_pages // PAGES_PER_BLOCK
    assert nb <= LANES
    means = moba_block_means(k_pool, page_offset, page_table).reshape(batch, nb, KV_WIDTH)
    dist = ((n_pages - jnp.arange(n_pages + 1))[:, None] * PAGE_SIZE + (PAGE_SIZE - 1)
            - jnp.arange(2 * PAGE_SIZE)[None, :]).reshape(-1)
    table = bias_by_distance(rel_bias, dist).reshape(N_HEADS * (n_pages + 1), 2 * PAGE_SIZE)
    specs = _page_specs(n_pages, page_offset, latest_first=False)
    return pl.pallas_call(
        functools.partial(_moba_sample_kernel, tq=tq, n_pages=n_pages),
        grid_spec=pltpu.PrefetchScalarGridSpec(
            num_scalar_prefetch=1,
            grid=(batch, n_pages // PAGES_PER_STEP),
            in_specs=[pl.BlockSpec((tq, ATTN_IN), lambda b, p, pt: (b, 0)),
                      pl.BlockSpec((1, nb, KV_WIDTH), lambda b, p, pt: (b, 0, 0)),
                      pl.BlockSpec((N_HEADS * (n_pages + 1), 2 * PAGE_SIZE), lambda b, p, pt: (0, 0))]
            + specs + specs,
            out_specs=pl.BlockSpec((tq, ATTN_WIDTH), lambda b, p, pt: (b, 0)),
            scratch_shapes=[pltpu.VMEM((N_KV_HEADS, rows, LANES), F32),
                            pltpu.VMEM((N_KV_HEADS, rows, 1), F32),
                            pltpu.VMEM((N_KV_HEADS, rows, 1), F32),
                            pltpu.VMEM((N_KV_HEADS, rows, HEAD_DIM), F32)]),
        out_shape=jax.ShapeDtypeStruct((batch * tq, ATTN_WIDTH), F32),
        compiler_params=_cparams("parallel", "arbitrary"),
        name="moba_attention_sample",
    )(page_table, proj, means, table, *([k_pool] * PAGES_PER_STEP), *([v_pool] * PAGES_PER_STEP))


N_MIXERS = 3
GDN_IN_PADDED = -(-GDN_IN // 512) * 512


def _new_kv(proj, batch, seq):
    k = proj[:, ATTN_WIDTH:ATTN_WIDTH + KV_WIDTH].reshape(batch, seq, N_KV_HEADS, HEAD_DIM)
    v = proj[:, ATTN_WIDTH + KV_WIDTH:ATTN_WIDTH + 2 * KV_WIDTH].reshape(batch, seq, N_KV_HEADS, HEAD_DIM)
    return k, v


def kernel(x_prompt, x_sample, cache_sb_k, cache_sb_v, state_gdn_conv, state_gdn_rec, cache_moba_k, cache_moba_v, page_table, norm_g, sb_w_in, sb_w_out, gdn_w_in, gdn_conv_w, gdn_a_log, gdn_dt_bias, gdn_o_norm, gdn_w_out, moba_w_in, moba_q_norm, moba_k_norm, moba_w_out, rel_bias):
    bp, tp, d = x_prompt.shape
    bs, ts, _ = x_sample.shape
    n_phys = cache_sb_k.shape[1]
    yp = x_prompt.reshape(bp * tp, d)
    ys = x_sample.reshape(bs * ts, d)
    outs = {name: [] for name in ("sb_kp", "sb_vp", "sb_ks", "sb_vs", "gdn_cp", "gdn_sp", "gdn_cs", "gdn_ss",
                                  "mb_kp", "mb_vp", "mb_ks", "mb_vs")}
    for layer in range(norm_g.shape[0]):
        kind = layer % N_MIXERS
        j = layer // N_MIXERS
        g = norm_g[layer]
        if kind == 0:
            w_in = sb_w_in[j].astype(BF16)
            w_out = sb_w_out[j].astype(BF16)
            pp = norm_matmul(yp, g, w_in)
            ps = norm_matmul(ys, g, w_in)
            op = sb_attention_prompt(pp, bp, tp)
            os_ = sb_attention_sample(ps, _flat_pool(cache_sb_k), _flat_pool(cache_sb_v), j * n_phys, page_table, ts)
            kp, vp = _new_kv(pp, bp, tp)
            ks, vs = _new_kv(ps, bs, ts)
            outs["sb_kp"].append(kp); outs["sb_vp"].append(vp); outs["sb_ks"].append(ks); outs["sb_vs"].append(vs)
        elif kind == 1:
            w_in = jnp.pad(gdn_w_in[j], ((0, 0), (0, GDN_IN_PADDED - GDN_IN))).astype(BF16)
            w_out = gdn_w_out[j].astype(BF16)
            pp = norm_matmul(yp, g, w_in)
            ps = norm_matmul(ys, g, w_in)
            conv0 = jnp.zeros((bp, GDN_CONV - 1, GDN_CONV_CH), F32)
            s0 = jnp.zeros((bp,) + state_gdn_rec.shape[2:], F32)
            op, sp = gdn_delta(pp, conv0, gdn_conv_w[j], gdn_a_log[j], gdn_dt_bias[j], gdn_o_norm[j], s0,
                               bp, tp, BF16)
            os_, ss = gdn_delta(ps, state_gdn_conv[j], gdn_conv_w[j], gdn_a_log[j], gdn_dt_bias[j],
                                gdn_o_norm[j], state_gdn_rec[j], bs, ts, F32)
            outs["gdn_cp"].append(pp.reshape(bp, tp, -1)[:, tp - (GDN_CONV - 1):, :GDN_CONV_CH])
            outs["gdn_cs"].append(ps.reshape(bs, ts, -1)[:, ts - (GDN_CONV - 1):, :GDN_CONV_CH])
            outs["gdn_sp"].append(sp); outs["gdn_ss"].append(ss)
        else:
            w_in = moba_w_in[j].astype(BF16)
            w_out = moba_w_out[j].astype(BF16)
            head_gain = jnp.concatenate([jnp.tile(moba_q_norm[j], N_HEADS), jnp.tile(moba_k_norm[j], N_KV_HEADS),
                                         jnp.ones((ATTN_IN - ATTN_WIDTH - KV_WIDTH,), F32)]).reshape(1, ATTN_IN)
            pp = norm_matmul(yp, g, w_in, head_gain, ATTN_WIDTH + KV_WIDTH)
            ps = norm_matmul(ys, g, w_in, head_gain, ATTN_WIDTH + KV_WIDTH)
            op = moba_attention_prompt(pp, rel_bias, bp, tp)
            os_ = moba_attention_sample(ps, _flat_pool(cache_moba_k), _flat_pool(cache_moba_v), j * n_phys,
                                        page_table, rel_bias, ts)
            kp, vp = _new_kv(pp, bp, tp)
            ks, vs = _new_kv(ps, bs, ts)
            outs["mb_kp"].append(kp); outs["mb_vp"].append(vp); outs["mb_ks"].append(ks); outs["mb_vs"].append(vs)
        yp = matmul_residual(op, w_out, yp)
        ys = matmul_residual(os_, w_out, ys)
    stack = lambda name: jnp.stack(outs[name])
    return (yp.reshape(bp, tp, d), ys.reshape(bs, ts, d),
            stack("sb_kp"), stack("sb_vp"), stack("sb_ks"), stack("sb_vs"),
            stack("gdn_cp"), stack("gdn_sp"), stack("gdn_cs"), stack("gdn_ss"),
            stack("mb_kp"), stack("mb_vp"), stack("mb_ks"), stack("mb_vs"))
```

```python
import functools
import math

import jax
import jax.numpy as jnp
from jax import lax
from jax.experimental import pallas as pl
from jax.experimental.pallas import tpu as pltpu

F32 = jnp.float32
BF16 = jnp.bfloat16
HI = lax.Precision.HIGHEST

LANES = 128
SUBLANES = 8
VMEM_LIMIT = 56 * 1024 * 1024

HEAD_DIM = 128
N_HEADS = 16
N_KV_HEADS = 4
GROUP = N_HEADS // N_KV_HEADS
ATTN_WIDTH = N_HEADS * HEAD_DIM
KV_WIDTH = N_KV_HEADS * HEAD_DIM
ATTN_IN = 2 * ATTN_WIDTH + 2 * KV_WIDTH
Q_COL = 0
K_COL = ATTN_WIDTH // HEAD_DIM
V_COL = K_COL + N_KV_HEADS
GATE_COL = (ATTN_WIDTH + 2 * KV_WIDTH) // (GROUP * HEAD_DIM)
PAGE_SIZE = 128

GDN_HEAD_DIM = 128
GDN_K_HEADS = 16
GDN_V_HEADS = 32
GDN_KEY_WIDTH = GDN_K_HEADS * GDN_HEAD_DIM
GDN_VAL_WIDTH = GDN_V_HEADS * GDN_HEAD_DIM
GDN_CONV_CH = 2 * GDN_KEY_WIDTH + GDN_VAL_WIDTH
GDN_CONV = 4
GDN_CHUNK = 64
GDN_IN = GDN_CONV_CH + GDN_VAL_WIDTH + 2 * GDN_V_HEADS
GDN_Z_COL = GDN_CONV_CH // GDN_HEAD_DIM
GDN_BD_COL = (GDN_CONV_CH + GDN_VAL_WIDTH) // LANES

MOBA_BLOCK = 256
MOBA_TOPK = 3
REL_BUCKETS = 32
REL_MAX_DIST = 4096

EPS = 1e-6
NEG = -1e30
SCALE = HEAD_DIM ** -0.5

_NT = (((1,), (1,)), ((), ()))


def _cparams(*sem):
    return pltpu.CompilerParams(dimension_semantics=sem, vmem_limit_bytes=VMEM_LIMIT)


def _softplus(z):
    return jnp.maximum(z, 0.0) + jnp.log1p(jnp.exp(-jnp.abs(z)))


def _silu(x):
    return x * jax.nn.sigmoid(x)


def _dot(a, b, precision=None):
    return jnp.dot(a, b, preferred_element_type=F32, precision=precision)


def _dot_nt(a, b, precision=None):
    return lax.dot_general(a, b, _NT, preferred_element_type=F32, precision=precision)


def _split_bf16(x):
    hi = x.astype(BF16)
    return hi, (x - hi.astype(F32)).astype(BF16)


def _bdot(a, b):
    return jnp.einsum("bij,bjk->bik", a, b, preferred_element_type=F32)


def _bdot_split(a, b):
    return _bdot(a[0], b[0]) + (_bdot(a[0], b[1]) + _bdot(a[1], b[0]))


def _norm_matmul_kernel(x_ref, g_ref, w_ref, hg_ref, o_ref, h_scr, *, n_norm_tiles, tn):
    j = pl.program_id(1)

    @pl.when(j == 0)
    def _():
        x = x_ref[...]
        ms = jnp.mean(x * x, axis=-1, keepdims=True)
        h_scr[...] = (x * lax.rsqrt(ms + EPS) * g_ref[...]).astype(BF16)

    acc = _dot(h_scr[...], w_ref[...])
    if n_norm_tiles == 0:
        o_ref[...] = acc
    else:
        @pl.when(j < n_norm_tiles)
        def _():
            for s in range(tn // HEAD_DIM):
                sl = slice(s * HEAD_DIM, (s + 1) * HEAD_DIM)
                a = acc[:, sl]
                ms = jnp.mean(a * a, axis=-1, keepdims=True)
                o_ref[:, sl] = a * lax.rsqrt(ms + EPS) * hg_ref[:, sl]

        @pl.when(j >= n_norm_tiles)
        def _():
            o_ref[...] = acc


def norm_matmul(x, g, w, head_gain=None, n_norm_cols=0, tn=512):
    m, d = x.shape
    n = w.shape[1]
    tm = min(m, 1024)
    assert m % tm == 0 and n % tn == 0 and n_norm_cols % tn == 0
    if head_gain is None:
        head_gain = jnp.ones((1, n), F32)
    return pl.pallas_call(
        functools.partial(_norm_matmul_kernel, n_norm_tiles=n_norm_cols // tn, tn=tn),
        grid=(m // tm, n // tn),
        in_specs=[pl.BlockSpec((tm, d), lambda i, j: (i, 0)),
                  pl.BlockSpec((1, d), lambda i, j: (0, 0)),
                  pl.BlockSpec((d, tn), lambda i, j: (0, j)),
                  pl.BlockSpec((1, tn), lambda i, j: (0, j))],
        out_specs=pl.BlockSpec((tm, tn), lambda i, j: (i, j)),
        out_shape=jax.ShapeDtypeStruct((m, n), F32),
        scratch_shapes=[pltpu.VMEM((tm, d), BF16)],
        compiler_params=_cparams("parallel", "arbitrary"),
        name="norm_matmul",
    )(x, g.reshape(1, d), w, head_gain)


def _matmul_residual_kernel(a_ref, w_ref, r_ref, o_ref):
    o_ref[...] = r_ref[...] + _dot(a_ref[...].astype(BF16), w_ref[...])


def matmul_residual(a, w, res, tn=512):
    m, k = a.shape
    n = w.shape[1]
    tm = min(m, 1024)
    assert m % tm == 0 and n % tn == 0
    return pl.pallas_call(
        _matmul_residual_kernel,
        grid=(m // tm, n // tn),
        in_specs=[pl.BlockSpec((tm, k), lambda i, j: (i, 0)),
                  pl.BlockSpec((k, tn), lambda i, j: (0, j)),
                  pl.BlockSpec((tm, tn), lambda i, j: (i, j))],
        out_specs=pl.BlockSpec((tm, tn), lambda i, j: (i, j)),
        out_shape=jax.ShapeDtypeStruct((m, n), F32),
        compiler_params=_cparams("parallel", "parallel"),
        name="matmul_residual",
    )(a, w, res)


def _stack_heads(x, n):
    return jnp.concatenate([x[:, g * HEAD_DIM:(g + 1) * HEAD_DIM] for g in range(n)], axis=0)


def _unstack_heads(x, n):
    t = x.shape[0] // n
    return jnp.concatenate([x[g * t:(g + 1) * t] for g in range(n)], axis=1)


def _later_matrix(tk):
    r = lax.broadcasted_iota(jnp.int32, (2 * tk, tk), 0) & (tk - 1)
    c = lax.broadcasted_iota(jnp.int32, (2 * tk, tk), 1)
    return jnp.where(r > c, -1.0, 0.0).astype(BF16)


def _sb_block(qs, k, v, carry, acc, later_mat, mask=None):
    tk = later_mat.shape[1]
    n = k.shape[-2] // tk
    batched = qs.ndim == 3
    z = jnp.einsum("hrd,hsd->hrs", qs, k, preferred_element_type=F32) if batched else _dot_nt(qs, k)
    if mask is not None:
        z = jnp.where(mask, z, NEG)
    sp = jnp.maximum(z, 0.0) + jnp.log(1.0 + jnp.exp(-jnp.abs(z)))
    hi = sp.astype(BF16)
    lo = (sp - hi.astype(F32)).astype(BF16)
    later, total = [], []
    for j in range(n):
        sl = slice(j * tk, (j + 1) * tk)
        split = jnp.concatenate([hi[..., sl], lo[..., sl]], axis=-1)
        lt = _dot(split.reshape(-1, 2 * tk), later_mat).reshape(z.shape[:-1] + (tk,))
        later.append(lt)
        total.append(lt[..., :1] - sp[..., j * tk:j * tk + 1])
    for j in reversed(range(n)):
        later[j] = later[j] + carry
        carry = carry + total[j]
    later = later[0] if n == 1 else jnp.concatenate(later, axis=-1)
    w = jnp.exp(z - sp + later).astype(BF16)
    pv = jnp.einsum("hrs,hsd->hrd", w, v, preferred_element_type=F32) if batched else _dot(w, v)
    return carry, acc + pv


SB_BLOCKS_PER_STEP = 4


def _sb_prompt_kernel(q_ref, k_ref, v_ref, gate_ref, o_ref, carry_scr, acc_scr, *, tq):
    qi = pl.program_id(2)
    rows = GROUP * tq
    nk = SB_BLOCKS_PER_STEP
    qs = (_stack_heads(q_ref[...], GROUP) * SCALE).astype(BF16)
    later_mat = _later_matrix(tq)
    q_pos = qi * tq + (lax.broadcasted_iota(jnp.int32, (rows, 1), 0) & (tq - 1))

    def run(first_blk, n, carry, acc, masked):
        start = pl.multiple_of(first_blk * tq, tq)
        k = k_ref[pl.ds(start, n * tq), :].astype(BF16)
        v = v_ref[pl.ds(start, n * tq), :].astype(BF16)
        mask = None
        if masked:
            mask = lax.broadcasted_iota(jnp.int32, (rows, n * tq), 1) + start < q_pos
        c, a = _sb_block(qs, k, v, carry, acc, later_mat, mask)
        carry_scr[...] = c
        acc_scr[...] = a

    n_first = qi % nk + 1
    for n in range(1, nk + 1):
        @pl.when(n_first == n)
        def _():
            run(qi + 1 - n, n, jnp.zeros((rows, 1), F32), jnp.zeros((rows, HEAD_DIM), F32), masked=True)

    def body(i, _):
        run(qi + 1 - n_first - nk * (i + 1), nk, carry_scr[...], acc_scr[...], masked=False)
        return 0

    lax.fori_loop(0, (qi + 1 - n_first) // nk, body, 0)
    o = _unstack_heads(acc_scr[...], GROUP)
    o_ref[...] = (o * _silu(gate_ref[...])).astype(o_ref.dtype)


def sb_attention_prompt(proj, batch, seq, tq=128):
    assert seq % tq == 0 and seq >= SB_BLOCKS_PER_STEP * tq
    nq = seq // tq
    gw = GROUP * HEAD_DIM
    return pl.pallas_call(
        functools.partial(_sb_prompt_kernel, tq=tq),
        grid=(batch, N_KV_HEADS, nq),
        in_specs=[pl.BlockSpec((tq, gw), lambda b, h, i: (b * nq + i, h)),
                  pl.BlockSpec((seq, HEAD_DIM), lambda b, h, i: (b, K_COL + h)),
                  pl.BlockSpec((seq, HEAD_DIM), lambda b, h, i: (b, V_COL + h)),
                  pl.BlockSpec((tq, gw), lambda b, h, i: (b * nq + i, GATE_COL + h))],
        out_specs=pl.BlockSpec((tq, gw), lambda b, h, i: (b * nq + i, h)),
        out_shape=jax.ShapeDtypeStruct((batch * seq, ATTN_WIDTH), BF16),
        scratch_shapes=[pltpu.VMEM((GROUP * tq, 1), F32), pltpu.VMEM((GROUP * tq, HEAD_DIM), F32)],
        compiler_params=_cparams("parallel", "parallel", "parallel"),
        name="sb_attention_prompt",
    )(proj, proj, proj, proj)


def _pad_rows(x, rows):
    return jnp.concatenate([x, jnp.zeros((rows - x.shape[0], x.shape[1]), x.dtype)], axis=0)


PAGES_PER_STEP = 16


def _page_specs(n_pages, page_offset, latest_first):
    n_steps = n_pages // PAGES_PER_STEP

    def index_map(b, p, pt, *, j):
        step = n_steps - 1 - p if latest_first else p
        return (page_offset + pt[b, step * PAGES_PER_STEP + j], 0, 0)

    return [pl.BlockSpec((1, PAGE_SIZE * N_KV_HEADS, HEAD_DIM), functools.partial(index_map, j=j))
            for j in range(PAGES_PER_STEP)]


def _head_pages(page_refs, h):
    return jnp.concatenate([r[0, pl.ds(h, PAGE_SIZE, stride=N_KV_HEADS), :] for r in page_refs], axis=0)


def _flat_pool(cache):
    return cache.reshape(cache.shape[0] * cache.shape[1], PAGE_SIZE * N_KV_HEADS, HEAD_DIM)


def _sb_sample_kernel(pt_ref, proj_ref, *refs, tq):
    del pt_ref
    k_refs, v_refs = refs[:PAGES_PER_STEP], refs[PAGES_PER_STEP:2 * PAGES_PER_STEP]
    o_ref, carry_scr, acc_scr = refs[2 * PAGES_PER_STEP:]
    p = pl.program_id(1)
    rows = GROUP * tq
    later_mat = _later_matrix(PAGE_SIZE)

    def per_head(f):
        return jnp.stack([f(h) for h in range(N_KV_HEADS)])

    def q_rows(h):
        q = proj_ref[:, h * GROUP * HEAD_DIM:(h + 1) * GROUP * HEAD_DIM]
        return (_stack_heads(q, GROUP) * SCALE).astype(BF16)

    def new_rows(col):
        return per_head(lambda h: _pad_rows(proj_ref[:, col + h * HEAD_DIM:col + (h + 1) * HEAD_DIM],
                                            PAGE_SIZE).astype(BF16))

    qs = per_head(q_rows)

    @pl.when(p == 0)
    def _():
        t_loc = lax.broadcasted_iota(jnp.int32, (rows, PAGE_SIZE), 0) & (tq - 1)
        s_loc = lax.broadcasted_iota(jnp.int32, (rows, PAGE_SIZE), 1)
        c, a = _sb_block(qs, new_rows(ATTN_WIDTH), new_rows(ATTN_WIDTH + KV_WIDTH),
                         jnp.zeros((N_KV_HEADS, rows, 1), F32), jnp.zeros((N_KV_HEADS, rows, HEAD_DIM), F32),
                         later_mat, s_loc < t_loc)
        carry_scr[...] = c
        acc_scr[...] = a

    c, a = _sb_block(qs, per_head(lambda h: _head_pages(k_refs, h).astype(BF16)),
                     per_head(lambda h: _head_pages(v_refs, h).astype(BF16)),
                     carry_scr[...], acc_scr[...], later_mat)
    carry_scr[...] = c
    acc_scr[...] = a

    @pl.when(p == pl.num_programs(1) - 1)
    def _():
        gc = ATTN_WIDTH + 2 * KV_WIDTH
        for h in range(N_KV_HEADS):
            sl = slice(h * GROUP * HEAD_DIM, (h + 1) * GROUP * HEAD_DIM)
            gate = proj_ref[:, gc + sl.start:gc + sl.stop]
            o_ref[:, sl] = _unstack_heads(acc_scr[h], GROUP) * _silu(gate)


def sb_attention_sample(proj, k_pool, v_pool, page_offset, page_table, tq):
    batch, n_pages = page_table.shape
    assert n_pages % PAGES_PER_STEP == 0
    rows = GROUP * tq
    specs = _page_specs(n_pages, page_offset, latest_first=True)
    return pl.pallas_call(
        functools.partial(_sb_sample_kernel, tq=tq),
        grid_spec=pltpu.PrefetchScalarGridSpec(
            num_scalar_prefetch=1,
            grid=(batch, n_pages // PAGES_PER_STEP),
            in_specs=[pl.BlockSpec((tq, ATTN_IN), lambda b, p, pt: (b, 0))] + specs + specs,
            out_specs=pl.BlockSpec((tq, ATTN_WIDTH), lambda b, p, pt: (b, 0)),
            scratch_shapes=[pltpu.VMEM((N_KV_HEADS, rows, 1), F32),
                            pltpu.VMEM((N_KV_HEADS, rows, HEAD_DIM), F32)]),
        out_shape=jax.ShapeDtypeStruct((batch * tq, ATTN_WIDTH), F32),
        compiler_params=_cparams("parallel", "arbitrary"),
        name="sb_attention_sample",
    )(page_table, proj, *([k_pool] * PAGES_PER_STEP), *([v_pool] * PAGES_PER_STEP))


GDN_ROWS = 128
GDN_REP = GDN_V_HEADS // GDN_K_HEADS


def _gdn_delta_kernel(q_ref, k_ref, v_ref, bd_ref, z_ref, cq_ref, ck_ref, cv_ref, wq_ref, wk_ref, wv_ref,
                      alog_ref, dt_ref, onorm_ref, s0_ref,
                      o_ref, s_ref, u_scr, w_scr, in_scr, qd_scr, kdt_scr, gl_scr, gt_scr, *, seq):
    hk = pl.program_id(1)
    c = GDN_ROWS
    n_chunks = max(seq // c, 1)
    unroll = gt_scr.shape[0]
    n_doublings = max(math.ceil(math.log2(min(seq, c))) - 1, 0)
    ri = lax.broadcasted_iota(jnp.int32, (c, c), 0)
    ci = lax.broadcasted_iota(jnp.int32, (c, c), 1)
    lane = lax.broadcasted_iota(jnp.int32, (c, LANES), 1)
    row = lax.broadcasted_iota(jnp.int32, (c, 1), 0)
    tril = ri >= ci
    tril_b = jnp.where(tril, 1.0, 0.0).astype(BF16)
    neg_a = -jnp.exp(alog_ref[...])

    def chunk_rows(i):
        return pl.ds(i * c, c) if isinstance(i, int) else pl.ds(pl.multiple_of(i * c, c), c)

    def rows(ref, i):
        if seq < c:
            return _pad_rows(ref[...], c)
        return ref[chunk_rows(i), :]

    def conv_silu(ref, prev_ref, w_ref, i):
        n = min(seq, c)
        if isinstance(i, int) and i == 0:
            ext = jnp.concatenate([prev_ref[0], ref[pl.ds(0, n), :]], axis=0)
        elif isinstance(i, int):
            ext = ref[pl.ds(i * c - SUBLANES, c + SUBLANES), :]
        else:
            ext = ref[pl.ds(pl.multiple_of(i * c - SUBLANES, SUBLANES), c + SUBLANES), :]
        w = w_ref[...]
        y = ext[SUBLANES:] * w[GDN_CONV - 1:GDN_CONV, :]
        for tap in range(1, GDN_CONV):
            y = y + pltpu.roll(ext, tap, 0)[SUBLANES:] * w[GDN_CONV - 1 - tap:GDN_CONV - tap, :]
        y = _silu(y)
        return y if n == c else _pad_rows(y, c)

    def l2_normalised(x, scale):
        return x * (lax.rsqrt(jnp.sum(x * x, axis=-1, keepdims=True) + EPS) * scale)

    def prepare(i, _):
        ms, rhss = [], []
        for j in range(unroll):
            ch = i * unroll + j
            sl = chunk_rows(ch)
            q = l2_normalised(conv_silu(q_ref, cq_ref, wq_ref, ch), GDN_HEAD_DIM ** -0.5)
            k = l2_normalised(conv_silu(k_ref, ck_ref, wk_ref, ch), 1.0)
            v2 = conv_silu(v_ref, cv_ref, wv_ref, ch)
            raw = rows(bd_ref, ch)
            k16 = k.astype(BF16)
            kk = _dot_nt(k16, k16)
            qk = _dot_nt(q.astype(BF16), k16)
            sig = jax.nn.sigmoid(raw)
            g_all = jnp.where(row < seq, neg_a * _softplus(raw + dt_ref[...]), 0.0)
            g_hi = g_all.astype(BF16)
            g_mid, g_lo = _split_bf16(g_all - g_hi.astype(F32))
            gcx = _dot(tril_b, g_hi) + (_dot(tril_b, g_mid) + _dot(tril_b, g_lo))
            gt_scr[j] = gcx.T
            for e in range(GDN_REP):
                hv = hk * GDN_REP + e
                beta = jnp.sum(jnp.where(lane == hv, sig, 0.0), axis=1, keepdims=True)
                gc = jnp.sum(jnp.where(lane == hv + GDN_V_HEADS, gcx, 0.0), axis=1, keepdims=True)
                g_row = gt_scr[j, pl.ds(hv + GDN_V_HEADS, 1), :]
                decay = jnp.exp(jnp.where(tril, gc - g_row, NEG))
                ms.append(-jnp.where(ri > ci, kk * beta * decay, 0.0))
                egc = jnp.exp(gc)
                v = v2[:, e * GDN_HEAD_DIM:(e + 1) * GDN_HEAD_DIM]
                rhss.append(jnp.concatenate([v * beta, k * (beta * egc)], axis=1))
                g_last = gc[c - 1:c, :]
                in_scr[e, sl, :] = (qk * decay).astype(BF16)
                qd_scr[e, sl, :] = (q * egc).astype(BF16)
                kdt_scr[e, ch] = (k * jnp.exp(g_last - gc)).T.astype(BF16)
                gl_scr[e, ch] = jnp.broadcast_to(jnp.exp(g_last), (SUBLANES, LANES))
        nmat = jnp.stack(ms)
        pw = _split_bf16(nmat)
        for _ in range(n_doublings):
            sq = _bdot_split(pw, pw)
            pw = _split_bf16(sq)
            nmat = nmat + sq + _bdot_split(_split_bf16(nmat), pw)
        rhs = jnp.stack(rhss)
        sol = rhs + _bdot(nmat.astype(BF16), rhs.astype(BF16))
        for j in range(unroll):
            sl = chunk_rows(i * unroll + j)
            for e in range(GDN_REP):
                x = sol[j * GDN_REP + e]
                u_scr[e, sl, :] = x[:, :GDN_HEAD_DIM]
                w_scr[e, sl, :] = x[:, GDN_HEAD_DIM:].astype(BF16)
        return 0

    def advance(i, states):
        sl = chunk_rows(i)
        z2 = rows(z_ref, i)
        out = []
        for e in range(GDN_REP):
            s = states[e]
            s16 = s.astype(BF16)
            v_new = u_scr[e, sl, :] - _dot(w_scr[e, sl, :], s16)
            v16 = v_new.astype(BF16)
            o = _dot(qd_scr[e, sl, :], s16) + _dot(in_scr[e, sl, :], v16)
            out.append(s * gl_scr[e, i][0:1, :] + _dot(kdt_scr[e, i], v16))
            ms = jnp.mean(o * o, axis=-1, keepdims=True)
            z = z2[:, e * GDN_HEAD_DIM:(e + 1) * GDN_HEAD_DIM]
            o = (o * lax.rsqrt(ms + EPS) * onorm_ref[...] * _silu(z)).astype(o_ref.dtype)
            cols = slice(e * GDN_HEAD_DIM, (e + 1) * GDN_HEAD_DIM)
            if seq < c:
                o_ref[:, cols] = o[:seq]
            else:
                o_ref[sl, cols] = o
        return tuple(out)

    states = tuple(s0_ref[0, e] for e in range(GDN_REP))
    if n_chunks == 1:
        prepare(0, 0)
        states = advance(0, states)
    else:
        n_groups = n_chunks // unroll
        prepare(0, 0)

        def body(i, states):
            for j in range(unroll):
                states = advance((i - 1) * unroll + j, states)
            prepare(i, 0)
            return states

        states = lax.fori_loop(1, n_groups, body, states)
        for j in range(unroll):
            states = advance((n_groups - 1) * unroll + j, states)
    for e in range(GDN_REP):
        s_ref[0, e] = states[e]


def gdn_delta(proj, conv_state, conv_w, a_log, dt_bias, o_norm, s0, batch, seq, out_dtype):
    assert (seq % GDN_ROWS == 0 or seq < GDN_ROWS) and seq % SUBLANES == 0
    prev = jnp.pad(conv_state, ((0, 0), (SUBLANES - (GDN_CONV - 1), 0), (0, 0)))
    vcol = 2 * GDN_K_HEADS // GDN_REP
    pad = jnp.zeros((GDN_V_HEADS,), F32)
    tail = jnp.zeros((LANES - 2 * GDN_V_HEADS,), F32)
    alog = jnp.concatenate([pad, a_log, tail]).reshape(1, LANES)
    dt = jnp.concatenate([pad, dt_bias, tail]).reshape(1, LANES)
    blk = (seq, GDN_HEAD_DIM)
    wide = (seq, GDN_REP * GDN_HEAD_DIM)
    state_blk = (1, GDN_REP, GDN_HEAD_DIM, GDN_HEAD_DIM)
    rows = max(seq, GDN_ROWS)
    n_chunks = rows // GDN_ROWS
    return pl.pallas_call(
        functools.partial(_gdn_delta_kernel, seq=seq),
        grid=(batch, GDN_K_HEADS),
        in_specs=[pl.BlockSpec(blk, lambda b, h: (b, h)),
                  pl.BlockSpec(blk, lambda b, h: (b, GDN_K_HEADS + h)),
                  pl.BlockSpec(wide, lambda b, h: (b, vcol + h)),
                  pl.BlockSpec((seq, LANES), lambda b, h: (b, GDN_BD_COL)),
                  pl.BlockSpec(wide, lambda b, h: (b, GDN_Z_COL // GDN_REP + h)),
                  pl.BlockSpec((1, SUBLANES, blk[1]), lambda b, h: (b, 0, h)),
                  pl.BlockSpec((1, SUBLANES, blk[1]), lambda b, h: (b, 0, GDN_K_HEADS + h)),
                  pl.BlockSpec((1, SUBLANES, wide[1]), lambda b, h: (b, 0, vcol + h)),
                  pl.BlockSpec((GDN_CONV, blk[1]), lambda b, h: (0, h)),
                  pl.BlockSpec((GDN_CONV, blk[1]), lambda b, h: (0, GDN_K_HEADS + h)),
                  pl.BlockSpec((GDN_CONV, wide[1]), lambda b, h: (0, vcol + h)),
                  pl.BlockSpec((1, LANES), lambda b, h: (0, 0)),
                  pl.BlockSpec((1, LANES), lambda b, h: (0, 0)),
                  pl.BlockSpec((1, GDN_HEAD_DIM), lambda b, h: (0, 0)),
                  pl.BlockSpec(state_blk, lambda b, h: (b, h, 0, 0))],
        out_specs=[pl.BlockSpec(wide, lambda b, h: (b, h)),
                   pl.BlockSpec(state_blk, lambda b, h: (b, h, 0, 0))],
        scratch_shapes=[pltpu.VMEM((GDN_REP, rows, GDN_HEAD_DIM), F32),
                        pltpu.VMEM((GDN_REP, rows, GDN_HEAD_DIM), BF16),
                        pltpu.VMEM((GDN_REP, rows, GDN_ROWS), BF16),
                        pltpu.VMEM((GDN_REP, rows, GDN_HEAD_DIM), BF16),
                        pltpu.VMEM((GDN_REP, n_chunks, GDN_HEAD_DIM, GDN_ROWS), BF16),
                        pltpu.VMEM((GDN_REP, n_chunks, SUBLANES, LANES), F32),
                        pltpu.VMEM((2 if n_chunks % 2 == 0 else 1, GDN_ROWS, LANES), F32)],
        out_shape=[jax.ShapeDtypeStruct((batch * seq, GDN_VAL_WIDTH), out_dtype),
                   jax.ShapeDtypeStruct((batch, GDN_V_HEADS, GDN_HEAD_DIM, GDN_HEAD_DIM), F32)],
        compiler_params=_cparams("parallel", "parallel"),
        name="gdn_delta",
    )(proj, proj, proj, proj, proj, prev, prev, prev, conv_w, conv_w, conv_w,
      alog, dt, o_norm.reshape(1, GDN_HEAD_DIM), s0)


def _rel_bucket(dist):
    max_exact = REL_BUCKETS // 2
    n = jnp.maximum(dist, 0)
    large = max_exact + (jnp.log(jnp.maximum(n, 1).astype(F32) / max_exact)
                         / math.log(REL_MAX_DIST / max_exact) * (REL_BUCKETS - max_exact)).astype(jnp.int32)
    large = jnp.minimum(large, REL_BUCKETS - 1)
    return jnp.where(n < max_exact, n, large)


def _bias_table_kernel(rbt_ref, onehot_ref, o_ref):
    o_ref[...] = _dot(rbt_ref[...], onehot_ref[...], HI)


def bias_by_distance(rel_bias, dist):
    bucket = _rel_bucket(dist)
    onehot = (bucket[None, :] == jnp.arange(REL_BUCKETS)[:, None]) & (dist[None, :] >= 0)
    return pl.pallas_call(
        _bias_table_kernel,
        out_shape=jax.ShapeDtypeStruct((N_HEADS, dist.shape[0]), F32),
        compiler_params=pltpu.CompilerParams(vmem_limit_bytes=VMEM_LIMIT),
        name="bias_table",
    )(rel_bias.T, onehot.astype(F32))


def _toeplitz(window_row, rows, shift):
    x = jnp.broadcast_to(window_row, (rows, window_row.shape[1]))
    return pltpu.roll(x, shift, 1, stride=1, stride_axis=0)


def _select_topk(gate, n_valid, n_blocks):
    lane = lax.broadcasted_iota(jnp.int32, gate.shape, 1)
    valid = lane < n_valid
    gm = jnp.where(valid, gate, -jnp.inf)
    cnt = jnp.zeros(gate.shape, jnp.int32)
    for m in range(n_blocks):
        col = gm[:, m:m + 1]
        beats = (col > gm) | ((col == gm) & (lane > m))
        cnt = cnt + jnp.where(beats, 1, 0)
    return jnp.where(valid & (cnt < MOBA_TOPK), 1.0, 0.0)


def _select_topk_t(gate_t, n_valid):
    blk = lax.broadcasted_iota(jnp.int32, gate_t.shape, 0)
    valid = blk < n_valid
    gm = jnp.where(valid, gate_t, -jnp.inf)
    cnt = jnp.zeros(gate_t.shape, jnp.int32)
    for m in range(gate_t.shape[0]):
        row = gm[m:m + 1, :]
        beats = (row > gm) | ((row == gm) & (blk > m))
        cnt = cnt + jnp.where(beats, 1, 0)
    return jnp.where(valid & (cnt < MOBA_TOPK), 1.0, 0.0)


def _moba_tile(qs, k, v, bias, mask, m, l, acc):
    batched = qs.ndim == 3
    s = jnp.einsum("hrd,hsd->hrs", qs, k, preferred_element_type=F32) if batched else _dot_nt(qs, k)
    s = jnp.where(mask, s * SCALE + bias, NEG)
    m_new = jnp.maximum(m, jnp.max(s, axis=-1, keepdims=True))
    p = jnp.where(mask, jnp.exp(s - m_new), 0.0)
    alpha = jnp.exp(m - m_new)
    l = alpha * l + jnp.sum(p, axis=-1, keepdims=True)
    p = p.astype(BF16)
    pv = jnp.einsum("hrs,hsd->hrd", p, v, preferred_element_type=F32) if batched else _dot(p, v)
    return m_new, l, alpha * acc + pv


def _moba_prompt_kernel(q_ref, k_ref, v_ref, gate_ref, tb_ref, o_ref,
                        means_scr, bias_scr, pen_scr, m_scr, l_scr, acc_scr, *, tq, n_blocks):
    h = pl.program_id(0)
    b = pl.program_id(1)
    qi = pl.program_id(2)
    rows = GROUP * tq

    @pl.when(qi == 0)
    def _():
        means_scr[...] = jnp.zeros_like(means_scr)
        means_scr[0:n_blocks, :] = jnp.mean(k_ref[...].reshape(n_blocks, MOBA_BLOCK, HEAD_DIM), axis=1)

    @pl.when(b == 0)
    def _():
        for g in range(GROUP):
            wrow = tb_ref[pl.ds((h * GROUP + g) * n_blocks + qi, 1), :]
            bias_scr[qi, pl.ds(g * tq, tq), :] = _toeplitz(wrow, tq, tq + 1)[:, :tq]

    qs32 = _stack_heads(q_ref[...], GROUP)
    qs = (qs32 * SCALE).astype(BF16)
    sel_t = _select_topk_t(_dot_nt(means_scr[...], qs32, HI), qi)
    pen_t = (sel_t - 1.0) * -NEG
    pen_scr[...] = jnp.concatenate([pen_t, jnp.zeros((LANES - pen_t.shape[0], rows), F32)], axis=0).T
    lane = lax.broadcasted_iota(jnp.int32, (rows, LANES), 1)
    t_loc = lax.broadcasted_iota(jnp.int32, (rows, tq), 0) & (tq - 1)
    s_loc = lax.broadcasted_iota(jnp.int32, (rows, tq), 1)

    def update(s, v, m, l, acc):
        m_new = jnp.maximum(m, jnp.max(s, axis=1, keepdims=True))
        p = jnp.exp(s - m_new)
        alpha = jnp.exp(m - m_new)
        m_scr[...] = m_new
        l_scr[...] = alpha * l + jnp.sum(p, axis=1, keepdims=True)
        acc_scr[...] = alpha * acc + _dot(p.astype(BF16), v)

    def penalty(kb):
        return jnp.sum(jnp.where(lane == kb, pen_scr[...], 0.0), axis=1, keepdims=True)

    def first(n):
        blk = qi + 1 - n
        start = pl.multiple_of(blk * tq, tq)
        z = _dot_nt(qs, k_ref[pl.ds(start, n * tq), :].astype(BF16))
        s = jnp.where(s_loc <= t_loc, z[:, (n - 1) * tq:] + bias_scr[0], NEG)
        if n == 2:
            s = jnp.concatenate([z[:, :tq] + bias_scr[1] + penalty(blk), s], axis=1)
        update(s, v_ref[pl.ds(start, n * tq), :].astype(BF16), jnp.full((rows, 1), NEG, F32),
               jnp.zeros((rows, 1), F32), jnp.zeros((rows, HEAD_DIM), F32))

    for n in (1, 2):
        @pl.when(qi % 2 == n - 1)
        def _():
            first(n)

    def pair(i, _):
        kb = 2 * i
        start = pl.multiple_of(kb * tq, tq)
        z = _dot_nt(qs, k_ref[pl.ds(start, 2 * tq), :].astype(BF16))
        s = jnp.concatenate([z[:, :tq] + bias_scr[qi - kb] + penalty(kb),
                             z[:, tq:] + bias_scr[qi - kb - 1] + penalty(kb + 1)], axis=1)
        update(s, v_ref[pl.ds(start, 2 * tq), :].astype(BF16), m_scr[...], l_scr[...], acc_scr[...])
        return 0

    lax.fori_loop(0, qi // 2, pair, 0)
    o = _unstack_heads(acc_scr[...] / l_scr[...], GROUP)
    o_ref[...] = (o * _silu(gate_ref[...])).astype(o_ref.dtype)


def moba_attention_prompt(proj, rel_bias, batch, seq):
    tq = MOBA_BLOCK
    assert seq % tq == 0
    nb = seq // tq
    gw = GROUP * HEAD_DIM
    rows = GROUP * tq
    dist = (jnp.arange(nb)[:, None] * tq + (tq - 1) - jnp.arange(2 * tq)[None, :]).reshape(-1)
    table = bias_by_distance(rel_bias, dist).reshape(N_HEADS * nb, 2 * tq)
    return pl.pallas_call(
        functools.partial(_moba_prompt_kernel, tq=tq, n_blocks=nb),
        grid=(N_KV_HEADS, batch, nb),
        in_specs=[pl.BlockSpec((tq, gw), lambda h, b, i: (b * nb + i, h)),
                  pl.BlockSpec((seq, HEAD_DIM), lambda h, b, i: (b, K_COL + h)),
                  pl.BlockSpec((seq, HEAD_DIM), lambda h, b, i: (b, V_COL + h)),
                  pl.BlockSpec((tq, gw), lambda h, b, i: (b * nb + i, GATE_COL + h)),
                  pl.BlockSpec((N_HEADS * nb, 2 * tq), lambda h, b, i: (0, 0))],
        out_specs=pl.BlockSpec((tq, gw), lambda h, b, i: (b * nb + i, h)),
        out_shape=jax.ShapeDtypeStruct((batch * seq, ATTN_WIDTH), BF16),
        scratch_shapes=[pltpu.VMEM((-(-nb // SUBLANES) * SUBLANES, HEAD_DIM), F32),
                        pltpu.VMEM((nb, rows, tq), F32),
                        pltpu.VMEM((rows, LANES), F32),
                        pltpu.VMEM((rows, 1), F32),
                        pltpu.VMEM((rows, 1), F32),
                        pltpu.VMEM((rows, HEAD_DIM), F32)],
        compiler_params=_cparams("arbitrary", "arbitrary", "arbitrary"),
        name="moba_attention_prompt",
    )(proj, proj, proj, proj, table)


PAGES_PER_BLOCK = MOBA_BLOCK // PAGE_SIZE
BLOCKS_PER_STEP = PAGES_PER_STEP // PAGES_PER_BLOCK


def _moba_means_kernel(pt_ref, *refs):
    del pt_ref
    page_refs, o_ref = refs[:-1], refs[-1]
    for blk in range(BLOCKS_PER_STEP):
        pages = page_refs[blk * PAGES_PER_BLOCK:(blk + 1) * PAGES_PER_BLOCK]
        for h in range(N_KV_HEADS):
            total = jnp.sum(_head_pages(pages, h), axis=0, keepdims=True)
            o_ref[0, blk, :, h * HEAD_DIM:(h + 1) * HEAD_DIM] = total * (1.0 / MOBA_BLOCK)


def moba_block_means(k_pool, page_offset, page_table):
    batch, n_pages = page_table.shape
    assert n_pages % PAGES_PER_STEP == 0
    nb = n_pages // PAGES_PER_BLOCK
    return pl.pallas_call(
        _moba_means_kernel,
        grid_spec=pltpu.PrefetchScalarGridSpec(
            num_scalar_prefetch=1,
            grid=(batch, n_pages // PAGES_PER_STEP),
            in_specs=_page_specs(n_pages, page_offset, latest_first=False),
            out_specs=pl.BlockSpec((1, BLOCKS_PER_STEP, 1, KV_WIDTH), lambda b, n, pt: (b, n, 0, 0))),
        out_shape=jax.ShapeDtypeStruct((batch, nb, 1, KV_WIDTH), F32),
        compiler_params=_cparams("parallel", "parallel"),
        name="moba_block_means",
    )(page_table, *([k_pool] * PAGES_PER_STEP))


def _moba_sample_kernel(pt_ref, proj_ref, means_ref, tb_ref, *refs, tq, n_pages):
    del pt_ref
    k_refs, v_refs = refs[:PAGES_PER_STEP], refs[PAGES_PER_STEP:2 * PAGES_PER_STEP]
    o_ref, sel_scr, m_scr, l_scr, acc_scr = refs[2 * PAGES_PER_STEP:]
    p = pl.program_id(1)
    rows = GROUP * tq
    n_past_blocks = n_pages // PAGES_PER_BLOCK
    lane = lax.broadcasted_iota(jnp.int32, (N_KV_HEADS, rows, LANES), 2)

    def q_rows(h):
        return _stack_heads(proj_ref[:, h * GROUP * HEAD_DIM:(h + 1) * GROUP * HEAD_DIM], GROUP)

    def bias_rows(h, page):
        tiles = []
        for g in range(GROUP):
            wrow = tb_ref[pl.ds((h * GROUP + g) * (n_pages + 1) + page, 1), :]
            tiles.append(_toeplitz(wrow, tq, PAGE_SIZE + 1)[:, :PAGE_SIZE])
        return jnp.concatenate(tiles, axis=0)

    def per_head(f):
        return jnp.stack([f(h) for h in range(N_KV_HEADS)])

    def new_rows(col):
        return per_head(lambda h: _pad_rows(proj_ref[:, col + h * HEAD_DIM:col + (h + 1) * HEAD_DIM],
                                            PAGE_SIZE).astype(BF16))

    qs = per_head(lambda h: q_rows(h).astype(BF16))

    @pl.when(p == 0)
    def _():
        t_loc = lax.broadcasted_iota(jnp.int32, (rows, PAGE_SIZE), 0) & (tq - 1)
        s_loc = lax.broadcasted_iota(jnp.int32, (rows, PAGE_SIZE), 1)
        for h in range(N_KV_HEADS):
            means = _pad_rows(means_ref[0, :, h * HEAD_DIM:(h + 1) * HEAD_DIM], LANES)
            sel_scr[h] = _select_topk(_dot_nt(q_rows(h), means, HI), n_past_blocks, n_past_blocks)
        m, l, acc = _moba_tile(qs, new_rows(ATTN_WIDTH), new_rows(ATTN_WIDTH + KV_WIDTH),
                               per_head(lambda h: bias_rows(h, n_pages)), s_loc <= t_loc,
                               jnp.full((N_KV_HEADS, rows, 1), NEG, F32), jnp.zeros((N_KV_HEADS, rows, 1), F32),
                               jnp.zeros((N_KV_HEADS, rows, HEAD_DIM), F32))
        m_scr[...] = m
        l_scr[...] = l
        acc_scr[...] = acc

    sel = sel_scr[...]
    mask = []
    for j in range(BLOCKS_PER_STEP):
        col = jnp.sum(jnp.where(lane == p * BLOCKS_PER_STEP + j, sel, 0.0), axis=-1, keepdims=True) > 0.5
        mask.append(jnp.broadcast_to(col, (N_KV_HEADS, rows, MOBA_BLOCK)))
    bias = per_head(lambda h: jnp.concatenate([bias_rows(h, p * PAGES_PER_STEP + j)
                                               for j in range(PAGES_PER_STEP)], axis=1))
    m, l, acc = _moba_tile(qs, per_head(lambda h: _head_pages(k_refs, h).astype(BF16)),
                           per_head(lambda h: _head_pages(v_refs, h).astype(BF16)), bias,
                           jnp.concatenate(mask, axis=-1), m_scr[...], l_scr[...], acc_scr[...])
    m_scr[...] = m
    l_scr[...] = l
    acc_scr[...] = acc

    @pl.when(p == pl.num_programs(1) - 1)
    def _():
        gc = ATTN_WIDTH + 2 * KV_WIDTH
        for h in range(N_KV_HEADS):
            sl = slice(h * GROUP * HEAD_DIM, (h + 1) * GROUP * HEAD_DIM)
            gate = proj_ref[:, gc + sl.start:gc + sl.stop]
            o_ref[:, sl] = _unstack_heads(acc_scr[h] / l_scr[h], GROUP) * _silu(gate)


def moba_attention_sample(proj, k_pool, v_pool, page_offset, page_table, rel_bias, tq):
    batch, n_pages = page_table.shape
    assert n_pages % PAGES_PER_STEP == 0 and tq <= MOBA_BLOCK
    rows = GROUP * tq
    nb = n_pages // PAGES_PER_BLOCK
    assert nb <= LANES
    means = moba_block_means(k_pool, page_offset, page_table).reshape(batch, nb, KV_WIDTH)
    dist = ((n_pages - jnp.arange(n_pages + 1))[:, None] * PAGE_SIZE + (PAGE_SIZE - 1)
            - jnp.arange(2 * PAGE_SIZE)[None, :]).reshape(-1)
    table = bias_by_distance(rel_bias, dist).reshape(N_HEADS * (n_pages + 1), 2 * PAGE_SIZE)
    specs = _page_specs(n_pages, page_offset, latest_first=False)
    return pl.pallas_call(
        functools.partial(_moba_sample_kernel, tq=tq, n_pages=n_pages),
        grid_spec=pltpu.PrefetchScalarGridSpec(
            num_scalar_prefetch=1,
            grid=(batch, n_pages // PAGES_PER_STEP),
            in_specs=[pl.BlockSpec((tq, ATTN_IN), lambda b, p, pt: (b, 0)),
                      pl.BlockSpec((1, nb, KV_WIDTH), lambda b, p, pt: (b, 0, 0)),
                      pl.BlockSpec((N_HEADS * (n_pages + 1), 2 * PAGE_SIZE), lambda b, p, pt: (0, 0))]
            + specs + specs,
            out_specs=pl.BlockSpec((tq, ATTN_WIDTH), lambda b, p, pt: (b, 0)),
            scratch_shapes=[pltpu.VMEM((N_KV_HEADS, rows, LANES), F32),
                            pltpu.VMEM((N_KV_HEADS, rows, 1), F32),
                            pltpu.VMEM((N_KV_HEADS, rows, 1), F32),
                            pltpu.VMEM((N_KV_HEADS, rows, HEAD_DIM), F32)]),
        out_shape=jax.ShapeDtypeStruct((batch * tq, ATTN_WIDTH), F32),
        compiler_params=_cparams("parallel", "arbitrary"),
        name="moba_attention_sample",
    )(page_table, proj, means, table, *([k_pool] * PAGES_PER_STEP), *([v_pool] * PAGES_PER_STEP))


N_MIXERS = 3
GDN_IN_PADDED = -(-GDN_IN // 512) * 512


def _new_kv(proj, batch, seq):
    k = proj[:, ATTN_WIDTH:ATTN_WIDTH + KV_WIDTH].reshape(batch, seq, N_KV_HEADS, HEAD_DIM)
    v = proj[:, ATTN_WIDTH + KV_WIDTH:ATTN_WIDTH + 2 * KV_WIDTH].reshape(batch, seq, N_KV_HEADS, HEAD_DIM)
    return k, v


def kernel(x_prompt, x_sample, cache_sb_k, cache_sb_v, state_gdn_conv, state_gdn_rec, cache_moba_k, cache_moba_v, page_table, norm_g, sb_w_in, sb_w_out, gdn_w_in, gdn_conv_w, gdn_a_log, gdn_dt_bias, gdn_o_norm, gdn_w_out, moba_w_in, moba_q_norm, moba_k_norm, moba_w_out, rel_bias):
    bp, tp, d = x_prompt.shape
    bs, ts, _ = x_sample.shape
    n_phys = cache_sb_k.shape[1]
    yp = x_prompt.reshape(bp * tp, d)
    ys = x_sample.reshape(bs * ts, d)
    outs = {name: [] for name in ("sb_kp", "sb_vp", "sb_ks", "sb_vs", "gdn_cp", "gdn_sp", "gdn_cs", "gdn_ss",
                                  "mb_kp", "mb_vp", "mb_ks", "mb_vs")}
    for layer in range(norm_g.shape[0]):
        kind = layer % N_MIXERS
        j = layer // N_MIXERS
        g = norm_g[layer]
        if kind == 0:
            w_in = sb_w_in[j].astype(BF16)
            w_out = sb_w_out[j].astype(BF16)
            pp = norm_matmul(yp, g, w_in)
            ps = norm_matmul(ys, g, w_in)
            op = sb_attention_prompt(pp, bp, tp)
            os_ = sb_attention_sample(ps, _flat_pool(cache_sb_k), _flat_pool(cache_sb_v), j * n_phys, page_table, ts)
            kp, vp = _new_kv(pp, bp, tp)
            ks, vs = _new_kv(ps, bs, ts)
            outs["sb_kp"].append(kp); outs["sb_vp"].append(vp); outs["sb_ks"].append(ks); outs["sb_vs"].append(vs)
        elif kind == 1:
            w_in = jnp.pad(gdn_w_in[j], ((0, 0), (0, GDN_IN_PADDED - GDN_IN))).astype(BF16)
            w_out = gdn_w_out[j].astype(BF16)
            pp = norm_matmul(yp, g, w_in)
            ps = norm_matmul(ys, g, w_in)
            conv0 = jnp.zeros((bp, GDN_CONV - 1, GDN_CONV_CH), F32)
            s0 = jnp.zeros((bp,) + state_gdn_rec.shape[2:], F32)
            op, sp = gdn_delta(pp, conv0, gdn_conv_w[j], gdn_a_log[j], gdn_dt_bias[j], gdn_o_norm[j], s0,
                               bp, tp, BF16)
            os_, ss = gdn_delta(ps, state_gdn_conv[j], gdn_conv_w[j], gdn_a_log[j], gdn_dt_bias[j],
                                gdn_o_norm[j], state_gdn_rec[j], bs, ts, F32)
            outs["gdn_cp"].append(pp.reshape(bp, tp, -1)[:, tp - (GDN_CONV - 1):, :GDN_CONV_CH])
            outs["gdn_cs"].append(ps.reshape(bs, ts, -1)[:, ts - (GDN_CONV - 1):, :GDN_CONV_CH])
            outs["gdn_sp"].append(sp); outs["gdn_ss"].append(ss)
        else:
            w_in = moba_w_in[j].astype(BF16)
            w_out = moba_w_out[j].astype(BF16)
            head_gain = jnp.concatenate([jnp.tile(moba_q_norm[j], N_HEADS), jnp.tile(moba_k_norm[j], N_KV_HEADS),
                                         jnp.ones((ATTN_IN - ATTN_WIDTH - KV_WIDTH,), F32)]).reshape(1, ATTN_IN)
            pp = norm_matmul(yp, g, w_in, head_gain, ATTN_WIDTH + KV_WIDTH)
            ps = norm_matmul(ys, g, w_in, head_gain, ATTN_WIDTH + KV_WIDTH)
            op = moba_attention_prompt(pp, rel_bias, bp, tp)
            os_ = moba_attention_sample(ps, _flat_pool(cache_moba_k), _flat_pool(cache_moba_v), j * n_phys,
                                        page_table, rel_bias, ts)
            kp, vp = _new_kv(pp, bp, tp)
            ks, vs = _new_kv(ps, bs, ts)
            outs["mb_kp"].append(kp); outs["mb_vp"].append(vp); outs["mb_ks"].append(ks); outs["mb_vs"].append(vs)
        yp = matmul_residual(op, w_out, yp)
        ys = matmul_residual(os_, w_out, ys)
    stack = lambda name: jnp.stack(outs[name])
    return (yp.reshape(bp, tp, d), ys.reshape(bs, ts, d),
            stack("sb_kp"), stack("sb_vp"), stack("sb_ks"), stack("sb_vs"),
            stack("gdn_cp"), stack("gdn_sp"), stack("gdn_cs"), stack("gdn_ss"),
            stack("mb_kp"), stack("mb_vp"), stack("mb_ks"), stack("mb_vs"))
```

```python
import functools
import math

import jax
import jax.numpy as jnp
from jax import lax
from jax.experimental import pallas as pl
from jax.experimental.pallas import tpu as pltpu

F32 = jnp.float32
BF16 = jnp.bfloat16
HI = lax.Precision.HIGHEST

LANES = 128
SUBLANES = 8
VMEM_LIMIT = 56 * 1024 * 1024

HEAD_DIM = 128
N_HEADS = 16
N_KV_HEADS = 4
GROUP = N_HEADS // N_KV_HEADS
ATTN_WIDTH = N_HEADS * HEAD_DIM
KV_WIDTH = N_KV_HEADS * HEAD_DIM
ATTN_IN = 2 * ATTN_WIDTH + 2 * KV_WIDTH
Q_COL = 0
K_COL = ATTN_WIDTH // HEAD_DIM
V_COL = K_COL + N_KV_HEADS
GATE_COL = (ATTN_WIDTH + 2 * KV_WIDTH) // (GROUP * HEAD_DIM)
PAGE_SIZE = 128

GDN_HEAD_DIM = 128
GDN_K_HEADS = 16
GDN_V_HEADS = 32
GDN_KEY_WIDTH = GDN_K_HEADS * GDN_HEAD_DIM
GDN_VAL_WIDTH = GDN_V_HEADS * GDN_HEAD_DIM
GDN_CONV_CH = 2 * GDN_KEY_WIDTH + GDN_VAL_WIDTH
GDN_CONV = 4
GDN_CHUNK = 64
GDN_IN = GDN_CONV_CH + GDN_VAL_WIDTH + 2 * GDN_V_HEADS
GDN_Z_COL = GDN_CONV_CH // GDN_HEAD_DIM
GDN_BD_COL = (GDN_CONV_CH + GDN_VAL_WIDTH) // LANES

MOBA_BLOCK = 256
MOBA_TOPK = 3
REL_BUCKETS = 32
REL_MAX_DIST = 4096

EPS = 1e-6
NEG = -1e30
SCALE = HEAD_DIM ** -0.5

_NT = (((1,), (1,)), ((), ()))


def _cparams(*sem):
    return pltpu.CompilerParams(dimension_semantics=sem, vmem_limit_bytes=VMEM_LIMIT)


def _softplus(z):
    return jnp.maximum(z, 0.0) + jnp.log1p(jnp.exp(-jnp.abs(z)))


def _silu(x):
    return x * jax.nn.sigmoid(x)


def _dot(a, b, precision=None):
    return jnp.dot(a, b, preferred_element_type=F32, precision=precision)


def _dot_nt(a, b, precision=None):
    return lax.dot_general(a, b, _NT, preferred_element_type=F32, precision=precision)


def _split_bf16(x):
    hi = x.astype(BF16)
    return hi, (x - hi.astype(F32)).astype(BF16)


def _bdot(a, b):
    return jnp.einsum("bij,bjk->bik", a, b, preferred_element_type=F32)


def _bdot_split(a, b):
    return _bdot(a[0], b[0]) + (_bdot(a[0], b[1]) + _bdot(a[1], b[0]))


def _norm_matmul_kernel(x_ref, g_ref, w_ref, hg_ref, o_ref, h_scr, *, n_norm_tiles, tn):
    j = pl.program_id(1)

    @pl.when(j == 0)
    def _():
        x = x_ref[...]
        ms = jnp.mean(x * x, axis=-1, keepdims=True)
        h_scr[...] = (x * lax.rsqrt(ms + EPS) * g_ref[...]).astype(BF16)

    acc = _dot(h_scr[...], w_ref[...])
    if n_norm_tiles == 0:
        o_ref[...] = acc
    else:
        @pl.when(j < n_norm_tiles)
        def _():
            for s in range(tn // HEAD_DIM):
                sl = slice(s * HEAD_DIM, (s + 1) * HEAD_DIM)
                a = acc[:, sl]
                ms = jnp.mean(a * a, axis=-1, keepdims=True)
                o_ref[:, sl] = a * lax.rsqrt(ms + EPS) * hg_ref[:, sl]

        @pl.when(j >= n_norm_tiles)
        def _():
            o_ref[...] = acc


def _column_tile(m, *widths):
    if m >= LANES:
        return 512
    return max(t for t in (512, 1024, 1280) if all(w % t == 0 for w in widths))


def norm_matmul(x, g, w, head_gain=None, n_norm_cols=0):
    m, d = x.shape
    n = w.shape[1]
    tm = min(m, 1024)
    tn = _column_tile(m, n, n_norm_cols)
    assert m % tm == 0 and n % tn == 0 and n_norm_cols % tn == 0
    if head_gain is None:
        head_gain = jnp.ones((1, n), F32)
    return pl.pallas_call(
        functools.partial(_norm_matmul_kernel, n_norm_tiles=n_norm_cols // tn, tn=tn),
        grid=(m // tm, n // tn),
        in_specs=[pl.BlockSpec((tm, d), lambda i, j: (i, 0)),
                  pl.BlockSpec((1, d), lambda i, j: (0, 0)),
                  pl.BlockSpec((d, tn), lambda i, j: (0, j)),
                  pl.BlockSpec((1, tn), lambda i, j: (0, j))],
        out_specs=pl.BlockSpec((tm, tn), lambda i, j: (i, j)),
        out_shape=jax.ShapeDtypeStruct((m, n), F32),
        scratch_shapes=[pltpu.VMEM((tm, d), BF16)],
        compiler_params=_cparams("parallel", "arbitrary"),
        name="norm_matmul",
    )(x, g.reshape(1, d), w, head_gain)


def _matmul_residual_kernel(a_ref, w_ref, r_ref, o_ref):
    o_ref[...] = r_ref[...] + _dot(a_ref[...].astype(BF16), w_ref[...])


def matmul_residual(a, w, res):
    m, k = a.shape
    n = w.shape[1]
    tm = min(m, 1024)
    tn = _column_tile(m, n)
    assert m % tm == 0 and n % tn == 0
    return pl.pallas_call(
        _matmul_residual_kernel,
        grid=(m // tm, n // tn),
        in_specs=[pl.BlockSpec((tm, k), lambda i, j: (i, 0)),
                  pl.BlockSpec((k, tn), lambda i, j: (0, j)),
                  pl.BlockSpec((tm, tn), lambda i, j: (i, j))],
        out_specs=pl.BlockSpec((tm, tn), lambda i, j: (i, j)),
        out_shape=jax.ShapeDtypeStruct((m, n), F32),
        compiler_params=_cparams("parallel", "parallel"),
        name="matmul_residual",
    )(a, w, res)


def _stack_heads(x, n):
    return jnp.concatenate([x[:, g * HEAD_DIM:(g + 1) * HEAD_DIM] for g in range(n)], axis=0)


def _unstack_heads(x, n):
    t = x.shape[0] // n
    return jnp.concatenate([x[g * t:(g + 1) * t] for g in range(n)], axis=1)


def _later_matrix(tk):
    r = lax.broadcasted_iota(jnp.int32, (2 * tk, tk), 0) & (tk - 1)
    c = lax.broadcasted_iota(jnp.int32, (2 * tk, tk), 1)
    return jnp.where(r > c, -1.0, 0.0).astype(BF16)


def _sb_block(qs, k, v, carry, acc, later_mat, mask=None):
    tk = later_mat.shape[1]
    n = k.shape[-2] // tk
    batched = qs.ndim == 3
    z = jnp.einsum("hrd,hsd->hrs", qs, k, preferred_element_type=F32) if batched else _dot_nt(qs, k)
    if mask is not None:
        z = jnp.where(mask, z, NEG)
    sp = jnp.maximum(z, 0.0) + jnp.log(1.0 + jnp.exp(-jnp.abs(z)))
    hi = sp.astype(BF16)
    lo = (sp - hi.astype(F32)).astype(BF16)
    later, total = [], []
    for j in range(n):
        sl = slice(j * tk, (j + 1) * tk)
        split = jnp.concatenate([hi[..., sl], lo[..., sl]], axis=-1)
        lt = _dot(split.reshape(-1, 2 * tk), later_mat).reshape(z.shape[:-1] + (tk,))
        later.append(lt)
        total.append(lt[..., :1] - sp[..., j * tk:j * tk + 1])
    for j in reversed(range(n)):
        later[j] = later[j] + carry
        carry = carry + total[j]
    later = later[0] if n == 1 else jnp.concatenate(later, axis=-1)
    w = jnp.exp(z - sp + later).astype(BF16)
    pv = jnp.einsum("hrs,hsd->hrd", w, v, preferred_element_type=F32) if batched else _dot(w, v)
    return carry, acc + pv


SB_BLOCKS_PER_STEP = 4


def _sb_prompt_kernel(q_ref, k_ref, v_ref, gate_ref, o_ref, carry_scr, acc_scr, *, tq):
    qi = pl.program_id(2)
    rows = GROUP * tq
    nk = SB_BLOCKS_PER_STEP
    qs = (_stack_heads(q_ref[...], GROUP) * SCALE).astype(BF16)
    later_mat = _later_matrix(tq)
    q_pos = qi * tq + (lax.broadcasted_iota(jnp.int32, (rows, 1), 0) & (tq - 1))

    def run(first_blk, n, carry, acc, masked):
        start = pl.multiple_of(first_blk * tq, tq)
        k = k_ref[pl.ds(start, n * tq), :].astype(BF16)
        v = v_ref[pl.ds(start, n * tq), :].astype(BF16)
        mask = None
        if masked:
            mask = lax.broadcasted_iota(jnp.int32, (rows, n * tq), 1) + start < q_pos
        c, a = _sb_block(qs, k, v, carry, acc, later_mat, mask)
        carry_scr[...] = c
        acc_scr[...] = a

    n_first = qi % nk + 1
    for n in range(1, nk + 1):
        @pl.when(n_first == n)
        def _():
            run(qi + 1 - n, n, jnp.zeros((rows, 1), F32), jnp.zeros((rows, HEAD_DIM), F32), masked=True)

    def body(i, _):
        run(qi + 1 - n_first - nk * (i + 1), nk, carry_scr[...], acc_scr[...], masked=False)
        return 0

    lax.fori_loop(0, (qi + 1 - n_first) // nk, body, 0)
    o = _unstack_heads(acc_scr[...], GROUP)
    o_ref[...] = (o * _silu(gate_ref[...])).astype(o_ref.dtype)


def sb_attention_prompt(proj, batch, seq, tq=128):
    assert seq % tq == 0 and seq >= SB_BLOCKS_PER_STEP * tq
    nq = seq // tq
    gw = GROUP * HEAD_DIM
    return pl.pallas_call(
        functools.partial(_sb_prompt_kernel, tq=tq),
        grid=(batch, N_KV_HEADS, nq),
        in_specs=[pl.BlockSpec((tq, gw), lambda b, h, i: (b * nq + i, h)),
                  pl.BlockSpec((seq, HEAD_DIM), lambda b, h, i: (b, K_COL + h)),
                  pl.BlockSpec((seq, HEAD_DIM), lambda b, h, i: (b, V_COL + h)),
                  pl.BlockSpec((tq, gw), lambda b, h, i: (b * nq + i, GATE_COL + h))],
        out_specs=pl.BlockSpec((tq, gw), lambda b, h, i: (b * nq + i, h)),
        out_shape=jax.ShapeDtypeStruct((batch * seq, ATTN_WIDTH), BF16),
        scratch_shapes=[pltpu.VMEM((GROUP * tq, 1), F32), pltpu.VMEM((GROUP * tq, HEAD_DIM), F32)],
        compiler_params=_cparams("parallel", "parallel", "parallel"),
        name="sb_attention_prompt",
    )(proj, proj, proj, proj)


def _pad_rows(x, rows):
    return jnp.concatenate([x, jnp.zeros((rows - x.shape[0], x.shape[1]), x.dtype)], axis=0)


PAGES_PER_STEP = 16


def _page_specs(n_pages, page_offset, latest_first):
    n_steps = n_pages // PAGES_PER_STEP

    def index_map(b, p, pt, *, j):
        step = n_steps - 1 - p if latest_first else p
        return (page_offset + pt[b, step * PAGES_PER_STEP + j], 0, 0)

    return [pl.BlockSpec((1, PAGE_SIZE * N_KV_HEADS, HEAD_DIM), functools.partial(index_map, j=j))
            for j in range(PAGES_PER_STEP)]


def _head_pages(page_refs, h):
    return jnp.concatenate([r[0, pl.ds(h, PAGE_SIZE, stride=N_KV_HEADS), :] for r in page_refs], axis=0)


def _flat_pool(cache):
    return cache.reshape(cache.shape[0] * cache.shape[1], PAGE_SIZE * N_KV_HEADS, HEAD_DIM)


def _sb_sample_kernel(pt_ref, proj_ref, *refs, tq):
    del pt_ref
    k_refs, v_refs = refs[:PAGES_PER_STEP], refs[PAGES_PER_STEP:2 * PAGES_PER_STEP]
    o_ref, carry_scr, acc_scr = refs[2 * PAGES_PER_STEP:]
    p = pl.program_id(1)
    rows = GROUP * tq
    later_mat = _later_matrix(PAGE_SIZE)

    def per_head(f):
        return jnp.stack([f(h) for h in range(N_KV_HEADS)])

    def q_rows(h):
        q = proj_ref[:, h * GROUP * HEAD_DIM:(h + 1) * GROUP * HEAD_DIM]
        return (_stack_heads(q, GROUP) * SCALE).astype(BF16)

    def new_rows(col):
        return per_head(lambda h: _pad_rows(proj_ref[:, col + h * HEAD_DIM:col + (h + 1) * HEAD_DIM],
                                            PAGE_SIZE).astype(BF16))

    qs = per_head(q_rows)

    @pl.when(p == 0)
    def _():
        t_loc = lax.broadcasted_iota(jnp.int32, (rows, PAGE_SIZE), 0) & (tq - 1)
        s_loc = lax.broadcasted_iota(jnp.int32, (rows, PAGE_SIZE), 1)
        c, a = _sb_block(qs, new_rows(ATTN_WIDTH), new_rows(ATTN_WIDTH + KV_WIDTH),
                         jnp.zeros((N_KV_HEADS, rows, 1), F32), jnp.zeros((N_KV_HEADS, rows, HEAD_DIM), F32),
                         later_mat, s_loc < t_loc)
        carry_scr[...] = c
        acc_scr[...] = a

    c, a = _sb_block(qs, per_head(lambda h: _head_pages(k_refs, h).astype(BF16)),
                     per_head(lambda h: _head_pages(v_refs, h).astype(BF16)),
                     carry_scr[...], acc_scr[...], later_mat)
    carry_scr[...] = c
    acc_scr[...] = a

    @pl.when(p == pl.num_programs(1) - 1)
    def _():
        gc = ATTN_WIDTH + 2 * KV_WIDTH
        for h in range(N_KV_HEADS):
            sl = slice(h * GROUP * HEAD_DIM, (h + 1) * GROUP * HEAD_DIM)
            gate = proj_ref[:, gc + sl.start:gc + sl.stop]
            o_ref[:, sl] = _unstack_heads(acc_scr[h], GROUP) * _silu(gate)


def sb_attention_sample(proj, k_pool, v_pool, page_offset, page_table, tq):
    batch, n_pages = page_table.shape
    assert n_pages % PAGES_PER_STEP == 0
    rows = GROUP * tq
    specs = _page_specs(n_pages, page_offset, latest_first=True)
    return pl.pallas_call(
        functools.partial(_sb_sample_kernel, tq=tq),
        grid_spec=pltpu.PrefetchScalarGridSpec(
            num_scalar_prefetch=1,
            grid=(batch, n_pages // PAGES_PER_STEP),
            in_specs=[pl.BlockSpec((tq, ATTN_IN), lambda b, p, pt: (b, 0))] + specs + specs,
            out_specs=pl.BlockSpec((tq, ATTN_WIDTH), lambda b, p, pt: (b, 0)),
            scratch_shapes=[pltpu.VMEM((N_KV_HEADS, rows, 1), F32),
                            pltpu.VMEM((N_KV_HEADS, rows, HEAD_DIM), F32)]),
        out_shape=jax.ShapeDtypeStruct((batch * tq, ATTN_WIDTH), F32),
        compiler_params=_cparams("parallel", "arbitrary"),
        name="sb_attention_sample",
    )(page_table, proj, *([k_pool] * PAGES_PER_STEP), *([v_pool] * PAGES_PER_STEP))


GDN_ROWS = 128
GDN_REP = GDN_V_HEADS // GDN_K_HEADS


def _gdn_delta_kernel(q_ref, k_ref, v_ref, bd_ref, z_ref, cq_ref, ck_ref, cv_ref, wq_ref, wk_ref, wv_ref,
                      alog_ref, dt_ref, onorm_ref, s0_ref,
                      o_ref, s_ref, u_scr, w_scr, in_scr, qd_scr, kdt_scr, gl_scr, gt_scr, *, seq):
    hk = pl.program_id(1)
    c = GDN_ROWS
    n_chunks = max(seq // c, 1)
    unroll = gt_scr.shape[0]
    n_doublings = max(math.ceil(math.log2(min(seq, c))) - 1, 0)
    ri = lax.broadcasted_iota(jnp.int32, (c, c), 0)
    ci = lax.broadcasted_iota(jnp.int32, (c, c), 1)
    lane = lax.broadcasted_iota(jnp.int32, (c, LANES), 1)
    row = lax.broadcasted_iota(jnp.int32, (c, 1), 0)
    tril = ri >= ci
    tril_b = jnp.where(tril, 1.0, 0.0).astype(BF16)
    neg_a = -jnp.exp(alog_ref[...])

    def chunk_rows(i):
        return pl.ds(i * c, c) if isinstance(i, int) else pl.ds(pl.multiple_of(i * c, c), c)

    def rows(ref, i):
        if seq < c:
            return _pad_rows(ref[...], c)
        return ref[chunk_rows(i), :]

    def conv_silu(ref, prev_ref, w_ref, i):
        n = min(seq, c)
        if isinstance(i, int) and i == 0:
            ext = jnp.concatenate([prev_ref[0], ref[pl.ds(0, n), :]], axis=0)
        elif isinstance(i, int):
            ext = ref[pl.ds(i * c - SUBLANES, c + SUBLANES), :]
        else:
            ext = ref[pl.ds(pl.multiple_of(i * c - SUBLANES, SUBLANES), c + SUBLANES), :]
        w = w_ref[...]
        y = ext[SUBLANES:] * w[GDN_CONV - 1:GDN_CONV, :]
        for tap in range(1, GDN_CONV):
            y = y + pltpu.roll(ext, tap, 0)[SUBLANES:] * w[GDN_CONV - 1 - tap:GDN_CONV - tap, :]
        y = _silu(y)
        return y if n == c else _pad_rows(y, c)

    def l2_normalised(x, scale):
        return x * (lax.rsqrt(jnp.sum(x * x, axis=-1, keepdims=True) + EPS) * scale)

    def prepare(i, _):
        ms, rhss = [], []
        for j in range(unroll):
            ch = i * unroll + j
            sl = chunk_rows(ch)
            q = l2_normalised(conv_silu(q_ref, cq_ref, wq_ref, ch), GDN_HEAD_DIM ** -0.5)
            k = l2_normalised(conv_silu(k_ref, ck_ref, wk_ref, ch), 1.0)
            v2 = conv_silu(v_ref, cv_ref, wv_ref, ch)
            raw = rows(bd_ref, ch)
            k16 = k.astype(BF16)
            kk = _dot_nt(k16, k16)
            qk = _dot_nt(q.astype(BF16), k16)
            sig = jax.nn.sigmoid(raw)
            g_all = jnp.where(row < seq, neg_a * _softplus(raw + dt_ref[...]), 0.0)
            g_hi = g_all.astype(BF16)
            g_mid, g_lo = _split_bf16(g_all - g_hi.astype(F32))
            gcx = _dot(tril_b, g_hi) + (_dot(tril_b, g_mid) + _dot(tril_b, g_lo))
            gt_scr[j] = gcx.T
            for e in range(GDN_REP):
                hv = hk * GDN_REP + e
                beta = jnp.sum(jnp.where(lane == hv, sig, 0.0), axis=1, keepdims=True)
                gc = jnp.sum(jnp.where(lane == hv + GDN_V_HEADS, gcx, 0.0), axis=1, keepdims=True)
                g_row = gt_scr[j, pl.ds(hv + GDN_V_HEADS, 1), :]
                decay = jnp.exp(jnp.where(tril, gc - g_row, NEG))
                ms.append(-jnp.where(ri > ci, kk * beta * decay, 0.0))
                egc = jnp.exp(gc)
                v = v2[:, e * GDN_HEAD_DIM:(e + 1) * GDN_HEAD_DIM]
                rhss.append(jnp.concatenate([v * beta, k * (beta * egc)], axis=1))
                g_last = gc[c - 1:c, :]
                in_scr[e, sl, :] = (qk * decay).astype(BF16)
                qd_scr[e, sl, :] = (q * egc).astype(BF16)
                kdt_scr[e, ch] = (k * jnp.exp(g_last - gc)).T.astype(BF16)
                gl_scr[e, ch] = jnp.broadcast_to(jnp.exp(g_last), (SUBLANES, LANES))
        nmat = jnp.stack(ms)
        pw = _split_bf16(nmat)
        for _ in range(n_doublings):
            sq = _bdot_split(pw, pw)
            pw = _split_bf16(sq)
            nmat = nmat + sq + _bdot_split(_split_bf16(nmat), pw)
        rhs = jnp.stack(rhss)
        sol = rhs + _bdot(nmat.astype(BF16), rhs.astype(BF16))
        for j in range(unroll):
            sl = chunk_rows(i * unroll + j)
            for e in range(GDN_REP):
                x = sol[j * GDN_REP + e]
                u_scr[e, sl, :] = x[:, :GDN_HEAD_DIM]
                w_scr[e, sl, :] = x[:, GDN_HEAD_DIM:].astype(BF16)
        return 0

    def advance(i, states):
        sl = chunk_rows(i)
        z2 = rows(z_ref, i)
        out = []
        for e in range(GDN_REP):
            s = states[e]
            s16 = s.astype(BF16)
            v_new = u_scr[e, sl, :] - _dot(w_scr[e, sl, :], s16)
            v16 = v_new.astype(BF16)
            o = _dot(qd_scr[e, sl, :], s16) + _dot(in_scr[e, sl, :], v16)
            out.append(s * gl_scr[e, i][0:1, :] + _dot(kdt_scr[e, i], v16))
            ms = jnp.mean(o * o, axis=-1, keepdims=True)
            z = z2[:, e * GDN_HEAD_DIM:(e + 1) * GDN_HEAD_DIM]
            o = (o * lax.rsqrt(ms + EPS) * onorm_ref[...] * _silu(z)).astype(o_ref.dtype)
            cols = slice(e * GDN_HEAD_DIM, (e + 1) * GDN_HEAD_DIM)
            if seq < c:
                o_ref[:, cols] = o[:seq]
            else:
                o_ref[sl, cols] = o
        return tuple(out)

    states = tuple(s0_ref[0, e] for e in range(GDN_REP))
    if n_chunks == 1:
        prepare(0, 0)
        states = advance(0, states)
    else:
        n_groups = n_chunks // unroll
        prepare(0, 0)

        def body(i, states):
            for j in range(unroll):
                states = advance((i - 1) * unroll + j, states)
            prepare(i, 0)
            return states

        states = lax.fori_loop(1, n_groups, body, states)
        for j in range(unroll):
            states = advance((n_groups - 1) * unroll + j, states)
    for e in range(GDN_REP):
        s_ref[0, e] = states[e]


def gdn_delta(proj, conv_state, conv_w, a_log, dt_bias, o_norm, s0, batch, seq, out_dtype):
    assert (seq % GDN_ROWS == 0 or seq < GDN_ROWS) and seq % SUBLANES == 0
    prev = jnp.pad(conv_state, ((0, 0), (SUBLANES - (GDN_CONV - 1), 0), (0, 0)))
    vcol = 2 * GDN_K_HEADS // GDN_REP
    pad = jnp.zeros((GDN_V_HEADS,), F32)
    tail = jnp.zeros((LANES - 2 * GDN_V_HEADS,), F32)
    alog = jnp.concatenate([pad, a_log, tail]).reshape(1, LANES)
    dt = jnp.concatenate([pad, dt_bias, tail]).reshape(1, LANES)
    blk = (seq, GDN_HEAD_DIM)
    wide = (seq, GDN_REP * GDN_HEAD_DIM)
    state_blk = (1, GDN_REP, GDN_HEAD_DIM, GDN_HEAD_DIM)
    rows = max(seq, GDN_ROWS)
    n_chunks = rows // GDN_ROWS
    return pl.pallas_call(
        functools.partial(_gdn_delta_kernel, seq=seq),
        grid=(batch, GDN_K_HEADS),
        in_specs=[pl.BlockSpec(blk, lambda b, h: (b, h)),
                  pl.BlockSpec(blk, lambda b, h: (b, GDN_K_HEADS + h)),
                  pl.BlockSpec(wide, lambda b, h: (b, vcol + h)),
                  pl.BlockSpec((seq, LANES), lambda b, h: (b, GDN_BD_COL)),
                  pl.BlockSpec(wide, lambda b, h: (b, GDN_Z_COL // GDN_REP + h)),
                  pl.BlockSpec((1, SUBLANES, blk[1]), lambda b, h: (b, 0, h)),
                  pl.BlockSpec((1, SUBLANES, blk[1]), lambda b, h: (b, 0, GDN_K_HEADS + h)),
                  pl.BlockSpec((1, SUBLANES, wide[1]), lambda b, h: (b, 0, vcol + h)),
                  pl.BlockSpec((GDN_CONV, blk[1]), lambda b, h: (0, h)),
                  pl.BlockSpec((GDN_CONV, blk[1]), lambda b, h: (0, GDN_K_HEADS + h)),
                  pl.BlockSpec((GDN_CONV, wide[1]), lambda b, h: (0, vcol + h)),
                  pl.BlockSpec((1, LANES), lambda b, h: (0, 0)),
                  pl.BlockSpec((1, LANES), lambda b, h: (0, 0)),
                  pl.BlockSpec((1, GDN_HEAD_DIM), lambda b, h: (0, 0)),
                  pl.BlockSpec(state_blk, lambda b, h: (b, h, 0, 0))],
        out_specs=[pl.BlockSpec(wide, lambda b, h: (b, h)),
                   pl.BlockSpec(state_blk, lambda b, h: (b, h, 0, 0))],
        scratch_shapes=[pltpu.VMEM((GDN_REP, rows, GDN_HEAD_DIM), F32),
                        pltpu.VMEM((GDN_REP, rows, GDN_HEAD_DIM), BF16),
                        pltpu.VMEM((GDN_REP, rows, GDN_ROWS), BF16),
                        pltpu.VMEM((GDN_REP, rows, GDN_HEAD_DIM), BF16),
                        pltpu.VMEM((GDN_REP, n_chunks, GDN_HEAD_DIM, GDN_ROWS), BF16),
                        pltpu.VMEM((GDN_REP, n_chunks, SUBLANES, LANES), F32),
                        pltpu.VMEM((2 if n_chunks % 2 == 0 else 1, GDN_ROWS, LANES), F32)],
        out_shape=[jax.ShapeDtypeStruct((batch * seq, GDN_VAL_WIDTH), out_dtype),
                   jax.ShapeDtypeStruct((batch, GDN_V_HEADS, GDN_HEAD_DIM, GDN_HEAD_DIM), F32)],
        compiler_params=_cparams("parallel", "parallel"),
        name="gdn_delta",
    )(proj, proj, proj, proj, proj, prev, prev, prev, conv_w, conv_w, conv_w,
      alog, dt, o_norm.reshape(1, GDN_HEAD_DIM), s0)


def _rel_bucket(dist):
    max_exact = REL_BUCKETS // 2
    n = jnp.maximum(dist, 0)
    large = max_exact + (jnp.log(jnp.maximum(n, 1).astype(F32) / max_exact)
                         / math.log(REL_MAX_DIST / max_exact) * (REL_BUCKETS - max_exact)).astype(jnp.int32)
    large = jnp.minimum(large, REL_BUCKETS - 1)
    return jnp.where(n < max_exact, n, large)


def _bias_table_kernel(rbt_ref, onehot_ref, o_ref):
    o_ref[...] = _dot(rbt_ref[...], onehot_ref[...], HI)


def bias_by_distance(rel_bias, dist):
    bucket = _rel_bucket(dist)
    onehot = (bucket[None, :] == jnp.arange(REL_BUCKETS)[:, None]) & (dist[None, :] >= 0)
    return pl.pallas_call(
        _bias_table_kernel,
        out_shape=jax.ShapeDtypeStruct((N_HEADS, dist.shape[0]), F32),
        compiler_params=pltpu.CompilerParams(vmem_limit_bytes=VMEM_LIMIT),
        name="bias_table",
    )(rel_bias.T, onehot.astype(F32))


def _toeplitz(window_row, rows, shift):
    x = jnp.broadcast_to(window_row, (rows, window_row.shape[1]))
    return pltpu.roll(x, shift, 1, stride=1, stride_axis=0)


def _select_topk(gate, n_valid, n_blocks):
    lane = lax.broadcasted_iota(jnp.int32, gate.shape, 1)
    valid = lane < n_valid
    gm = jnp.where(valid, gate, -jnp.inf)
    cnt = jnp.zeros(gate.shape, jnp.int32)
    for m in range(n_blocks):
        col = gm[:, m:m + 1]
        beats = (col > gm) | ((col == gm) & (lane > m))
        cnt = cnt + jnp.where(beats, 1, 0)
    return jnp.where(valid & (cnt < MOBA_TOPK), 1.0, 0.0)


def _select_topk_t(gate_t, n_valid):
    blk = lax.broadcasted_iota(jnp.int32, gate_t.shape, 0)
    valid = blk < n_valid
    gm = jnp.where(valid, gate_t, -jnp.inf)
    cnt = jnp.zeros(gate_t.shape, jnp.int32)
    for m in range(gate_t.shape[0]):
        row = gm[m:m + 1, :]
        beats = (row > gm) | ((row == gm) & (blk > m))
        cnt = cnt + jnp.where(beats, 1, 0)
    return jnp.where(valid & (cnt < MOBA_TOPK), 1.0, 0.0)


def _moba_tile(qs, k, v, bias, mask, m, l, acc):
    batched = qs.ndim == 3
    s = jnp.einsum("hrd,hsd->hrs", qs, k, preferred_element_type=F32) if batched else _dot_nt(qs, k)
    s = jnp.where(mask, s * SCALE + bias, NEG)
    m_new = jnp.maximum(m, jnp.max(s, axis=-1, keepdims=True))
    p = jnp.where(mask, jnp.exp(s - m_new), 0.0)
    alpha = jnp.exp(m - m_new)
    l = alpha * l + jnp.sum(p, axis=-1, keepdims=True)
    p = p.astype(BF16)
    pv = jnp.einsum("hrs,hsd->hrd", p, v, preferred_element_type=F32) if batched else _dot(p, v)
    return m_new, l, alpha * acc + pv


def _moba_prompt_kernel(q_ref, k_ref, v_ref, gate_ref, tb_ref, o_ref,
                        means_scr, bias_scr, pen_scr, m_scr, l_scr, acc_scr, *, tq, n_blocks):
    h = pl.program_id(0)
    b = pl.program_id(1)
    qi = pl.program_id(2)
    rows = GROUP * tq

    @pl.when(qi == 0)
    def _():
        means_scr[...] = jnp.zeros_like(means_scr)
        means_scr[0:n_blocks, :] = jnp.mean(k_ref[...].reshape(n_blocks, MOBA_BLOCK, HEAD_DIM), axis=1)

    @pl.when(b == 0)
    def _():
        for g in range(GROUP):
            wrow = tb_ref[pl.ds((h * GROUP + g) * n_blocks + qi, 1), :]
            bias_scr[qi, pl.ds(g * tq, tq), :] = _toeplitz(wrow, tq, tq + 1)[:, :tq]

    qs32 = _stack_heads(q_ref[...], GROUP)
    qs = (qs32 * SCALE).astype(BF16)
    sel_t = _select_topk_t(_dot_nt(means_scr[...], qs32, HI), qi)
    pen_t = (sel_t - 1.0) * -NEG
    pen_scr[...] = jnp.concatenate([pen_t, jnp.zeros((LANES - pen_t.shape[0], rows), F32)], axis=0).T
    lane = lax.broadcasted_iota(jnp.int32, (rows, LANES), 1)
    t_loc = lax.broadcasted_iota(jnp.int32, (rows, tq), 0) & (tq - 1)
    s_loc = lax.broadcasted_iota(jnp.int32, (rows, tq), 1)

    def update(s, v, m, l, acc):
        m_new = jnp.maximum(m, jnp.max(s, axis=1, keepdims=True))
        p = jnp.exp(s - m_new)
        alpha = jnp.exp(m - m_new)
        m_scr[...] = m_new
        l_scr[...] = alpha * l + jnp.sum(p, axis=1, keepdims=True)
        acc_scr[...] = alpha * acc + _dot(p.astype(BF16), v)

    def penalty(kb):
        return jnp.sum(jnp.where(lane == kb, pen_scr[...], 0.0), axis=1, keepdims=True)

    def first(n):
        blk = qi + 1 - n
        start = pl.multiple_of(blk * tq, tq)
        z = _dot_nt(qs, k_ref[pl.ds(start, n * tq), :].astype(BF16))
        s = jnp.where(s_loc <= t_loc, z[:, (n - 1) * tq:] + bias_scr[0], NEG)
        if n == 2:
            s = jnp.concatenate([z[:, :tq] + bias_scr[1] + penalty(blk), s], axis=1)
        update(s, v_ref[pl.ds(start, n * tq), :].astype(BF16), jnp.full((rows, 1), NEG, F32),
               jnp.zeros((rows, 1), F32), jnp.zeros((rows, HEAD_DIM), F32))

    for n in (1, 2):
        @pl.when(qi % 2 == n - 1)
        def _():
            first(n)

    def pair(i, _):
        kb = 2 * i
        start = pl.multiple_of(kb * tq, tq)
        z = _dot_nt(qs, k_ref[pl.ds(start, 2 * tq), :].astype(BF16))
        s = jnp.concatenate([z[:, :tq] + bias_scr[qi - kb] + penalty(kb),
                             z[:, tq:] + bias_scr[qi - kb - 1] + penalty(kb + 1)], axis=1)
        update(s, v_ref[pl.ds(start, 2 * tq), :].astype(BF16), m_scr[...], l_scr[...], acc_scr[...])
        return 0

    lax.fori_loop(0, qi // 2, pair, 0)
    o = _unstack_heads(acc_scr[...] / l_scr[...], GROUP)
    o_ref[...] = (o * _silu(gate_ref[...])).astype(o_ref.dtype)


def moba_attention_prompt(proj, rel_bias, batch, seq):
    tq = MOBA_BLOCK
    assert seq % tq == 0
    nb = seq // tq
    gw = GROUP * HEAD_DIM
    rows = GROUP * tq
    dist = (jnp.arange(nb)[:, None] * tq + (tq - 1) - jnp.arange(2 * tq)[None, :]).reshape(-1)
    table = bias_by_distance(rel_bias, dist).reshape(N_HEADS * nb, 2 * tq)
    return pl.pallas_call(
        functools.partial(_moba_prompt_kernel, tq=tq, n_blocks=nb),
        grid=(N_KV_HEADS, batch, nb),
        in_specs=[pl.BlockSpec((tq, gw), lambda h, b, i: (b * nb + i, h)),
                  pl.BlockSpec((seq, HEAD_DIM), lambda h, b, i: (b, K_COL + h)),
                  pl.BlockSpec((seq, HEAD_DIM), lambda h, b, i: (b, V_COL + h)),
                  pl.BlockSpec((tq, gw), lambda h, b, i: (b * nb + i, GATE_COL + h)),
                  pl.BlockSpec((N_HEADS * nb, 2 * tq), lambda h, b, i: (0, 0))],
        out_specs=pl.BlockSpec((tq, gw), lambda h, b, i: (b * nb + i, h)),
        out_shape=jax.ShapeDtypeStruct((batch * seq, ATTN_WIDTH), BF16),
        scratch_shapes=[pltpu.VMEM((-(-nb // SUBLANES) * SUBLANES, HEAD_DIM), F32),
                        pltpu.VMEM((nb, rows, tq), F32),
                        pltpu.VMEM((rows, LANES), F32),
                        pltpu.VMEM((rows, 1), F32),
                        pltpu.VMEM((rows, 1), F32),
                        pltpu.VMEM((rows, HEAD_DIM), F32)],
        compiler_params=_cparams("arbitrary", "arbitrary", "arbitrary"),
        name="moba_attention_prompt",
    )(proj, proj, proj, proj, table)


PAGES_PER_BLOCK = MOBA_BLOCK // PAGE_SIZE
BLOCKS_PER_STEP = PAGES_PER_STEP // PAGES_PER_BLOCK


def _moba_means_kernel(pt_ref, *refs):
    del pt_ref
    page_refs, o_ref = refs[:-1], refs[-1]
    for blk in range(BLOCKS_PER_STEP):
        pages = page_refs[blk * PAGES_PER_BLOCK:(blk + 1) * PAGES_PER_BLOCK]
        for h in range(N_KV_HEADS):
            total = jnp.sum(_head_pages(pages, h), axis=0, keepdims=True)
            o_ref[0, blk, :, h * HEAD_DIM:(h + 1) * HEAD_DIM] = total * (1.0 / MOBA_BLOCK)


def moba_block_means(k_pool, page_offset, page_table):
    batch, n_pages = page_table.shape
    assert n_pages % PAGES_PER_STEP == 0
    nb = n_pages // PAGES_PER_BLOCK
    return pl.pallas_call(
        _moba_means_kernel,
        grid_spec=pltpu.PrefetchScalarGridSpec(
            num_scalar_prefetch=1,
            grid=(batch, n_pages // PAGES_PER_STEP),
            in_specs=_page_specs(n_pages, page_offset, latest_first=False),
            out_specs=pl.BlockSpec((1, BLOCKS_PER_STEP, 1, KV_WIDTH), lambda b, n, pt: (b, n, 0, 0))),
        out_shape=jax.ShapeDtypeStruct((batch, nb, 1, KV_WIDTH), F32),
        compiler_params=_cparams("parallel", "parallel"),
        name="moba_block_means",
    )(page_table, *([k_pool] * PAGES_PER_STEP))


def _moba_sample_kernel(pt_ref, proj_ref, means_ref, tb_ref, *refs, tq, n_pages):
    del pt_ref
    k_refs, v_refs = refs[:PAGES_PER_STEP], refs[PAGES_PER_STEP:2 * PAGES_PER_STEP]
    o_ref, sel_scr, m_scr, l_scr, acc_scr = refs[2 * PAGES_PER_STEP:]
    p = pl.program_id(1)
    rows = GROUP * tq
    n_past_blocks = n_pages // PAGES_PER_BLOCK
    lane = lax.broadcasted_iota(jnp.int32, (N_KV_HEADS, rows, LANES), 2)

    def q_rows(h):
        return _stack_heads(proj_ref[:, h * GROUP * HEAD_DIM:(h + 1) * GROUP * HEAD_DIM], GROUP)

    def bias_rows(h, page):
        tiles = []
        for g in range(GROUP):
            wrow = tb_ref[pl.ds((h * GROUP + g) * (n_pages + 1) + page, 1), :]
            tiles.append(_toeplitz(wrow, tq, PAGE_SIZE + 1)[:, :PAGE_SIZE])
        return jnp.concatenate(tiles, axis=0)

    def per_head(f):
        return jnp.stack([f(h) for h in range(N_KV_HEADS)])

    def new_rows(col):
        return per_head(lambda h: _pad_rows(proj_ref[:, col + h * HEAD_DIM:col + (h + 1) * HEAD_DIM],
                                            PAGE_SIZE).astype(BF16))

    qs = per_head(lambda h: q_rows(h).astype(BF16))

    @pl.when(p == 0)
    def _():
        t_loc = lax.broadcasted_iota(jnp.int32, (rows, PAGE_SIZE), 0) & (tq - 1)
        s_loc = lax.broadcasted_iota(jnp.int32, (rows, PAGE_SIZE), 1)
        for h in range(N_KV_HEADS):
            means = _pad_rows(means_ref[0, :, h * HEAD_DIM:(h + 1) * HEAD_DIM], LANES)
            sel_scr[h] = _select_topk(_dot_nt(q_rows(h), means, HI), n_past_blocks, n_past_blocks)
        m, l, acc = _moba_tile(qs, new_rows(ATTN_WIDTH), new_rows(ATTN_WIDTH + KV_WIDTH),
                               per_head(lambda h: bias_rows(h, n_pages)), s_loc <= t_loc,
                               jnp.full((N_KV_HEADS, rows, 1), NEG, F32), jnp.zeros((N_KV_HEADS, rows, 1), F32),
                               jnp.zeros((N_KV_HEADS, rows, HEAD_DIM), F32))
        m_scr[...] = m
        l_scr[...] = l
        acc_scr[...] = acc

    sel = sel_scr[...]
    mask = []
    for j in range(BLOCKS_PER_STEP):
        col = jnp.sum(jnp.where(lane == p * BLOCKS_PER_STEP + j, sel, 0.0), axis=-1, keepdims=True) > 0.5
        mask.append(jnp.broadcast_to(col, (N_KV_HEADS, rows, MOBA_BLOCK)))
    bias = per_head(lambda h: jnp.concatenate([bias_rows(h, p * PAGES_PER_STEP + j)
                                               for j in range(PAGES_PER_STEP)], axis=1))
    m, l, acc = _moba_tile(qs, per_head(lambda h: _head_pages(k_refs, h).astype(BF16)),
                           per_head(lambda h: _head_pages(v_refs, h).astype(BF16)), bias,
                           jnp.concatenate(mask, axis=-1), m_scr[...], l_scr[...], acc_scr[...])
    m_scr[...] = m
    l_scr[...] = l
    acc_scr[...] = acc

    @pl.when(p == pl.num_programs(1) - 1)
    def _():
        gc = ATTN_WIDTH + 2 * KV_WIDTH
        for h in range(N_KV_HEADS):
            sl = slice(h * GROUP * HEAD_DIM, (h + 1) * GROUP * HEAD_DIM)
            gate = proj_ref[:, gc + sl.start:gc + sl.stop]
            o_ref[:, sl] = _unstack_heads(acc_scr[h] / l_scr[h], GROUP) * _silu(gate)


def moba_attention_sample(proj, k_pool, v_pool, page_offset, page_table, rel_bias, tq):
    batch, n_pages = page_table.shape
    assert n_pages % PAGES_PER_STEP == 0 and tq <= MOBA_BLOCK
    rows = GROUP * tq
    nb = n_pages // PAGES_PER_BLOCK
    assert nb <= LANES
    means = moba_block_means(k_pool, page_offset, page_table).reshape(batch, nb, KV_WIDTH)
    dist = ((n_pages - jnp.arange(n_pages + 1))[:, None] * PAGE_SIZE + (PAGE_SIZE - 1)
            - jnp.arange(2 * PAGE_SIZE)[None, :]).reshape(-1)
    table = bias_by_distance(rel_bias, dist).reshape(N_HEADS * (n_pages + 1), 2 * PAGE_SIZE)
    specs = _page_specs(n_pages, page_offset, latest_first=False)
    return pl.pallas_call(
        functools.partial(_moba_sample_kernel, tq=tq, n_pages=n_pages),
        grid_spec=pltpu.PrefetchScalarGridSpec(
            num_scalar_prefetch=1,
            grid=(batch, n_pages // PAGES_PER_STEP),
            in_specs=[pl.BlockSpec((tq, ATTN_IN), lambda b, p, pt: (b, 0)),
                      pl.BlockSpec((1, nb, KV_WIDTH), lambda b, p, pt: (b, 0, 0)),
                      pl.BlockSpec((N_HEADS * (n_pages + 1), 2 * PAGE_SIZE), lambda b, p, pt: (0, 0))]
            + specs + specs,
            out_specs=pl.BlockSpec((tq, ATTN_WIDTH), lambda b, p, pt: (b, 0)),
            scratch_shapes=[pltpu.VMEM((N_KV_HEADS, rows, LANES), F32),
                            pltpu.VMEM((N_KV_HEADS, rows, 1), F32),
                            pltpu.VMEM((N_KV_HEADS, rows, 1), F32),
                            pltpu.VMEM((N_KV_HEADS, rows, HEAD_DIM), F32)]),
        out_shape=jax.ShapeDtypeStruct((batch * tq, ATTN_WIDTH), F32),
        compiler_params=_cparams("parallel", "arbitrary"),
        name="moba_attention_sample",
    )(page_table, proj, means, table, *([k_pool] * PAGES_PER_STEP), *([v_pool] * PAGES_PER_STEP))


N_MIXERS = 3
GDN_IN_PADDED = -(-GDN_IN // 512) * 512


def _new_kv(proj, batch, seq):
    k = proj[:, ATTN_WIDTH:ATTN_WIDTH + KV_WIDTH].reshape(batch, seq, N_KV_HEADS, HEAD_DIM)
    v = proj[:, ATTN_WIDTH + KV_WIDTH:ATTN_WIDTH + 2 * KV_WIDTH].reshape(batch, seq, N_KV_HEADS, HEAD_DIM)
    return k, v


def kernel(x_prompt, x_sample, cache_sb_k, cache_sb_v, state_gdn_conv, state_gdn_rec, cache_moba_k, cache_moba_v, page_table, norm_g, sb_w_in, sb_w_out, gdn_w_in, gdn_conv_w, gdn_a_log, gdn_dt_bias, gdn_o_norm, gdn_w_out, moba_w_in, moba_q_norm, moba_k_norm, moba_w_out, rel_bias):
    bp, tp, d = x_prompt.shape
    bs, ts, _ = x_sample.shape
    n_phys = cache_sb_k.shape[1]
    yp = x_prompt.reshape(bp * tp, d)
    ys = x_sample.reshape(bs * ts, d)
    outs = {name: [] for name in ("sb_kp", "sb_vp", "sb_ks", "sb_vs", "gdn_cp", "gdn_sp", "gdn_cs", "gdn_ss",
                                  "mb_kp", "mb_vp", "mb_ks", "mb_vs")}
    for layer in range(norm_g.shape[0]):
        kind = layer % N_MIXERS
        j = layer // N_MIXERS
        g = norm_g[layer]
        if kind == 0:
            w_in = sb_w_in[j].astype(BF16)
            w_out = sb_w_out[j].astype(BF16)
            pp = norm_matmul(yp, g, w_in)
            ps = norm_matmul(ys, g, w_in)
            op = sb_attention_prompt(pp, bp, tp)
            os_ = sb_attention_sample(ps, _flat_pool(cache_sb_k), _flat_pool(cache_sb_v), j * n_phys, page_table, ts)
            kp, vp = _new_kv(pp, bp, tp)
            ks, vs = _new_kv(ps, bs, ts)
            outs["sb_kp"].append(kp); outs["sb_vp"].append(vp); outs["sb_ks"].append(ks); outs["sb_vs"].append(vs)
        elif kind == 1:
            w_in = jnp.pad(gdn_w_in[j], ((0, 0), (0, GDN_IN_PADDED - GDN_IN))).astype(BF16)
            w_out = gdn_w_out[j].astype(BF16)
            pp = norm_matmul(yp, g, w_in)
            ps = norm_matmul(ys, g, w_in)
            conv0 = jnp.zeros((bp, GDN_CONV - 1, GDN_CONV_CH), F32)
            s0 = jnp.zeros((bp,) + state_gdn_rec.shape[2:], F32)
            op, sp = gdn_delta(pp, conv0, gdn_conv_w[j], gdn_a_log[j], gdn_dt_bias[j], gdn_o_norm[j], s0,
                               bp, tp, BF16)
            os_, ss = gdn_delta(ps, state_gdn_conv[j], gdn_conv_w[j], gdn_a_log[j], gdn_dt_bias[j],
                                gdn_o_norm[j], state_gdn_rec[j], bs, ts, F32)
            outs["gdn_cp"].append(pp.reshape(bp, tp, -1)[:, tp - (GDN_CONV - 1):, :GDN_CONV_CH])
            outs["gdn_cs"].append(ps.reshape(bs, ts, -1)[:, ts - (GDN_CONV - 1):, :GDN_CONV_CH])
            outs["gdn_sp"].append(sp); outs["gdn_ss"].append(ss)
        else:
            w_in = moba_w_in[j].astype(BF16)
            w_out = moba_w_out[j].astype(BF16)
            head_gain = jnp.concatenate([jnp.tile(moba_q_norm[j], N_HEADS), jnp.tile(moba_k_norm[j], N_KV_HEADS),
                                         jnp.ones((ATTN_IN - ATTN_WIDTH - KV_WIDTH,), F32)]).reshape(1, ATTN_IN)
            pp = norm_matmul(yp, g, w_in, head_gain, ATTN_WIDTH + KV_WIDTH)
            ps = norm_matmul(ys, g, w_in, head_gain, ATTN_WIDTH + KV_WIDTH)
            op = moba_attention_prompt(pp, rel_bias, bp, tp)
            os_ = moba_attention_sample(ps, _flat_pool(cache_moba_k), _flat_pool(cache_moba_v), j * n_phys,
                                        page_table, rel_bias, ts)
            kp, vp = _new_kv(pp, bp, tp)
            ks, vs = _new_kv(ps, bs, ts)
            outs["mb_kp"].append(kp); outs["mb_vp"].append(vp); outs["mb_ks"].append(ks); outs["mb_vs"].append(vs)
        yp = matmul_residual(op, w_out, yp)
        ys = matmul_residual(os_, w_out, ys)
    stack = lambda name: jnp.stack(outs[name])
    return (yp.reshape(bp, tp, d), ys.reshape(bs, ts, d),
            stack("sb_kp"), stack("sb_vp"), stack("sb_ks"), stack("sb_vs"),
            stack("gdn_cp"), stack("gdn_sp"), stack("gdn_cs"), stack("gdn_ss"),
            stack("mb_kp"), stack("mb_vp"), stack("mb_ks"), stack("mb_vs"))
```
